```python
import math
import jax, jax.numpy as jnp
from jax import lax
import numpy as np

D_MODEL = 1024
BATCH = 32
SEQ = 2048
DEPTH = 2
DEC_BATCH = 8
DEC_SEQ = 4096
PAST_LEN = 128

N_META = 16
NORM_EPS = 1e-6
N_BRANCH = 4
BRANCH_W = 512
FNET_GROUPS = 4
FNET_GW = BRANCH_W // FNET_GROUPS
SSM_HEAD_DIM = 64
SSM_HEADS = BRANCH_W // SSM_HEAD_DIM
SSM_GROUPS = 2
SSM_HPG = SSM_HEADS // SSM_GROUPS
SSM_STATE = 128
SSM_CHUNK = 128
SSM_CONV = 3
SSM_XBC = BRANCH_W + 2 * SSM_GROUPS * SSM_STATE
HYENA_ORDER = 2
HYENA_EMB = 33
HYENA_FFN = 64
HYENA_CONV = 3
HYENA_TARGET = 1e-2
HYENA_FAST = 0.3
HYENA_SLOW = 1.5
SC_CONV = 3
D_FF = 2816
FFN_CONV = 3

COL_FNET = BRANCH_W
COL_SSM = BRANCH_W + SSM_XBC + 2 * SSM_HEADS
COL_HYENA = (HYENA_ORDER + 1) * BRANCH_W
COL_SC = 3 * BRANCH_W
COL_GATE = N_BRANCH * D_MODEL
D_IN_PROJ = COL_FNET + COL_SSM + COL_HYENA + COL_SC + COL_GATE
SPLIT_IN = (COL_FNET, COL_FNET + COL_SSM, COL_FNET + COL_SSM + COL_HYENA, COL_FNET + COL_SSM + COL_HYENA + COL_SC)

kernel_name = 'hybrid_bidir_gated_encoder'


def rms_norm(x, w):
    xf = x.astype(jnp.float32)
    y = xf * lax.rsqrt(jnp.mean(xf * xf, axis=-1, keepdims=True) + NORM_EPS)
    return (y * w.astype(jnp.float32)).astype(x.dtype)


def dwconv_centred(u, w):
    k = w.shape[0]
    p = k // 2
    t = u.shape[1]
    up = jnp.pad(u, ((0, 0), (p, p), (0, 0)))
    out = up[:, 0:t] * w[0]
    for j in range(1, k):
        out = out + up[:, j:j + t] * w[j]
    return out


def _pad_time(a, front, back):
    return jnp.pad(a, [(0, 0), (front, back)] + [(0, 0)] * (a.ndim - 2))


def fnet_branch(u):
    b, t, _ = u.shape
    g = u.astype(jnp.float32).reshape(b, t, FNET_GROUPS, FNET_GW)
    y = jnp.fft.fft2(g, axes=(1, 3), norm='ortho').real
    return y.reshape(b, t, BRANCH_W).astype(u.dtype)


def ssd_scan(x, dt, a, bm, cm):
    b, t, g, r, p = x.shape
    n = bm.shape[-1]
    c = t // SSM_CHUNK
    xc = (x * dt[..., None]).reshape(b, c, SSM_CHUNK, g, r, p)
    a_cs = jnp.cumsum((dt * a).reshape(b, c, SSM_CHUNK, g, r), axis=2)
    bc = bm.reshape(b, c, SSM_CHUNK, g, n)
    cc = cm.reshape(b, c, SSM_CHUNK, g, n)
    seg = a_cs[:, :, :, None] - a_cs[:, :, None, :]
    lower = jnp.tril(jnp.ones((SSM_CHUNK, SSM_CHUNK), dtype=bool))[None, None, :, :, None, None]
    decay = jnp.exp(jnp.where(lower, seg, -jnp.inf))
    cb = jnp.einsum('bclgn,bcsgn->bclsg', cc, bc)
    y_diag = jnp.einsum('bclsgr,bcsgrp->bclgrp', cb[..., None] * decay, xc)
    decay_states = jnp.exp(a_cs[:, :, -1:] - a_cs)
    states = jnp.einsum('bcsgn,bcsgr,bcsgrp->bcgrpn', bc, decay_states, xc)
    chunk_decay = jnp.exp(a_cs[:, :, -1])

    def step(h, inp):
        s, d = inp
        return h * d[..., None, None] + s, h

    h0 = jnp.zeros((b, g, r, p, n), jnp.float32)
    _, prev = lax.scan(step, h0, (jnp.moveaxis(states, 1, 0), jnp.moveaxis(chunk_decay, 1, 0)))
    prev = jnp.moveaxis(prev, 0, 1)
    y_off = jnp.einsum('bclgn,bcgrpn,bclgr->bclgrp', cc, prev, jnp.exp(a_cs))
    return (y_diag + y_off).reshape(b, t, g, r, p)


def ssd_branch(u, conv_w, conv_b, dt_bias, a_log, d_skip, norm_w):
    b, t, _ = u.shape
    f32 = jnp.float32
    z, xbc, dt_raw = jnp.split(u, (BRANCH_W, BRANCH_W + SSM_XBC), axis=-1)
    xbc = jax.nn.silu(dwconv_centred(xbc, conv_w) + conv_b).astype(f32)
    xs, bm, cm = jnp.split(xbc, (BRANCH_W, BRANCH_W + SSM_GROUPS * SSM_STATE), axis=-1)
    dt = jax.nn.softplus(dt_raw.astype(f32).reshape(b, t, 2, SSM_GROUPS, SSM_HPG)
                         + dt_bias.astype(f32).reshape(2, SSM_GROUPS, SSM_HPG))
    a = -jnp.exp(a_log.astype(f32)).reshape(2, SSM_GROUPS, SSM_HPG)
    front = SSM_CHUNK - N_META
    back = (-(t - N_META)) % SSM_CHUNK
    xs5 = xs.reshape(b, t, SSM_GROUPS, SSM_HPG, SSM_HEAD_DIM)
    xh = _pad_time(xs5, front, back)
    bp = _pad_time(bm.reshape(b, t, SSM_GROUPS, SSM_STATE), front, back)
    cp = _pad_time(cm.reshape(b, t, SSM_GROUPS, SSM_STATE), front, back)
    dtp = _pad_time(dt, front, back)
    y_fwd = ssd_scan(xh, dtp[:, :, 0], a[0], bp, cp)
    y_bwd = ssd_scan(xh[:, ::-1], dtp[:, ::-1, 1], a[1], bp[:, ::-1], cp[:, ::-1])[:, ::-1]
    y = (y_fwd + y_bwd)[:, front:front + t] + xs5 * d_skip.astype(f32).reshape(SSM_GROUPS, SSM_HPG, 1)
    y = y.reshape(b, t, BRANCH_W) * jax.nn.silu(z.astype(f32))
    yg = y.reshape(b, t, SSM_GROUPS, BRANCH_W // SSM_GROUPS)
    yg = yg * lax.rsqrt(jnp.mean(yg * yg, axis=-1, keepdims=True) + NORM_EPS)
    return (yg.reshape(b, t, BRANCH_W) * norm_w.astype(f32)).astype(u.dtype)


def hyena_filter_spectrum(t, w1, b1, w2, b2, w3, freq):
    f32 = jnp.float32
    tt = jnp.linspace(0.0, 1.0, t, dtype=f32)[:, None]
    bands = (HYENA_EMB - 1) // 2
    w = (2.0 * math.pi / t) * jnp.arange(t, dtype=f32)[:, None]
    fr = jnp.linspace(1e-4, bands - 1, bands, dtype=f32)[None, :]
    z = jnp.concatenate([tt, jnp.cos(fr * w), -jnp.sin(fr * w)], axis=-1)
    fq = freq.astype(f32)
    hid = jnp.sin(fq * (z @ w1.astype(f32) + b1.astype(f32)))
    hid = jnp.sin(fq * (hid @ w2.astype(f32) + b2.astype(f32)))
    h = (hid @ w3.astype(f32)).reshape(t, HYENA_ORDER, 2, BRANCH_W)
    max_decay = math.log(HYENA_TARGET) / HYENA_FAST
    min_decay = math.log(HYENA_TARGET) / HYENA_SLOW
    deltas = jnp.abs(jnp.linspace(min_decay, max_decay, BRANCH_W, dtype=f32))
    h = h * jnp.exp(-tt[:, :, None, None] * deltas)
    h_f = h[:, :, 0]
    h_b = h[1:, :, 1]
    l1 = jnp.sum(jnp.abs(h_f), axis=0) + jnp.sum(jnp.abs(h_b), axis=0)
    k = jnp.concatenate([h_f, jnp.zeros((1, HYENA_ORDER, BRANCH_W), f32), h_b[::-1]], axis=0) / l1
    return jnp.fft.rfft(k, axis=0)


def long_conv(u, kf, bias):
    t = u.shape[1]
    uf = jnp.fft.rfft(u, n=2 * t, axis=1)
    return jnp.fft.irfft(uf * kf, n=2 * t, axis=1)[:, :t] + u * bias


def hyena_branch(u, conv_w, w1, b1, w2, b2, w3, freq, bias):
    t = u.shape[1]
    uc = dwconv_centred(u, conv_w).astype(jnp.float32)
    v, x1, x2 = jnp.split(uc, 3, axis=-1)
    kf = hyena_filter_spectrum(t, w1, b1, w2, b2, w3, freq)
    bias = bias.astype(jnp.float32)
    z = x1 * long_conv(v, kf[:, 0], bias[0])
    z = x2 * long_conv(z, kf[:, 1], bias[1])
    return z.astype(u.dtype)


def shortconv_branch(u, conv_w):
    bg, cg, xin = jnp.split(u, 3, axis=-1)
    return bg * dwconv_centred(cg * xin, conv_w)


def mixer_block(x, norm_mix, w_in, ssm_conv_w, ssm_conv_b, ssm_dt_bias, ssm_a_log, ssm_d, ssm_norm,
                hyena_conv_w, hyena_w1, hyena_b1, hyena_w2, hyena_b2, hyena_w3, hyena_freq, hyena_bias,
                sc_conv_w, w_branch, w_out):
    b, t, _ = x.shape
    h = rms_norm(x, norm_mix)
    proj = h @ w_in
    u_fn, u_ssm, u_hy, u_sc, g_raw = jnp.split(proj, SPLIT_IN, axis=-1)
    branches = (
        fnet_branch(u_fn),
        ssd_branch(u_ssm, ssm_conv_w, ssm_conv_b, ssm_dt_bias, ssm_a_log, ssm_d, ssm_norm),
        hyena_branch(u_hy, hyena_conv_w, hyena_w1, hyena_b1, hyena_w2, hyena_b2, hyena_w3, hyena_freq, hyena_bias),
        shortconv_branch(u_sc, sc_conv_w),
    )
    gates = jax.nn.sigmoid(g_raw.astype(jnp.float32)).astype(x.dtype).reshape(b, t, N_BRANCH, D_MODEL)
    merged = gates[:, :, 0] * (branches[0] @ w_branch[0])
    for k in range(1, N_BRANCH):
        merged = merged + gates[:, :, k] * (branches[k] @ w_branch[k])
    return merged @ w_out


def conv_ffn(x, norm_ffn, ffn_conv_w, w_up, w_down):
    h = rms_norm(x, norm_ffn)
    up = dwconv_centred(h @ w_up, ffn_conv_w)
    a, v = jnp.split(up, 2, axis=-1)
    return (jax.nn.silu(a) * v) @ w_down


def run_trunk(x, meta_tokens, norm_final, mixer_params, ffn_params):
    b = x.shape[0]
    meta = jnp.broadcast_to(meta_tokens[None].astype(x.dtype), (b, N_META, D_MODEL))
    h = jnp.concatenate([meta, x], axis=1)
    for l in range(DEPTH):
        h = h + mixer_block(h, *(p[l] for p in mixer_params))
        h = h + conv_ffn(h, *(p[l] for p in ffn_params))
    return rms_norm(h, norm_final)[:, N_META:]


def setup_inputs(seed: int = 0) -> dict:
    key = jax.random.key(seed)
    ks = jax.random.split(key, 32)
    f32 = jnp.float32

    def nrm(k, shape, scale):
        return jax.random.normal(k, shape, f32) * scale

    dt0 = jnp.exp(jax.random.uniform(ks[8], (DEPTH, 2, SSM_HEADS), f32, math.log(1e-3), math.log(1e-1)))
    return {
        'x_prompt': nrm(ks[0], (BATCH, SEQ, D_MODEL), 1.0),
        'x_sample': nrm(ks[1], (DEC_BATCH, DEC_SEQ, D_MODEL), 1.0),
        'meta_tokens': nrm(ks[2], (N_META, D_MODEL), 1.0),
        'norm_mix': 1.0 + nrm(ks[3], (DEPTH, D_MODEL), 0.02),
        'w_in': nrm(ks[4], (DEPTH, D_MODEL, D_IN_PROJ), D_MODEL ** -0.5),
        'ssm_conv_w': nrm(ks[5], (DEPTH, SSM_CONV, SSM_XBC), SSM_CONV ** -0.5),
        'ssm_conv_b': nrm(ks[6], (DEPTH, SSM_XBC), 0.02),
        'ssm_dt_bias': dt0 + jnp.log(-jnp.expm1(-dt0)),
        'ssm_a_log': jnp.log(jax.random.uniform(ks[9], (DEPTH, 2, SSM_HEADS), f32, 1.0, 16.0)),
        'ssm_d': 1.0 + nrm(ks[10], (DEPTH, SSM_HEADS), 0.02),
        'ssm_norm': 1.0 + nrm(ks[11], (DEPTH, BRANCH_W), 0.02),
        'hyena_conv_w': nrm(ks[12], (DEPTH, HYENA_CONV, COL_HYENA), HYENA_CONV ** -0.5),
        'hyena_w1': nrm(ks[13], (DEPTH, HYENA_EMB, HYENA_FFN), HYENA_EMB ** -0.5),
        'hyena_b1': nrm(ks[14], (DEPTH, HYENA_FFN), 0.02),
        'hyena_w2': nrm(ks[15], (DEPTH, HYENA_FFN, HYENA_FFN), HYENA_FFN ** -0.5),
        'hyena_b2': nrm(ks[16], (DEPTH, HYENA_FFN), 0.02),
        'hyena_w3': nrm(ks[17], (DEPTH, HYENA_FFN, HYENA_ORDER * 2 * BRANCH_W), HYENA_FFN ** -0.5),
        'hyena_freq': 1.0 + nrm(ks[18], (DEPTH, HYENA_FFN), 0.02),
        'hyena_bias': nrm(ks[19], (DEPTH, HYENA_ORDER, BRANCH_W), 1.0),
        'sc_conv_w': nrm(ks[20], (DEPTH, SC_CONV, BRANCH_W), SC_CONV ** -0.5),
        'w_branch': nrm(ks[21], (DEPTH, N_BRANCH, BRANCH_W, D_MODEL), BRANCH_W ** -0.5),
        'w_out': nrm(ks[22], (DEPTH, D_MODEL, D_MODEL), D_MODEL ** -0.5),
        'norm_ffn': 1.0 + nrm(ks[23], (DEPTH, D_MODEL), 0.02),
        'ffn_conv_w': nrm(ks[24], (DEPTH, FFN_CONV, 2 * D_FF), FFN_CONV ** -0.5),
        'w_up': nrm(ks[25], (DEPTH, D_MODEL, 2 * D_FF), D_MODEL ** -0.5),
        'w_down': nrm(ks[26], (DEPTH, D_FF, D_MODEL), D_FF ** -0.5),
        'norm_final': 1.0 + nrm(ks[27], (D_MODEL,), 0.02),
    }


def reference(x_prompt, x_sample, meta_tokens, norm_mix, w_in, ssm_conv_w, ssm_conv_b, ssm_dt_bias,
              ssm_a_log, ssm_d, ssm_norm, hyena_conv_w, hyena_w1, hyena_b1, hyena_w2, hyena_b2, hyena_w3,
              hyena_freq, hyena_bias, sc_conv_w, w_branch, w_out, norm_ffn, ffn_conv_w, w_up, w_down,
              norm_final):
    mixer_params = (norm_mix, w_in, ssm_conv_w, ssm_conv_b, ssm_dt_bias, ssm_a_log, ssm_d, ssm_norm,
                    hyena_conv_w, hyena_w1, hyena_b1, hyena_w2, hyena_b2, hyena_w3, hyena_freq, hyena_bias,
                    sc_conv_w, w_branch, w_out)
    ffn_params = (norm_ffn, ffn_conv_w, w_up, w_down)
    y_prompt = run_trunk(x_prompt, meta_tokens, norm_final, mixer_params, ffn_params)
    y_sample = run_trunk(x_sample, meta_tokens, norm_final, mixer_params, ffn_params)
    return (y_prompt, y_sample)
```

```python
import functools
import math

import jax
import jax.numpy as jnp
from jax import lax
from jax.experimental import pallas as pl
from jax.experimental.pallas import tpu as pltpu

F32 = jnp.float32
BF16 = jnp.bfloat16
HI = lax.Precision.HIGHEST

NORM_EPS = 1e-6
N_META = 16
BRANCH_W = 512
FNET_GW = 128
SSM_GROUPS = 2
SSM_HPG = 4
SSM_HEAD_DIM = 64
SSM_STATE = 128
SSM_CHUNK = 128
HYENA_ORDER = 2
HYENA_EMB = 33
HYENA_TARGET = 1e-2
HYENA_FAST = 0.3
HYENA_SLOW = 1.5

ROW_TILE = 700
FFN_ROW_TILE = 350
TOKEN_TILE = 1024
HALO = 16
VMEM_LIMIT = 56 * 1024 * 1024

COL_FN = 0
COL_Z = 512
COL_XBC = 1024
COL_HY = 2048
COL_SC = 3584
COL_GATE = 5120
COL_DT = 9216
N_PROJ = 9472


def _cp(*sem):
    return pltpu.CompilerParams(dimension_semantics=sem, vmem_limit_bytes=VMEM_LIMIT)


def _pick_tile(n, target, mult=16):
    k = max(1, -(-n // target))
    t = -(-n // k)
    return -(-t // mult) * mult


def _silu(x):
    return x * (1.0 / (1.0 + jnp.exp(-x)))


def _sigmoid(x):
    return 1.0 / (1.0 + jnp.exp(-x))


def _norm_matmul_kernel(x_ref, g_ref, w_ref, o_ref, xn_ref):
    @pl.when(pl.program_id(1) == 0)
    def _():
        x = x_ref[...]
        ms = jnp.mean(x * x, axis=-1, keepdims=True)
        xn_ref[...] = (x * lax.rsqrt(ms + NORM_EPS) * g_ref[...]).astype(BF16)

    o_ref[...] = jnp.dot(xn_ref[...], w_ref[...], preferred_element_type=F32).astype(o_ref.dtype)


def _norm_matmul(x, g, w, tm, tn):
    n, d = x.shape
    m = w.shape[1]
    tm = min(tm, -(-n // HALO) * HALO)
    return pl.pallas_call(
        _norm_matmul_kernel,
        grid=(pl.cdiv(n, tm), pl.cdiv(m, tn)),
        in_specs=[pl.BlockSpec((tm, d), lambda i, j: (i, 0)),
                  pl.BlockSpec((1, d), lambda i, j: (0, 0)),
                  pl.BlockSpec((d, tn), lambda i, j: (0, j))],
        out_specs=pl.BlockSpec((tm, tn), lambda i, j: (i, j)),
        out_shape=jax.ShapeDtypeStruct((n, m), BF16),
        scratch_shapes=[pltpu.VMEM((tm, d), BF16)],
        compiler_params=_cp("arbitrary", "arbitrary"),
        name="norm_matmul",
    )(x, g.reshape(1, d), w)


def _conv3(x, prev_row, next_row, w, row0, t_len):
    n = x.shape[0]
    li = lax.broadcasted_iota(jnp.int32, (n, 1), 0)
    gi = li + row0
    xm = pltpu.roll(x, 1, 0)
    xm = jnp.where(li == 0, prev_row, xm)
    xm = jnp.where(gi >= 1, xm, 0.0)
    xp = pltpu.roll(x, n - 1, 0)
    xp = jnp.where(li == n - 1, next_row, xp)
    xp = jnp.where(gi + 1 < t_len, xp, 0.0)
    return xm * w[0:1] + x * w[1:2] + xp * w[2:3]


def _last_row(halo_ref):
    return halo_ref[0].astype(F32)[HALO - 1:HALO]


def _first_row(halo_ref):
    return halo_ref[0].astype(F32)[0:1]


def _halo_specs(tr, cw, t_len, col_blk):
    nh = tr // HALO
    last = t_len // HALO - 1
    main = pl.BlockSpec((1, tr, cw), lambda b, i, j: (b, i, col_blk + j))
    prev = pl.BlockSpec((1, HALO, cw), lambda b, i, j: (b, jnp.maximum(i * nh - 1, 0), col_blk + j))
    nxt = pl.BlockSpec((1, HALO, cw), lambda b, i, j: (b, jnp.minimum((i + 1) * nh, last), col_blk + j))
    return main, prev, nxt


def _conv_act_kernel(x_ref, p_ref, n_ref, w_ref, b_ref, o_ref, *, tr, t_len, act):
    row0 = pl.program_id(1) * tr
    x = x_ref[0].astype(F32)
    y = _conv3(x, _last_row(p_ref), _first_row(n_ref), w_ref[...], row0, t_len) + b_ref[...]
    if act:
        y = _silu(y)
    o_ref[0] = y.astype(o_ref.dtype)


def _conv_act(proj, col0, width, w, bias, act, tr):
    b, t, _ = proj.shape
    cw = 512
    main, prev, nxt = _halo_specs(tr, cw, t, col0 // cw)
    return pl.pallas_call(
        functools.partial(_conv_act_kernel, tr=tr, t_len=t, act=act),
        grid=(b, pl.cdiv(t, tr), width // cw),
        in_specs=[main, prev, nxt,
                  pl.BlockSpec((3, cw), lambda b_, i, j: (0, j)),
                  pl.BlockSpec((1, cw), lambda b_, i, j: (0, j))],
        out_specs=pl.BlockSpec((1, tr, cw), lambda b_, i, j: (b_, i, j)),
        out_shape=jax.ShapeDtypeStruct((b, t, width), BF16),
        compiler_params=_cp("arbitrary", "arbitrary", "arbitrary"),
        name="conv_act",
    )(proj, proj, proj, w, bias)


def _sc_kernel(bg_ref, cg_ref, xi_ref, cgp_ref, xip_ref, cgn_ref, xin_ref, w_ref, o_ref, *, tr, t_len):
    row0 = pl.program_id(1) * tr
    u = cg_ref[0].astype(F32) * xi_ref[0].astype(F32)
    prev = _last_row(cgp_ref) * _last_row(xip_ref)
    nxt = _first_row(cgn_ref) * _first_row(xin_ref)
    y = _conv3(u, prev, nxt, w_ref[...], row0, t_len)
    o_ref[0] = (bg_ref[0].astype(F32) * y).astype(o_ref.dtype)


def _sc_branch(proj, w, tr):
    b, t, _ = proj.shape
    cw = BRANCH_W
    c0 = COL_SC // cw
    bg, _, _ = _halo_specs(tr, cw, t, c0)
    cg, cgp, cgn = _halo_specs(tr, cw, t, c0 + 1)
    xi, xip, xin = _halo_specs(tr, cw, t, c0 + 2)
    return pl.pallas_call(
        functools.partial(_sc_kernel, tr=tr, t_len=t),
        grid=(b, pl.cdiv(t, tr), 1),
        in_specs=[bg, cg, xi, cgp, xip, cgn, xin, pl.BlockSpec((3, cw), lambda b_, i, j: (0, 0))],
        out_specs=pl.BlockSpec((1, tr, cw), lambda b_, i, j: (b_, i, 0)),
        out_shape=jax.ShapeDtypeStruct((b, t, cw), BF16),
        compiler_params=_cp("arbitrary", "arbitrary", "arbitrary"),
        name="sc_branch",
    )(proj, proj, proj, proj, proj, proj, proj, w)


def _fnet_kernel(a_ref, u_ref, cs_ref, o_ref, *, tmf):
    p = jnp.dot(a_ref[0], u_ref[0], preferred_element_type=F32)
    pc = p[:tmf].astype(BF16)
    ps = p[tmf:].astype(BF16)
    y = jnp.dot(pc, cs_ref[0], preferred_element_type=F32) - jnp.dot(ps, cs_ref[1], preferred_element_type=F32)
    o_ref[0] = y.astype(o_ref.dtype)


def _fnet_tables(t, tmf):
    nm = -(-t // tmf)
    j = jnp.arange(nm * tmf, dtype=jnp.int32)[:, None]
    k = jnp.arange(t, dtype=jnp.int32)[None, :]
    ang = ((j * k) % t).astype(F32) * (2.0 * math.pi / t)
    valid = j < t
    c = jnp.where(valid, jnp.cos(ang), 0.0).astype(BF16).reshape(nm, tmf, t)
    s = jnp.where(valid, jnp.sin(ang), 0.0).astype(BF16).reshape(nm, tmf, t)
    a = jnp.concatenate([c, s], axis=1)
    jj = jnp.arange(BRANCH_W, dtype=jnp.int32)[:, None]
    kk = jnp.arange(BRANCH_W, dtype=jnp.int32)[None, :]
    same = (jj // FNET_GW) == (kk // FNET_GW)
    ang2 = (((jj % FNET_GW) * (kk % FNET_GW)) % FNET_GW).astype(F32) * (2.0 * math.pi / FNET_GW)
    scale = 1.0 / math.sqrt(t * FNET_GW)
    cc = jnp.where(same, jnp.cos(ang2), 0.0) * scale
    sc = jnp.where(same, jnp.sin(ang2), 0.0) * scale
    return a, jnp.stack([cc, sc]).astype(BF16)


def _fnet_branch(proj, tmf):
    b, t, _ = proj.shape
    a, cs = _fnet_tables(t, tmf)
    nm = a.shape[0]
    return pl.pallas_call(
        functools.partial(_fnet_kernel, tmf=tmf),
        grid=(nm, b),
        in_specs=[pl.BlockSpec((1, 2 * tmf, t), lambda i, j: (i, 0, 0)),
                  pl.BlockSpec((1, t, BRANCH_W), lambda i, j: (j, 0, COL_FN // BRANCH_W)),
                  pl.BlockSpec((2, BRANCH_W, BRANCH_W), lambda i, j: (0, 0, 0))],
        out_specs=pl.BlockSpec((1, tmf, BRANCH_W), lambda i, j: (j, i, 0)),
        out_shape=jax.ShapeDtypeStruct((b, t, BRANCH_W), BF16),
        compiler_params=_cp("arbitrary", "arbitrary"),
        name="fnet",
    )(a, proj, cs)


def _hyena_filter_kernel(w1t_ref, w1c_ref, w1s_ref, b1_ref, w2_ref, b2_ref, fq_ref, w3f_ref, w3b_ref, dl_ref,
                         o_ref, hid_ref, *, t):
    @pl.when(pl.program_id(0) == 0)
    def _():
        n = lax.broadcasted_iota(jnp.int32, (t, 1), 0).astype(F32)
        tt = n * (1.0 / (t - 1))
        wv = n * (2.0 * math.pi / t)
        bands = (HYENA_EMB - 1) // 2
        fi = lax.broadcasted_iota(jnp.int32, (1, bands), 1).astype(F32)
        fr = 1e-4 + fi * ((bands - 1 - 1e-4) / (bands - 1))
        arg = wv * fr
        pre = (tt * w1t_ref[...]
               + jnp.dot(jnp.cos(arg), w1c_ref[...], precision=HI, preferred_element_type=F32)
               - jnp.dot(jnp.sin(arg), w1s_ref[...], precision=HI, preferred_element_type=F32)
               + b1_ref[...])
        fq = fq_ref[...]
        h1 = jnp.sin(fq * pre)
        h2 = jnp.sin(fq * (jnp.dot(h1, w2_ref[...], precision=HI, preferred_element_type=F32) + b2_ref[...]))
        hid_ref[...] = h2

    n = lax.broadcasted_iota(jnp.int32, (t, 1), 0)
    tt = n.astype(F32) * (1.0 / (t - 1))
    dec = jnp.exp(-tt * dl_ref[...])
    hid = hid_ref[...]
    hf = jnp.dot(hid, w3f_ref[...], precision=HI, preferred_element_type=F32) * dec
    hb = jnp.dot(hid, w3b_ref[...], precision=HI, preferred_element_type=F32) * dec
    hb = jnp.where(n >= 1, hb, 0.0)
    l1 = jnp.sum(jnp.abs(hf), axis=0, keepdims=True) + jnp.sum(jnp.abs(hb), axis=0, keepdims=True)
    inv = 1.0 / l1
    o_ref[0, 0] = (hf * inv).astype(o_ref.dtype)
    o_ref[0, 1] = (hb * inv).astype(o_ref.dtype)


def _hyena_filters(t, w1, b1, w2, b2, w3, freq):
    cw = 128
    nb = BRANCH_W // cw
    bands = (HYENA_EMB - 1) // 2
    nf = w2.shape[0]
    max_decay = math.log(HYENA_TARGET) / HYENA_FAST
    min_decay = math.log(HYENA_TARGET) / HYENA_SLOW
    deltas = jnp.abs(jnp.linspace(min_decay, max_decay, BRANCH_W, dtype=F32)).reshape(1, BRANCH_W)
    full = lambda shape: pl.BlockSpec(shape, lambda g: (0,) * len(shape))
    return pl.pallas_call(
        functools.partial(_hyena_filter_kernel, t=t),
        grid=(HYENA_ORDER * nb,),
        in_specs=[full((1, nf)), full((bands, nf)), full((bands, nf)), full((1, nf)), full((nf, nf)),
                  full((1, nf)), full((1, nf)),
                  pl.BlockSpec((nf, cw), lambda g: (0, (g // nb) * 2 * nb + g % nb)),
                  pl.BlockSpec((nf, cw), lambda g: (0, (g // nb) * 2 * nb + nb + g % nb)),
                  pl.BlockSpec((1, cw), lambda g: (0, g % nb))],
        out_specs=pl.BlockSpec((1, 2, t, cw), lambda g: (g // nb, 0, 0, g % nb)),
        out_shape=jax.ShapeDtypeStruct((HYENA_ORDER, 2, t, BRANCH_W), BF16),
        scratch_shapes=[pltpu.VMEM((t, nf), F32)],
        compiler_params=_cp("arbitrary"),
        name="hyena_filter",
    )(w1[0:1], w1[1:1 + bands], w1[1 + bands:], b1.reshape(1, nf), w2, b2.reshape(1, nf), freq.reshape(1, nf),
      w3, w3, deltas)


def _hyena_tables(t, n_fft, tf):
    nfb = (n_fft // 2) // tf
    f = jnp.arange(n_fft // 2, dtype=jnp.int32)[:, None]
    n = jnp.arange(t, dtype=jnp.int32)[None, :]
    ang = (((2 * f + 1) * n) % (2 * n_fft)).astype(F32) * (math.pi / n_fft)
    c = jnp.cos(ang).astype(BF16).reshape(nfb, tf, t)
    s = (-jnp.sin(ang)).astype(BF16).reshape(nfb, tf, t)
    fwd = jnp.concatenate([c, s], axis=1)
    inv = fwd.reshape(n_fft, t).T
    return fwd, inv


def _hy_spec_kernel(a_ref, x_ref, o_ref):
    o_ref[0, 0] = jnp.dot(a_ref[0], x_ref[0], preferred_element_type=F32)


def _hy_filter_spectrum(fwd, filt):
    nfb, tf2, t = fwd.shape
    nb = filt.shape[0]
    return pl.pallas_call(
        _hy_spec_kernel,
        grid=(nfb, nb),
        in_specs=[pl.BlockSpec((1, tf2, t), lambda i, j: (i, 0, 0)),
                  pl.BlockSpec((1, t, BRANCH_W), lambda i, j: (j, 0, 0))],
        out_specs=pl.BlockSpec((1, 1, tf2, BRANCH_W), lambda i, j: (j, i, 0, 0)),
        out_shape=jax.ShapeDtypeStruct((nb, nfb, tf2, BRANCH_W), F32),
        compiler_params=_cp("arbitrary", "arbitrary"),
        name="hyena_filter_spectrum",
    )(fwd, filt)


def _hy_fwd_kernel(a_ref, x_ref, kf_ref, kb_ref, o_ref, *, tf, scale):
    p = jnp.dot(a_ref[0], x_ref[0], preferred_element_type=F32)
    ur, ui = p[:tf], p[tf:]
    kr = (kf_ref[0, 0, :tf] + kb_ref[0, 0, :tf]) * scale
    ki = (kf_ref[0, 0, tf:] - kb_ref[0, 0, tf:]) * scale
    o_ref[0, 0, :tf] = (ur * kr - ui * ki).astype(o_ref.dtype)
    o_ref[0, 0, tf:] = (ur * ki + ui * kr).astype(o_ref.dtype)


def _hy_forward(fwd, x, x_col, spec, order, n_fft):
    nfb, tf2, t = fwd.shape
    b = x.shape[0]
    return pl.pallas_call(
        functools.partial(_hy_fwd_kernel, tf=tf2 // 2, scale=2.0 / n_fft),
        grid=(nfb, b),
        in_specs=[pl.BlockSpec((1, tf2, t), lambda i, j: (i, 0, 0)),
                  pl.BlockSpec((1, t, BRANCH_W), lambda i, j: (j, 0, x_col)),
                  pl.BlockSpec((1, 1, tf2, BRANCH_W), lambda i, j: (2 * order, i, 0, 0)),
                  pl.BlockSpec((1, 1, tf2, BRANCH_W), lambda i, j: (2 * order + 1, i, 0, 0))],
        out_specs=pl.BlockSpec((1, 1, tf2, BRANCH_W), lambda i, j: (j, i, 0, 0)),
        out_shape=jax.ShapeDtypeStruct((b, nfb, tf2, BRANCH_W), BF16),
        compiler_params=_cp("arbitrary", "arbitrary"),
        name="hyena_fwd",
    )(fwd, x, spec, spec)


def _hy_inv_kernel(a_ref, y_ref, g_ref, v_ref, bias_ref, o_ref):
    conv = jnp.dot(a_ref[...], y_ref[0], preferred_element_type=F32)
    v = v_ref[0].astype(F32)
    o_ref[0] = (g_ref[0].astype(F32) * (conv + v * bias_ref[0])).astype(o_ref.dtype)


def _hy_inverse(inv, yhat, gate, gate_col, v, v_col, bias, order, tt):
    t, n_fft = inv.shape
    b = yhat.shape[0]
    yh = yhat.reshape(b, n_fft, BRANCH_W)
    return pl.pallas_call(
        _hy_inv_kernel,
        grid=(pl.cdiv(t, tt), b),
        in_specs=[pl.BlockSpec((tt, n_fft), lambda i, j: (i, 0)),
                  pl.BlockSpec((1, n_fft, BRANCH_W), lambda i, j: (j, 0, 0)),
                  pl.BlockSpec((1, tt, BRANCH_W), lambda i, j: (j, i, gate_col)),
                  pl.BlockSpec((1, tt, BRANCH_W), lambda i, j: (j, i, v_col)),
                  pl.BlockSpec((1, 1, BRANCH_W), lambda i, j: (order, 0, 0))],
        out_specs=pl.BlockSpec((1, tt, BRANCH_W), lambda i, j: (j, i, 0)),
        out_shape=jax.ShapeDtypeStruct((b, t, BRANCH_W), BF16),
        compiler_params=_cp("arbitrary", "arbitrary"),
        name="hyena_inv",
    )(inv, yh, gate, v, bias.reshape(HYENA_ORDER, 1, BRANCH_W))


def _softplus(x):
    return jnp.maximum(x, 0.0) + jnp.log(1.0 + jnp.exp(-jnp.abs(x)))


def _ssd_kernel(xs_ref, b_ref, c_ref, z_ref, dt_ref, dtb_ref, an_ref, aw_ref, dsk_ref, nw_ref, o_ref, y_scr,
                *, t_len, nc):
    q = SSM_CHUNK
    gw = SSM_HPG * SSM_HEAD_DIM
    front = q - N_META
    ri = lax.broadcasted_iota(jnp.int32, (q, q), 0)
    ci = lax.broadcasted_iota(jnp.int32, (q, q), 1)
    tri = (ri >= ci, ri <= ci)
    trif = (tri[0].astype(F32), tri[1].astype(F32))
    ek = lax.broadcasted_iota(jnp.int32, (q, gw), 0)
    ej = lax.broadcasted_iota(jnp.int32, (q, gw), 1) // SSM_HEAD_DIM
    expand = ((ek == ej).astype(F32), (ek == ej + SSM_HPG).astype(F32))
    hmask = [(lax.broadcasted_iota(jnp.int32, (1, gw), 1) // SSM_HEAD_DIM == r).astype(F32) for r in range(SSM_HPG)]
    lane = lax.broadcasted_iota(jnp.int32, (1, q), 1)
    dt_lane_mask = (lane < 2 * SSM_HPG).astype(F32)
    dtb = dtb_ref[...]
    an = an_ref[...]

    def load(ref, r0, nrows):
        return ref[0, pl.ds(r0, nrows), :].astype(F32)

    def chunk_data(c):
        r0 = pl.multiple_of(c * q - front, HALO)
        dt = _softplus(load(dt_ref, r0, q) + dtb) * dt_lane_mask
        return load(xs_ref, r0, q), load(b_ref, r0, q), load(c_ref, r0, q), dt

    def chunk0_data():
        def pad(x):
            return jnp.concatenate([jnp.zeros((front, x.shape[1]), F32), x], axis=0)
        dt = _softplus(load(dt_ref, 0, N_META) + dtb) * dt_lane_mask
        return pad(load(xs_ref, 0, N_META)), pad(load(b_ref, 0, N_META)), pad(load(c_ref, 0, N_META)), pad(dt)

    def process(data, h, d):
        xs, bm, cm, dt = data
        dt_w = jnp.dot(dt, expand[d], precision=HI, preferred_element_type=F32)
        cs_w = jnp.dot(trif[d], dt_w * aw_ref[d:d + 1, :], precision=HI, preferred_element_type=F32)
        cs_n = jnp.dot(trif[d], dt * an, precision=HI, preferred_element_type=F32)
        cs_t = cs_n.T
        tot = cs_w[q - 1:q, :] if d == 0 else cs_w[0:1, :]
        xt = xs * dt_w
        xd = (xt * jnp.exp(tot - cs_w)).astype(BF16)
        bmb = bm.astype(BF16)
        cmb = cm.astype(BF16)
        s_new = jnp.dot(bm.T.astype(BF16), xd, preferred_element_type=F32)
        y = jnp.dot(cmb, h.astype(BF16), preferred_element_type=F32) * jnp.exp(cs_w)
        cb = lax.dot_general(cmb, bmb, (((1,), (1,)), ((), ())), preferred_element_type=F32)
        for r in range(SSM_HPG):
            ln = d * SSM_HPG + r
            seg = cs_n[:, ln:ln + 1] - cs_t[ln:ln + 1, :]
            dec = jnp.where(tri[d], jnp.exp(jnp.minimum(seg, 0.0)), 0.0)
            m = (cb * dec).astype(BF16)
            y = y + jnp.dot(m, (xt * hmask[r]).astype(BF16), preferred_element_type=F32)
        h_new = h * jnp.exp(tot) + s_new
        return y, h_new

    def finish(y, xs, r_out, nrows, y_off):
        y = y[y_off:y_off + nrows] + xs[y_off:y_off + nrows] * dsk_ref[...]
        y = y * _silu(load(z_ref, r_out, nrows))
        y = y * lax.rsqrt(jnp.mean(y * y, axis=-1, keepdims=True) + NORM_EPS)
        o_ref[0, pl.ds(r_out, nrows), :] = (y * nw_ref[...]).astype(o_ref.dtype)

    h0 = jnp.zeros((SSM_STATE, gw), F32)
    y, h = process(chunk0_data(), h0, 0)
    y_scr[0:q, :] = y

    def fwd_body(c, h):
        y, h = process(chunk_data(c), h, 0)
        y_scr[pl.ds(pl.multiple_of(c * q, q), q), :] = y
        return h

    lax.fori_loop(1, nc, fwd_body, h)

    def bwd_body(i, h):
        c = nc - 1 - i
        data = chunk_data(c)
        y, h = process(data, h, 1)
        y = y + y_scr[pl.ds(pl.multiple_of(c * q, q), q), :]
        finish(y, data[0], pl.multiple_of(c * q - front, HALO), q, 0)
        return h

    h = lax.fori_loop(0, nc - 1, bwd_body, h0)
    data = chunk0_data()
    y, _ = process(data, h, 1)
    finish(y + y_scr[0:q, :], data[0], 0, N_META, front)


def _ssd_branch(xbc, proj, dt_bias, a_log, d_skip, norm_w):
    b, t, _ = xbc.shape
    q = SSM_CHUNK
    gw = SSM_HPG * SSM_HEAD_DIM
    nc = (q - N_META + t) // q
    assert nc * q == q - N_META + t, "sequence length minus meta tokens must be a multiple of the SSD chunk"
    a = -jnp.exp(a_log.astype(F32)).reshape(2, SSM_GROUPS, SSM_HPG)
    dtb = dt_bias.astype(F32).reshape(2, SSM_GROUPS, SSM_HPG)
    pad = lambda v: jnp.pad(jnp.transpose(v, (1, 0, 2)).reshape(SSM_GROUPS, 1, 2 * SSM_HPG),
                            ((0, 0), (0, 0), (0, q - 2 * SSM_HPG)))
    a_n = pad(a)
    dtb_n = pad(dtb)
    a_w = jnp.repeat(jnp.transpose(a, (1, 0, 2)), SSM_HEAD_DIM, axis=-1)
    dsk = jnp.repeat(d_skip.astype(F32).reshape(SSM_GROUPS, 1, SSM_HPG), SSM_HEAD_DIM, axis=-1)
    nw = norm_w.astype(F32).reshape(SSM_GROUPS, 1, gw)
    par = lambda shape: pl.BlockSpec((None,) + shape, lambda i, g: (g, 0, 0))
    return pl.pallas_call(
        functools.partial(_ssd_kernel, t_len=t, nc=nc),
        grid=(b, SSM_GROUPS),
        in_specs=[pl.BlockSpec((1, t, gw), lambda i, g: (i, 0, g)),
                  pl.BlockSpec((1, t, SSM_STATE), lambda i, g: (i, 0, 4 + g)),
                  pl.BlockSpec((1, t, SSM_STATE), lambda i, g: (i, 0, 6 + g)),
                  pl.BlockSpec((1, t, gw), lambda i, g: (i, 0, COL_Z // gw + g)),
                  pl.BlockSpec((1, t, q), lambda i, g: (i, 0, COL_DT // q + g)),
                  par((1, q)), par((1, q)), par((2, gw)), par((1, gw)), par((1, gw))],
        out_specs=pl.BlockSpec((1, t, gw), lambda i, g: (i, 0, g)),
        out_shape=jax.ShapeDtypeStruct((b, t, BRANCH_W), BF16),
        scratch_shapes=[pltpu.VMEM((nc * q, gw), F32)],
        compiler_params=_cp("arbitrary", "arbitrary"),
        name="ssd",
    )(xbc, xbc, xbc, proj, proj, dtb_n, a_n, a_w, dsk, nw)


def _merge_kernel(h_ref, y0_ref, y1_ref, y2_ref, y3_ref, g0_ref, g1_ref, g2_ref, g3_ref, wb_ref, wo_ref, o_ref):
    merged = None
    for k, (y_ref, g_ref) in enumerate(zip((y0_ref, y1_ref, y2_ref, y3_ref), (g0_ref, g1_ref, g2_ref, g3_ref))):
        gate = _sigmoid(g_ref[...].astype(F32))
        term = gate * jnp.dot(y_ref[...], wb_ref[k], preferred_element_type=F32)
        merged = term if merged is None else merged + term
    o_ref[...] = h_ref[...] + jnp.dot(merged.astype(BF16), wo_ref[...], preferred_element_type=F32)


def _merge(h, ys, proj, wb, wo, tm):
    n, d = h.shape
    nb = len(ys)
    row = lambda w: pl.BlockSpec((tm, w), lambda i: (i, 0))
    return pl.pallas_call(
        _merge_kernel,
        grid=(pl.cdiv(n, tm),),
        in_specs=[row(d)] + [row(BRANCH_W)] * nb
                 + [pl.BlockSpec((tm, d), functools.partial(lambda i, k: (i, COL_GATE // d + k), k=k)) for k in range(nb)]
                 + [pl.BlockSpec((nb, BRANCH_W, d), lambda i: (0, 0, 0)),
                    pl.BlockSpec((d, d), lambda i: (0, 0))],
        out_specs=row(d),
        out_shape=jax.ShapeDtypeStruct((n, d), F32),
        compiler_params=_cp("arbitrary"),
        name="merge",
    )(h, *ys, *([proj] * nb), wb, wo)


def _ffn_down_kernel(h_ref, a_ref, ap_ref, an_ref, v_ref, vp_ref, vn_ref, wa_ref, wv_ref, wd_ref, o_ref, acc_ref,
                     *, tr, t_len):
    j = pl.program_id(2)
    row0 = pl.program_id(1) * tr

    def conv(x_ref, p_ref, n_ref, w_ref):
        return _conv3(x_ref[0].astype(F32), _last_row(p_ref), _first_row(n_ref), w_ref[...], row0, t_len)

    g = _silu(conv(a_ref, ap_ref, an_ref, wa_ref)) * conv(v_ref, vp_ref, vn_ref, wv_ref)
    part = jnp.dot(g.astype(BF16), wd_ref[...], preferred_element_type=F32)

    @pl.when(j == 0)
    def _():
        acc_ref[...] = h_ref[0] + part

    @pl.when(j > 0)
    def _():
        acc_ref[...] += part

    @pl.when(j == pl.num_programs(2) - 1)
    def _():
        o_ref[0] = acc_ref[...]


def _ffn_down(h, up, conv_w, w_down, tr, tc):
    b, t, d = h.shape
    dff = w_down.shape[0]
    ncb = dff // tc
    a, ap, an = _halo_specs(tr, tc, t, 0)
    v, vp, vn = _halo_specs(tr, tc, t, ncb)
    return pl.pallas_call(
        functools.partial(_ffn_down_kernel, tr=tr, t_len=t),
        grid=(b, pl.cdiv(t, tr), ncb),
        in_specs=[pl.BlockSpec((1, tr, d), lambda b_, i, j: (b_, i, 0)),
                  a, ap, an, v, vp, vn,
                  pl.BlockSpec((3, tc), lambda b_, i, j: (0, j)),
                  pl.BlockSpec((3, tc), lambda b_, i, j: (0, ncb + j)),
                  pl.BlockSpec((tc, d), lambda b_, i, j: (j, 0))],
        out_specs=pl.BlockSpec((1, tr, d), lambda b_, i, j: (b_, i, 0)),
        out_shape=jax.ShapeDtypeStruct((b, t, d), F32),
        scratch_shapes=[pltpu.VMEM((tr, d), F32)],
        compiler_params=_cp("arbitrary", "arbitrary", "arbitrary"),
        name="ffn_down",
    )(h, up, up, up, up, up, up, conv_w, conv_w, w_down)


def _final_norm_kernel(x_ref, g_ref, o_ref):
    x = x_ref[...]
    ms = jnp.mean(x * x, axis=-1, keepdims=True)
    o_ref[...] = x * lax.rsqrt(ms + NORM_EPS) * g_ref[...]


def _final_norm(x, g, tm):
    n, d = x.shape
    return pl.pallas_call(
        _final_norm_kernel,
        grid=(pl.cdiv(n, tm),),
        in_specs=[pl.BlockSpec((tm, d), lambda i: (i, 0)), pl.BlockSpec((1, d), lambda i: (0, 0))],
        out_specs=pl.BlockSpec((tm, d), lambda i: (i, 0)),
        out_shape=jax.ShapeDtypeStruct((n, d), F32),
        compiler_params=_cp("arbitrary"),
        name="final_norm",
    )(x, g.reshape(1, d))


def _prep_w_in(w_in):
    d = w_in.shape[0]
    dt0 = COL_XBC + 1024
    ndt = 2 * SSM_GROUPS * SSM_HPG
    w_dt = w_in[:, dt0:dt0 + ndt].reshape(d, 2, SSM_GROUPS, SSM_HPG)
    w_dt = jnp.transpose(w_dt, (0, 2, 1, 3)).reshape(d, SSM_GROUPS, 2 * SSM_HPG)
    w_dt = jnp.pad(w_dt, ((0, 0), (0, 0), (0, SSM_CHUNK - 2 * SSM_HPG))).reshape(d, SSM_GROUPS * SSM_CHUNK)
    return jnp.concatenate([w_in[:, :dt0], w_in[:, dt0 + ndt:], w_dt], axis=1).astype(BF16)


def _hyena_fft_len(t):
    tf = 528
    nfb = -(-(2 * t - 1) // (2 * tf))
    return 2 * tf * nfb, tf


def _run_trunk(x, meta_tokens, norm_final, layers):
    b, seq, d = x.shape
    t = seq + N_META
    meta = jnp.broadcast_to(meta_tokens[None].astype(x.dtype), (b, N_META, d))
    h = jnp.concatenate([meta, x], axis=1).reshape(b * t, d)
    tr = _pick_tile(t, ROW_TILE)
    tr_ffn = _pick_tile(t, FFN_ROW_TILE)
    n_fft, tf = _hyena_fft_len(t)
    hy_fwd, hy_inv = _hyena_tables(t, n_fft, tf)
    tmf = tti = tr
    tok = min(TOKEN_TILE, b * t)
    for p in layers:
        proj = _norm_matmul(h, p["norm_mix"], p["w_in"], tok, 1024).reshape(b, t, N_PROJ)
        y_fn = _fnet_branch(proj, tmf)
        xbc = _conv_act(proj, COL_XBC, 1024, p["ssm_conv_w"], p["ssm_conv_b"].reshape(1, -1), True, tr)
        y_ssm = _ssd_branch(xbc, proj, p["ssm_dt_bias"], p["ssm_a_log"], p["ssm_d"], p["ssm_norm"])
        uc = _conv_act(proj, COL_HY, 3 * BRANCH_W, p["hyena_conv_w"], jnp.zeros((1, 3 * BRANCH_W), F32), False, tr)
        filt = _hyena_filters(t, p["hyena_w1"], p["hyena_b1"], p["hyena_w2"], p["hyena_b2"], p["hyena_w3"],
                              p["hyena_freq"]).reshape(2 * HYENA_ORDER, t, BRANCH_W)
        spec = _hy_filter_spectrum(hy_fwd, filt)
        z = _hy_inverse(hy_inv, _hy_forward(hy_fwd, uc, 0, spec, 0, n_fft), uc, 1, uc, 0, p["hyena_bias"], 0, tti)
        y_hy = _hy_inverse(hy_inv, _hy_forward(hy_fwd, z, 0, spec, 1, n_fft), uc, 2, z, 0, p["hyena_bias"], 1, tti)
        y_sc = _sc_branch(proj, p["sc_conv_w"], tr)
        flat = lambda a: a.reshape(b * t, a.shape[-1])
        h = _merge(h, [flat(y_fn), flat(y_ssm), flat(y_hy), flat(y_sc)], flat(proj), p["w_branch"], p["w_out"], min(512, tok))
        up = _norm_matmul(h, p["norm_ffn"], p["w_up"], tok, 1408).reshape(b, t, -1)
        h = _ffn_down(h.reshape(b, t, d), up, p["ffn_conv_w"], p["w_down"], tr_ffn, 1408).reshape(b * t, d)
    out = _final_norm(h, norm_final, tok).reshape(b, t, d)
    return out[:, N_META:]


def kernel(x_prompt, x_sample, meta_tokens, norm_mix, w_in, ssm_conv_w, ssm_conv_b, ssm_dt_bias, ssm_a_log, ssm_d,
           ssm_norm, hyena_conv_w, hyena_w1, hyena_b1, hyena_w2, hyena_b2, hyena_w3, hyena_freq, hyena_bias,
           sc_conv_w, w_branch, w_out, norm_ffn, ffn_conv_w, w_up, w_down, norm_final):
    depth = w_in.shape[0]
    layers = []
    for l in range(depth):
        layers.append(dict(
            norm_mix=norm_mix[l], w_in=_prep_w_in(w_in[l]), ssm_conv_w=ssm_conv_w[l], ssm_conv_b=ssm_conv_b[l],
            ssm_dt_bias=ssm_dt_bias[l], ssm_a_log=ssm_a_log[l], ssm_d=ssm_d[l], ssm_norm=ssm_norm[l],
            hyena_conv_w=hyena_conv_w[l], hyena_w1=hyena_w1[l], hyena_b1=hyena_b1[l], hyena_w2=hyena_w2[l],
            hyena_b2=hyena_b2[l], hyena_w3=hyena_w3[l], hyena_freq=hyena_freq[l], hyena_bias=hyena_bias[l],
            sc_conv_w=sc_conv_w[l], w_branch=w_branch[l].astype(BF16), w_out=w_out[l].astype(BF16),
            norm_ffn=norm_ffn[l], ffn_conv_w=ffn_conv_w[l], w_up=w_up[l].astype(BF16), w_down=w_down[l].astype(BF16)))
    y_prompt = _run_trunk(x_prompt, meta_tokens, norm_final, layers)
    y_sample = _run_trunk(x_sample, meta_tokens, norm_final, layers)
    return (y_prompt, y_sample)
```

```python
import functools
import math

import jax
import jax.numpy as jnp
from jax import lax
from jax.experimental import pallas as pl
from jax.experimental.pallas import tpu as pltpu

F32 = jnp.float32
BF16 = jnp.bfloat16
HI = lax.Precision.HIGHEST

NORM_EPS = 1e-6
N_META = 16
BRANCH_W = 512
FNET_GW = 128
SSM_GROUPS = 2
SSM_HPG = 4
SSM_HEAD_DIM = 64
SSM_STATE = 128
SSM_CHUNK = 128
HYENA_ORDER = 2
HYENA_EMB = 33
HYENA_TARGET = 1e-2
HYENA_FAST = 0.3
HYENA_SLOW = 1.5

ROW_TILE = 700
FFN_ROW_TILE = 350
TOKEN_TILE = 1024
SSD_UNROLL = 4
HALO = 16
VMEM_LIMIT = 56 * 1024 * 1024

CV_XBC = 0
CV_HY = 1024
N_CONV = 2560
COL_FN = 0
COL_Z = 512
COL_GATE = 1024
COL_SC = 5120
COL_DT = 6656
N_PLAIN = 6912


def _cp(*sem):
    return pltpu.CompilerParams(dimension_semantics=sem, vmem_limit_bytes=VMEM_LIMIT)


def _pick_tile(n, target, mult=16):
    k = max(1, -(-n // target))
    t = -(-n // k)
    return -(-t // mult) * mult


def _silu(x):
    return x * (1.0 / (1.0 + jnp.exp(-x)))


def _sigmoid(x):
    return 1.0 / (1.0 + jnp.exp(-x))


def _norm_matmul_kernel(x_ref, g_ref, w_ref, o_ref, xn_ref):
    @pl.when(pl.program_id(1) == 0)
    def _():
        x = x_ref[...]
        ms = jnp.mean(x * x, axis=-1, keepdims=True)
        xn_ref[...] = (x * lax.rsqrt(ms + NORM_EPS) * g_ref[...]).astype(BF16)

    o_ref[...] = jnp.dot(xn_ref[...], w_ref[...], preferred_element_type=F32).astype(o_ref.dtype)


def _norm_matmul(x, g, w, tm, tn):
    n, d = x.shape
    m = w.shape[1]
    tm = min(tm, -(-n // HALO) * HALO)
    return pl.pallas_call(
        _norm_matmul_kernel,
        grid=(pl.cdiv(n, tm), pl.cdiv(m, tn)),
        in_specs=[pl.BlockSpec((tm, d), lambda i, j: (i, 0)),
                  pl.BlockSpec((1, d), lambda i, j: (0, 0)),
                  pl.BlockSpec((d, tn), lambda i, j: (0, j))],
        out_specs=pl.BlockSpec((tm, tn), lambda i, j: (i, j)),
        out_shape=jax.ShapeDtypeStruct((n, m), BF16),
        scratch_shapes=[pltpu.VMEM((tm, d), BF16)],
        compiler_params=_cp("arbitrary", "arbitrary"),
        name="norm_matmul",
    )(x, g.reshape(1, d), w)


def _conv3(x, prev_row, next_row, w, row0, t_len):
    n = x.shape[0]
    li = lax.broadcasted_iota(jnp.int32, (n, 1), 0)
    gi = li + row0
    xm = pltpu.roll(x, 1, 0)
    xm = jnp.where(li == 0, prev_row, xm)
    xm = jnp.where(gi >= 1, xm, 0.0)
    xp = pltpu.roll(x, n - 1, 0)
    xp = jnp.where(li == n - 1, next_row, xp)
    xp = jnp.where(gi + 1 < t_len, xp, 0.0)
    return xm * w[0:1] + x * w[1:2] + xp * w[2:3]


def _last_row(halo_ref):
    return halo_ref[0].astype(F32)[HALO - 1:HALO]


def _first_row(halo_ref):
    return halo_ref[0].astype(F32)[0:1]


def _halo_specs(tr, cw, t_len, col_blk):
    nh = tr // HALO
    last = t_len // HALO - 1
    main = pl.BlockSpec((1, tr, cw), lambda b, i, j: (b, i, col_blk + j))
    prev = pl.BlockSpec((1, HALO, cw), lambda b, i, j: (b, jnp.maximum(i * nh - 1, 0), col_blk + j))
    nxt = pl.BlockSpec((1, HALO, cw), lambda b, i, j: (b, jnp.minimum((i + 1) * nh, last), col_blk + j))
    return main, prev, nxt


def _sc_kernel(bg_ref, cg_ref, xi_ref, cgp_ref, xip_ref, cgn_ref, xin_ref, w_ref, o_ref, *, tr, t_len):
    row0 = pl.program_id(1) * tr
    u = cg_ref[0].astype(F32) * xi_ref[0].astype(F32)
    prev = _last_row(cgp_ref) * _last_row(xip_ref)
    nxt = _first_row(cgn_ref) * _first_row(xin_ref)
    y = _conv3(u, prev, nxt, w_ref[...], row0, t_len)
    o_ref[0] = (bg_ref[0].astype(F32) * y).astype(o_ref.dtype)


def _sc_branch(proj, w, tr):
    b, t, _ = proj.shape
    cw = BRANCH_W
    c0 = COL_SC // cw
    bg, _, _ = _halo_specs(tr, cw, t, c0)
    cg, cgp, cgn = _halo_specs(tr, cw, t, c0 + 1)
    xi, xip, xin = _halo_specs(tr, cw, t, c0 + 2)
    return pl.pallas_call(
        functools.partial(_sc_kernel, tr=tr, t_len=t),
        grid=(b, pl.cdiv(t, tr), 1),
        in_specs=[bg, cg, xi, cgp, xip, cgn, xin, pl.BlockSpec((3, cw), lambda b_, i, j: (0, 0))],
        out_specs=pl.BlockSpec((1, tr, cw), lambda b_, i, j: (b_, i, 0)),
        out_shape=jax.ShapeDtypeStruct((b, t, cw), BF16),
        compiler_params=_cp("arbitrary", "arbitrary", "arbitrary"),
        name="sc_branch",
    )(proj, proj, proj, proj, proj, proj, proj, w)


def _fnet_kernel(a_ref, u_ref, cs_ref, o_ref, *, tmf):
    p = jnp.dot(a_ref[0], u_ref[0], preferred_element_type=F32)
    pc = p[:tmf].astype(BF16)
    ps = p[tmf:].astype(BF16)
    y = jnp.dot(pc, cs_ref[0], preferred_element_type=F32) - jnp.dot(ps, cs_ref[1], preferred_element_type=F32)
    o_ref[0] = y.astype(o_ref.dtype)


def _fnet_tables(t, tmf):
    nm = -(-t // tmf)
    j = jnp.arange(nm * tmf, dtype=jnp.int32)[:, None]
    k = jnp.arange(t, dtype=jnp.int32)[None, :]
    ang = ((j * k) % t).astype(F32) * (2.0 * math.pi / t)
    valid = j < t
    c = jnp.where(valid, jnp.cos(ang), 0.0).astype(BF16).reshape(nm, tmf, t)
    s = jnp.where(valid, jnp.sin(ang), 0.0).astype(BF16).reshape(nm, tmf, t)
    a = jnp.concatenate([c, s], axis=1)
    jj = jnp.arange(BRANCH_W, dtype=jnp.int32)[:, None]
    kk = jnp.arange(BRANCH_W, dtype=jnp.int32)[None, :]
    same = (jj // FNET_GW) == (kk // FNET_GW)
    ang2 = (((jj % FNET_GW) * (kk % FNET_GW)) % FNET_GW).astype(F32) * (2.0 * math.pi / FNET_GW)
    scale = 1.0 / math.sqrt(t * FNET_GW)
    cc = jnp.where(same, jnp.cos(ang2), 0.0) * scale
    sc = jnp.where(same, jnp.sin(ang2), 0.0) * scale
    return a, jnp.stack([cc, sc]).astype(BF16)


def _fnet_branch(proj, tmf):
    b, t, _ = proj.shape
    a, cs = _fnet_tables(t, tmf)
    nm = a.shape[0]
    return pl.pallas_call(
        functools.partial(_fnet_kernel, tmf=tmf),
        grid=(nm, b),
        in_specs=[pl.BlockSpec((1, 2 * tmf, t), lambda i, j: (i, 0, 0)),
                  pl.BlockSpec((1, t, BRANCH_W), lambda i, j: (j, 0, COL_FN // BRANCH_W)),
                  pl.BlockSpec((2, BRANCH_W, BRANCH_W), lambda i, j: (0, 0, 0))],
        out_specs=pl.BlockSpec((1, tmf, BRANCH_W), lambda i, j: (j, i, 0)),
        out_shape=jax.ShapeDtypeStruct((b, t, BRANCH_W), BF16),
        compiler_params=_cp("arbitrary", "arbitrary"),
        name="fnet",
    )(a, proj, cs)


def _hyena_filter_kernel(w1t_ref, w1c_ref, w1s_ref, b1_ref, w2_ref, b2_ref, fq_ref, w3f_ref, w3b_ref, dl_ref,
                         o_ref, hid_ref, *, t):
    @pl.when(pl.program_id(0) == 0)
    def _():
        n = lax.broadcasted_iota(jnp.int32, (t, 1), 0).astype(F32)
        tt = n * (1.0 / (t - 1))
        wv = n * (2.0 * math.pi / t)
        bands = (HYENA_EMB - 1) // 2
        fi = lax.broadcasted_iota(jnp.int32, (1, bands), 1).astype(F32)
        fr = 1e-4 + fi * ((bands - 1 - 1e-4) / (bands - 1))
        arg = wv * fr
        pre = (tt * w1t_ref[...]
               + jnp.dot(jnp.cos(arg), w1c_ref[...], precision=HI, preferred_element_type=F32)
               - jnp.dot(jnp.sin(arg), w1s_ref[...], precision=HI, preferred_element_type=F32)
               + b1_ref[...])
        fq = fq_ref[...]
        h1 = jnp.sin(fq * pre)
        h2 = jnp.sin(fq * (jnp.dot(h1, w2_ref[...], precision=HI, preferred_element_type=F32) + b2_ref[...]))
        hid_ref[...] = h2

    n = lax.broadcasted_iota(jnp.int32, (t, 1), 0)
    tt = n.astype(F32) * (1.0 / (t - 1))
    dec = jnp.exp(-tt * dl_ref[...])
    hid = hid_ref[...]
    hf = jnp.dot(hid, w3f_ref[...], precision=HI, preferred_element_type=F32) * dec
    hb = jnp.dot(hid, w3b_ref[...], precision=HI, preferred_element_type=F32) * dec
    hb = jnp.where(n >= 1, hb, 0.0)
    l1 = jnp.sum(jnp.abs(hf), axis=0, keepdims=True) + jnp.sum(jnp.abs(hb), axis=0, keepdims=True)
    inv = 1.0 / l1
    o_ref[0, 0] = (hf * inv).astype(o_ref.dtype)
    o_ref[0, 1] = (hb * inv).astype(o_ref.dtype)


def _hyena_filters(t, w1, b1, w2, b2, w3, freq):
    cw = 128
    nb = BRANCH_W // cw
    bands = (HYENA_EMB - 1) // 2
    nf = w2.shape[0]
    max_decay = math.log(HYENA_TARGET) / HYENA_FAST
    min_decay = math.log(HYENA_TARGET) / HYENA_SLOW
    deltas = jnp.abs(jnp.linspace(min_decay, max_decay, BRANCH_W, dtype=F32)).reshape(1, BRANCH_W)
    full = lambda shape: pl.BlockSpec(shape, lambda g: (0,) * len(shape))
    return pl.pallas_call(
        functools.partial(_hyena_filter_kernel, t=t),
        grid=(HYENA_ORDER * nb,),
        in_specs=[full((1, nf)), full((bands, nf)), full((bands, nf)), full((1, nf)), full((nf, nf)),
                  full((1, nf)), full((1, nf)),
                  pl.BlockSpec((nf, cw), lambda g: (0, (g // nb) * 2 * nb + g % nb)),
                  pl.BlockSpec((nf, cw), lambda g: (0, (g // nb) * 2 * nb + nb + g % nb)),
                  pl.BlockSpec((1, cw), lambda g: (0, g % nb))],
        out_specs=pl.BlockSpec((1, 2, t, cw), lambda g: (g // nb, 0, 0, g % nb)),
        out_shape=jax.ShapeDtypeStruct((HYENA_ORDER, 2, t, BRANCH_W), BF16),
        scratch_shapes=[pltpu.VMEM((t, nf), F32)],
        compiler_params=_cp("arbitrary"),
        name="hyena_filter",
    )(w1[0:1], w1[1:1 + bands], w1[1 + bands:], b1.reshape(1, nf), w2, b2.reshape(1, nf), freq.reshape(1, nf),
      w3, w3, deltas)


def _hyena_tables(t, n_fft, tf):
    nfb = (n_fft // 2) // tf
    f = jnp.arange(n_fft // 2, dtype=jnp.int32)[:, None]
    n = jnp.arange(t, dtype=jnp.int32)[None, :]
    ang = (((2 * f + 1) * n) % (2 * n_fft)).astype(F32) * (math.pi / n_fft)
    c = jnp.cos(ang).astype(BF16).reshape(nfb, tf, t)
    s = (-jnp.sin(ang)).astype(BF16).reshape(nfb, tf, t)
    fwd = jnp.concatenate([c, s], axis=1)
    inv = fwd.reshape(n_fft, t).T
    return fwd, inv


def _hy_spec_kernel(a_ref, x_ref, o_ref):
    o_ref[0, 0] = jnp.dot(a_ref[0], x_ref[0], preferred_element_type=F32)


def _hy_filter_spectrum(fwd, filt):
    nfb, tf2, t = fwd.shape
    nb = filt.shape[0]
    return pl.pallas_call(
        _hy_spec_kernel,
        grid=(nfb, nb),
        in_specs=[pl.BlockSpec((1, tf2, t), lambda i, j: (i, 0, 0)),
                  pl.BlockSpec((1, t, BRANCH_W), lambda i, j: (j, 0, 0))],
        out_specs=pl.BlockSpec((1, 1, tf2, BRANCH_W), lambda i, j: (j, i, 0, 0)),
        out_shape=jax.ShapeDtypeStruct((nb, nfb, tf2, BRANCH_W), F32),
        compiler_params=_cp("arbitrary", "arbitrary"),
        name="hyena_filter_spectrum",
    )(fwd, filt)


def _hy_fwd_kernel(a_ref, x_ref, kf_ref, kb_ref, o_ref, *, tf, scale):
    p = jnp.dot(a_ref[0], x_ref[0], preferred_element_type=F32)
    ur, ui = p[:tf], p[tf:]
    kr = (kf_ref[0, 0, :tf] + kb_ref[0, 0, :tf]) * scale
    ki = (kf_ref[0, 0, tf:] - kb_ref[0, 0, tf:]) * scale
    o_ref[0, 0, :tf] = (ur * kr - ui * ki).astype(o_ref.dtype)
    o_ref[0, 0, tf:] = (ur * ki + ui * kr).astype(o_ref.dtype)


def _hy_forward(fwd, x, x_col, spec, order, n_fft):
    nfb, tf2, t = fwd.shape
    b = x.shape[0]
    return pl.pallas_call(
        functools.partial(_hy_fwd_kernel, tf=tf2 // 2, scale=2.0 / n_fft),
        grid=(nfb, b),
        in_specs=[pl.BlockSpec((1, tf2, t), lambda i, j: (i, 0, 0)),
                  pl.BlockSpec((1, t, BRANCH_W), lambda i, j: (j, 0, x_col)),
                  pl.BlockSpec((1, 1, tf2, BRANCH_W), lambda i, j: (2 * order, i, 0, 0)),
                  pl.BlockSpec((1, 1, tf2, BRANCH_W), lambda i, j: (2 * order + 1, i, 0, 0))],
        out_specs=pl.BlockSpec((1, 1, tf2, BRANCH_W), lambda i, j: (j, i, 0, 0)),
        out_shape=jax.ShapeDtypeStruct((b, nfb, tf2, BRANCH_W), BF16),
        compiler_params=_cp("arbitrary", "arbitrary"),
        name="hyena_fwd",
    )(fwd, x, spec, spec)


def _hy_inv_kernel(a_ref, y_ref, g_ref, v_ref, bias_ref, o_ref):
    conv = jnp.dot(a_ref[...], y_ref[0], preferred_element_type=F32)
    v = v_ref[0].astype(F32)
    o_ref[0] = (g_ref[0].astype(F32) * (conv + v * bias_ref[0])).astype(o_ref.dtype)


def _hy_inverse(inv, yhat, gate, gate_col, v, v_col, bias, order, tt):
    t, n_fft = inv.shape
    b = yhat.shape[0]
    yh = yhat.reshape(b, n_fft, BRANCH_W)
    return pl.pallas_call(
        _hy_inv_kernel,
        grid=(pl.cdiv(t, tt), b),
        in_specs=[pl.BlockSpec((tt, n_fft), lambda i, j: (i, 0)),
                  pl.BlockSpec((1, n_fft, BRANCH_W), lambda i, j: (j, 0, 0)),
                  pl.BlockSpec((1, tt, BRANCH_W), lambda i, j: (j, i, gate_col)),
                  pl.BlockSpec((1, tt, BRANCH_W), lambda i, j: (j, i, v_col)),
                  pl.BlockSpec((1, 1, BRANCH_W), lambda i, j: (order, 0, 0))],
        out_specs=pl.BlockSpec((1, tt, BRANCH_W), lambda i, j: (j, i, 0)),
        out_shape=jax.ShapeDtypeStruct((b, t, BRANCH_W), BF16),
        compiler_params=_cp("arbitrary", "arbitrary"),
        name="hyena_inv",
    )(inv, yh, gate, v, bias.reshape(HYENA_ORDER, 1, BRANCH_W))


def _softplus(x):
    return jnp.maximum(x, 0.0) + jnp.log(1.0 + jnp.exp(-jnp.abs(x)))


def _ssd_kernel(xs_ref, b_ref, c_ref, z_ref, dt_ref, dtb_ref, an_ref, dsk_ref, nw_ref, tri_ref, exp_ref, o_ref,
                yf_scr, yb_scr, *, t_len, nc, unroll):
    q = SSM_CHUNK
    gw = SSM_HPG * SSM_HEAD_DIM
    front = q - N_META
    ri = lax.broadcasted_iota(jnp.int32, (q, q), 0)
    ci = lax.broadcasted_iota(jnp.int32, (q, q), 1)
    tri = (ri >= ci, ri <= ci)
    head_of_lane = lax.broadcasted_iota(jnp.int32, (1, gw), 1) // SSM_HEAD_DIM
    lane = lax.broadcasted_iota(jnp.int32, (1, q), 1)
    dt_lane_mask = (lane < 2 * SSM_HPG).astype(F32)
    low_half = lane < SSM_HEAD_DIM
    dtb = dtb_ref[...]
    an = an_ref[...]
    masked_out = -1e30

    def widen(cols):
        return jnp.concatenate([jnp.where(low_half, cols[0], cols[1]), jnp.where(low_half, cols[2], cols[3])], axis=1)

    def cumsum(d, x):
        hi = x.astype(BF16)
        r1 = x - hi.astype(F32)
        mid = r1.astype(BF16)
        lo = (r1 - mid.astype(F32)).astype(BF16)
        s = jnp.dot(tri_ref[d], jnp.concatenate([hi, mid, lo], axis=1), preferred_element_type=F32)
        return s[:, :q] + s[:, q:2 * q] + s[:, 2 * q:]

    def load(ref, r0, nrows):
        return ref[0, pl.ds(r0, nrows), :]

    def dt_of(raw):
        return _softplus(raw.astype(F32) + dtb) * dt_lane_mask

    def chunk_data(c):
        r0 = pl.multiple_of(c * q - front, HALO)
        return load(xs_ref, r0, q), load(b_ref, r0, q), load(c_ref, r0, q), dt_of(load(dt_ref, r0, q))

    def chunk0_data():
        def pad(x):
            return jnp.concatenate([jnp.zeros((front, x.shape[1]), x.dtype), x], axis=0)
        return (pad(load(xs_ref, 0, N_META)), pad(load(b_ref, 0, N_META)), pad(load(c_ref, 0, N_META)),
                pad(dt_of(load(dt_ref, 0, N_META))))

    def stage_local(job, cs_n):
        (xs, bm, cm, dt), d = job
        lanes = [d * SSM_HPG + r for r in range(SSM_HPG)]
        cs_cols = [jnp.broadcast_to(cs_n[:, ln:ln + 1], (q, q)) for ln in lanes]
        cs_w = widen(cs_cols)
        dt_w = jnp.dot(dt.astype(BF16), exp_ref[d], preferred_element_type=F32).astype(BF16)
        tot = cs_w[q - 1:q, :] if d == 0 else cs_w[0:1, :]
        xt = xs * dt_w
        xd = xt * jnp.exp2(tot - cs_w).astype(BF16)
        s_new = lax.dot_general(bm, xd, (((0,), (0,)), ((), ())), preferred_element_type=F32)
        cb = lax.dot_general(cm, bm, (((1,), (1,)), ((), ())), preferred_element_type=F32)
        return cs_cols, xt, cb, cm, s_new, jnp.exp2(cs_w), jnp.exp2(tot), cs_n.T

    def stage_carry(loc, h):
        _, _, _, cm, s_new, ecs, etot, _ = loc
        y_off = jnp.dot(cm, h.astype(BF16), preferred_element_type=F32) * ecs
        return y_off, h * etot + s_new

    def stage_diag(job, cs_n, loc, y_off):
        d = job[1]
        cs_cols, xt, cb = loc[:3]
        cs_t = loc[7]
        ms, xm = [], []
        for r in range(SSM_HPG):
            ln = d * SSM_HPG + r
            dec = jnp.exp2(jnp.where(tri[d], cs_cols[r] - cs_t[ln:ln + 1, :], masked_out))
            ms.append((cb * dec).astype(BF16))
            xm.append(jnp.where(head_of_lane == r, xt, jnp.zeros_like(xt)))
        return y_off + jnp.dot(jnp.concatenate(ms, axis=1), jnp.concatenate(xm, axis=0), preferred_element_type=F32)

    def process_all(fwd_data, bwd_data, hf, hb):
        jobs = [(x, 0) for x in fwd_data] + [(x, 1) for x in bwd_data]
        cs = [cumsum(d, data[3] * an) for data, d in jobs]
        ys, pending = [], None
        for k, (job, c) in enumerate(zip(jobs, cs)):
            loc = stage_local(job, c)
            if pending is not None:
                ys.append(stage_diag(*pending))
            y_off, h = stage_carry(loc, hf if job[1] == 0 else hb)
            hf, hb = (h, hb) if job[1] == 0 else (hf, h)
            pending = (job, c, loc, y_off)
        ys.append(stage_diag(*pending))
        return ys[:len(fwd_data)], ys[len(fwd_data):], hf, hb

    def finish(y, r_out, nrows):
        y = y + load(xs_ref, r_out, nrows).astype(F32) * dsk_ref[...]
        y = y * _silu(load(z_ref, r_out, nrows).astype(F32))
        y = y * lax.rsqrt(jnp.mean(y * y, axis=-1, keepdims=True) + NORM_EPS)
        o_ref[0, pl.ds(r_out, nrows), :] = (y * nw_ref[...]).astype(o_ref.dtype)

    def rows(c):
        return pl.ds(pl.multiple_of(c * q, q), q)

    h0 = jnp.zeros((SSM_STATE, gw), F32)
    (y,), _, hf, _ = process_all([chunk0_data()], [], h0, h0)
    yf_scr[0:q, :] = y

    def scan_body(i, carry):
        hf, hb = carry
        cf = [1 + unroll * i + k for k in range(unroll)]
        cb_ = [nc - c for c in cf]
        yf, yb, hf, hb = process_all([chunk_data(c) for c in cf], [chunk_data(c) for c in cb_], hf, hb)
        for c, y in zip(cf, yf):
            yf_scr[rows(c), :] = y
        for c, y in zip(cb_, yb):
            yb_scr[rows(c), :] = y
        return hf, hb

    _, hb = lax.fori_loop(0, (nc - 1) // unroll, scan_body, (hf, h0))
    _, (y,), _, _ = process_all([], [chunk0_data()], h0, hb)
    yb_scr[0:q, :] = y

    finish(yf_scr[front:q, :] + yb_scr[front:q, :], 0, N_META)

    def finish_body(c, carry):
        finish(yf_scr[rows(c), :] + yb_scr[rows(c), :], pl.multiple_of(c * q - front, HALO), q)
        return carry

    lax.fori_loop(1, nc, finish_body, 0)


def _ssd_branch(xbc, proj, dt_bias, a_log, d_skip, norm_w):
    b, t, _ = xbc.shape
    q = SSM_CHUNK
    gw = SSM_HPG * SSM_HEAD_DIM
    nc = (q - N_META + t) // q
    assert nc * q == q - N_META + t, "sequence length minus meta tokens must be a multiple of the SSD chunk"
    a = -jnp.exp(a_log.astype(F32)).reshape(2, SSM_GROUPS, SSM_HPG)
    dtb = dt_bias.astype(F32).reshape(2, SSM_GROUPS, SSM_HPG)
    pad = lambda v: jnp.pad(jnp.transpose(v, (1, 0, 2)).reshape(SSM_GROUPS, 1, 2 * SSM_HPG),
                            ((0, 0), (0, 0), (0, q - 2 * SSM_HPG)))
    a_n = pad(a * math.log2(math.e))
    dtb_n = pad(dtb)
    ri = jnp.arange(q, dtype=jnp.int32)[:, None]
    ci = jnp.arange(q, dtype=jnp.int32)[None, :]
    tri = jnp.stack([ri >= ci, ri <= ci]).astype(BF16)
    cj = jnp.arange(gw, dtype=jnp.int32)[None, :] // SSM_HEAD_DIM
    expand = jnp.stack([ri == cj, ri == cj + SSM_HPG]).astype(BF16)
    dsk = jnp.repeat(d_skip.astype(F32).reshape(SSM_GROUPS, 1, SSM_HPG), SSM_HEAD_DIM, axis=-1)
    nw = norm_w.astype(F32).reshape(SSM_GROUPS, 1, gw)
    par = lambda shape: pl.BlockSpec((None,) + shape, lambda i, g: (g, 0, 0))
    return pl.pallas_call(
        functools.partial(_ssd_kernel, t_len=t, nc=nc, unroll=math.gcd(nc - 1, SSD_UNROLL)),
        grid=(b, SSM_GROUPS),
        in_specs=[pl.BlockSpec((1, t, gw), lambda i, g: (i, 0, g)),
                  pl.BlockSpec((1, t, SSM_STATE), lambda i, g: (i, 0, 4 + g)),
                  pl.BlockSpec((1, t, SSM_STATE), lambda i, g: (i, 0, 6 + g)),
                  pl.BlockSpec((1, t, gw), lambda i, g: (i, 0, COL_Z // gw + g)),
                  pl.BlockSpec((1, t, q), lambda i, g: (i, 0, COL_DT // q + g)),
                  par((1, q)), par((1, q)), par((1, gw)), par((1, gw)),
                  pl.BlockSpec((2, q, q), lambda i, g: (0, 0, 0)),
                  pl.BlockSpec((2, q, gw), lambda i, g: (0, 0, 0))],
        out_specs=pl.BlockSpec((1, t, gw), lambda i, g: (i, 0, g)),
        out_shape=jax.ShapeDtypeStruct((b, t, BRANCH_W), BF16),
        scratch_shapes=[pltpu.VMEM((nc * q, gw), F32), pltpu.VMEM((nc * q, gw), F32)],
        compiler_params=_cp("arbitrary", "arbitrary"),
        name="ssd",
    )(xbc, xbc, xbc, proj, proj, dtb_n, a_n, dsk, nw, tri, expand)


def _merge_kernel(h_ref, y0_ref, y1_ref, y2_ref, y3_ref, g0_ref, g1_ref, g2_ref, g3_ref, wb_ref, wo_ref, o_ref):
    merged = None
    for k, (y_ref, g_ref) in enumerate(zip((y0_ref, y1_ref, y2_ref, y3_ref), (g0_ref, g1_ref, g2_ref, g3_ref))):
        gate = _sigmoid(g_ref[...].astype(F32))
        term = gate * jnp.dot(y_ref[...], wb_ref[k], preferred_element_type=F32)
        merged = term if merged is None else merged + term
    o_ref[...] = h_ref[...] + jnp.dot(merged.astype(BF16), wo_ref[...], preferred_element_type=F32)


def _merge(h, ys, proj, wb, wo, tm):
    n, d = h.shape
    nb = len(ys)
    row = lambda w: pl.BlockSpec((tm, w), lambda i: (i, 0))
    return pl.pallas_call(
        _merge_kernel,
        grid=(pl.cdiv(n, tm),),
        in_specs=[row(d)] + [row(BRANCH_W)] * nb
                 + [pl.BlockSpec((tm, d), functools.partial(lambda i, k: (i, COL_GATE // d + k), k=k)) for k in range(nb)]
                 + [pl.BlockSpec((nb, BRANCH_W, d), lambda i: (0, 0, 0)),
                    pl.BlockSpec((d, d), lambda i: (0, 0))],
        out_specs=row(d),
        out_shape=jax.ShapeDtypeStruct((n, d), F32),
        compiler_params=_cp("arbitrary"),
        name="merge",
    )(h, *ys, *([proj] * nb), wb, wo)


def _rms_rows(x, g):
    return x * lax.rsqrt(jnp.mean(x * x, axis=-1, keepdims=True) + NORM_EPS) * g


def _stage_normed_rows(xs_ref, x_ref, xp_ref, xn_ref, g, row0, tr, t_len):
    li = lax.broadcasted_iota(jnp.int32, (tr, 1), 0)
    xs_ref[HALO:HALO + tr, :] = jnp.where(row0 + li < t_len, _rms_rows(x_ref[0], g), 0.0).astype(BF16)
    xs_ref[0:HALO, :] = jnp.where(row0 > 0, _rms_rows(xp_ref[0], g), 0.0).astype(BF16)
    xs_ref[HALO + tr:, :] = jnp.where(row0 + tr < t_len, _rms_rows(xn_ref[0], g), 0.0).astype(BF16)


def _conv3_rows(u, w):
    n = u.shape[0]
    return pltpu.roll(u, 1, 0) * w[0:1] + u * w[1:2] + pltpu.roll(u, n - 1, 0) * w[2:3]


def _row_halo_specs(tr, d, t_len):
    nh = tr // HALO
    last = t_len // HALO - 1
    return (pl.BlockSpec((1, tr, d), lambda b, i, j: (b, i, 0)),
            pl.BlockSpec((1, HALO, d), lambda b, i, j: (b, jnp.maximum(i * nh - 1, 0), 0)),
            pl.BlockSpec((1, HALO, d), lambda b, i, j: (b, jnp.minimum((i + 1) * nh, last), 0)))


def _conv_proj_kernel(x_ref, xp_ref, xn_ref, g_ref, w_ref, c_ref, b_ref, o_ref, xs_ref, *, tr, t_len, n_act):
    j = pl.program_id(2)
    row0 = pl.program_id(1) * tr

    @pl.when(j == 0)
    def _():
        _stage_normed_rows(xs_ref, x_ref, xp_ref, xn_ref, g_ref[...], row0, tr, t_len)

    u = jnp.dot(xs_ref[...], w_ref[...], preferred_element_type=F32)
    y = _conv3_rows(u, c_ref[...])[HALO:HALO + tr] + b_ref[...]

    @pl.when(j < n_act)
    def _():
        o_ref[0] = _silu(y).astype(o_ref.dtype)

    @pl.when(j >= n_act)
    def _():
        o_ref[0] = y.astype(o_ref.dtype)


def _conv_proj(h, g, w, taps, bias, n_act, tr, tn):
    b, t, d = h.shape
    m = w.shape[1]
    main, prev, nxt = _row_halo_specs(tr, d, t)
    return pl.pallas_call(
        functools.partial(_conv_proj_kernel, tr=tr, t_len=t, n_act=n_act),
        grid=(b, pl.cdiv(t, tr), m // tn),
        in_specs=[main, prev, nxt,
                  pl.BlockSpec((1, d), lambda b_, i, j: (0, 0)),
                  pl.BlockSpec((d, tn), lambda b_, i, j: (0, j)),
                  pl.BlockSpec((3, tn), lambda b_, i, j: (0, j)),
                  pl.BlockSpec((1, tn), lambda b_, i, j: (0, j))],
        out_specs=pl.BlockSpec((1, tr, tn), lambda b_, i, j: (b_, i, j)),
        out_shape=jax.ShapeDtypeStruct((b, t, m), BF16),
        scratch_shapes=[pltpu.VMEM((tr + 2 * HALO, d), BF16)],
        compiler_params=_cp("arbitrary", "arbitrary", "arbitrary"),
        name="conv_proj",
    )(h, h, h, g.reshape(1, d), w, taps, bias)


def _ffn_kernel(x_ref, xp_ref, xn_ref, g_ref, wa_ref, wv_ref, ca_ref, cv_ref, wd_ref, o_ref, xs_ref, gate_ref,
                acc_ref, *, tr, t_len, subs):
    j = pl.program_id(2)
    row0 = pl.program_id(1) * tr

    @pl.when(j == 0)
    def _():
        _stage_normed_rows(xs_ref, x_ref, xp_ref, xn_ref, g_ref[...], row0, tr, t_len)

    xs = xs_ref[...]
    for s0, w in subs:
        a = _conv3_rows(jnp.dot(xs, wa_ref[:, s0:s0 + w], preferred_element_type=F32), ca_ref[:, s0:s0 + w])
        v = _conv3_rows(jnp.dot(xs, wv_ref[:, s0:s0 + w], preferred_element_type=F32), cv_ref[:, s0:s0 + w])
        gate_ref[:, s0:s0 + w] = (_silu(a) * v)[HALO:HALO + tr].astype(BF16)
    part = jnp.dot(gate_ref[...], wd_ref[...], preferred_element_type=F32)

    @pl.when(j == 0)
    def _():
        acc_ref[...] = x_ref[0] + part

    @pl.when(j > 0)
    def _():
        acc_ref[...] += part

    @pl.when(j == pl.num_programs(2) - 1)
    def _():
        o_ref[0] = acc_ref[...]


def _ffn(h, g, w_up, conv_w, w_down, tr, tc):
    b, t, d = h.shape
    dff = w_down.shape[0]
    ncb = dff // tc
    subs = tuple((s0, min(256, tc - s0)) for s0 in range(0, tc, 256))
    main, prev, nxt = _row_halo_specs(tr, d, t)
    return pl.pallas_call(
        functools.partial(_ffn_kernel, tr=tr, t_len=t, subs=subs),
        grid=(b, pl.cdiv(t, tr), ncb),
        in_specs=[main, prev, nxt,
                  pl.BlockSpec((1, d), lambda b_, i, j: (0, 0)),
                  pl.BlockSpec((d, tc), lambda b_, i, j: (0, j)),
                  pl.BlockSpec((d, tc), lambda b_, i, j: (0, ncb + j)),
                  pl.BlockSpec((3, tc), lambda b_, i, j: (0, j)),
                  pl.BlockSpec((3, tc), lambda b_, i, j: (0, ncb + j)),
                  pl.BlockSpec((tc, d), lambda b_, i, j: (j, 0))],
        out_specs=pl.BlockSpec((1, tr, d), lambda b_, i, j: (b_, i, 0)),
        out_shape=jax.ShapeDtypeStruct((b, t, d), F32),
        scratch_shapes=[pltpu.VMEM((tr + 2 * HALO, d), BF16), pltpu.VMEM((tr, tc), BF16), pltpu.VMEM((tr, d), F32)],
        compiler_params=_cp("arbitrary", "arbitrary", "arbitrary"),
        name="ffn",
    )(h, h, h, g.reshape(1, d), w_up, w_up, conv_w, conv_w, w_down)


def _final_norm_kernel(x_ref, g_ref, o_ref):
    x = x_ref[...]
    ms = jnp.mean(x * x, axis=-1, keepdims=True)
    o_ref[...] = x * lax.rsqrt(ms + NORM_EPS) * g_ref[...]


def _final_norm(x, g, tm):
    n, d = x.shape
    return pl.pallas_call(
        _final_norm_kernel,
        grid=(pl.cdiv(n, tm),),
        in_specs=[pl.BlockSpec((tm, d), lambda i: (i, 0)), pl.BlockSpec((1, d), lambda i: (0, 0))],
        out_specs=pl.BlockSpec((tm, d), lambda i: (i, 0)),
        out_shape=jax.ShapeDtypeStruct((n, d), F32),
        compiler_params=_cp("arbitrary"),
        name="final_norm",
    )(x, g.reshape(1, d))


def _prep_w_in(w_in):
    d = w_in.shape[0]
    o_z, o_xbc, o_dt = BRANCH_W, 2 * BRANCH_W, 2 * BRANCH_W + 1024
    ndt = 2 * SSM_GROUPS * SSM_HPG
    o_hy = o_dt + ndt
    o_sc = o_hy + 3 * BRANCH_W
    o_gate = o_sc + 3 * BRANCH_W
    w_dt = w_in[:, o_dt:o_hy].reshape(d, 2, SSM_GROUPS, SSM_HPG)
    w_dt = jnp.transpose(w_dt, (0, 2, 1, 3)).reshape(d, SSM_GROUPS, 2 * SSM_HPG)
    w_dt = jnp.pad(w_dt, ((0, 0), (0, 0), (0, SSM_CHUNK - 2 * SSM_HPG))).reshape(d, SSM_GROUPS * SSM_CHUNK)
    w_conv = jnp.concatenate([w_in[:, o_xbc:o_dt], w_in[:, o_hy:o_sc]], axis=1).astype(BF16)
    w_plain = jnp.concatenate([w_in[:, :o_xbc], w_in[:, o_gate:], w_in[:, o_sc:o_gate], w_dt], axis=1).astype(BF16)
    return w_conv, w_plain


def _hyena_fft_len(t):
    tf = 528
    nfb = -(-(2 * t - 1) // (2 * tf))
    return 2 * tf * nfb, tf


def _run_trunk(x, meta_tokens, norm_final, layers):
    b, seq, d = x.shape
    t = seq + N_META
    meta = jnp.broadcast_to(meta_tokens[None].astype(x.dtype), (b, N_META, d))
    h = jnp.concatenate([meta, x], axis=1).reshape(b * t, d)
    tr = _pick_tile(t, ROW_TILE)
    n_fft, tf = _hyena_fft_len(t)
    hy_fwd, hy_inv = _hyena_tables(t, n_fft, tf)
    tmf = tti = tr
    tok = min(TOKEN_TILE, b * t)
    hv, hx1, hx2 = (CV_HY // BRANCH_W + k for k in range(3))
    for p in layers:
        w_conv, w_plain = p["w_in"]
        taps = jnp.concatenate([p["ssm_conv_w"], p["hyena_conv_w"]], axis=1)
        bias = jnp.concatenate([p["ssm_conv_b"], jnp.zeros((3 * BRANCH_W,), F32)]).reshape(1, N_CONV)
        projc = _conv_proj(h.reshape(b, t, d), p["norm_mix"], w_conv, taps, bias, CV_HY // BRANCH_W, tr, BRANCH_W)
        proj = _norm_matmul(h, p["norm_mix"], w_plain, tok, 1152).reshape(b, t, N_PLAIN)
        y_fn = _fnet_branch(proj, tmf)
        y_ssm = _ssd_branch(projc, proj, p["ssm_dt_bias"], p["ssm_a_log"], p["ssm_d"], p["ssm_norm"])
        filt = _hyena_filters(t, p["hyena_w1"], p["hyena_b1"], p["hyena_w2"], p["hyena_b2"], p["hyena_w3"],
                              p["hyena_freq"]).reshape(2 * HYENA_ORDER, t, BRANCH_W)
        spec = _hy_filter_spectrum(hy_fwd, filt)
        z = _hy_inverse(hy_inv, _hy_forward(hy_fwd, projc, hv, spec, 0, n_fft), projc, hx1, projc, hv,
                        p["hyena_bias"], 0, tti)
        y_hy = _hy_inverse(hy_inv, _hy_forward(hy_fwd, z, 0, spec, 1, n_fft), projc, hx2, z, 0, p["hyena_bias"], 1, tti)
        y_sc = _sc_branch(proj, p["sc_conv_w"], tr)
        flat = lambda a: a.reshape(b * t, a.shape[-1])
        h = _merge(h, [flat(y_fn), flat(y_ssm), flat(y_hy), flat(y_sc)], flat(proj), p["w_branch"], p["w_out"], min(512, tok))
        h = _ffn(h.reshape(b, t, d), p["norm_ffn"], p["w_up"], p["ffn_conv_w"], p["w_down"], tr, 1408).reshape(b * t, d)
    out = _final_norm(h, norm_final, tok).reshape(b, t, d)
    return out[:, N_META:]


def kernel(x_prompt, x_sample, meta_tokens, norm_mix, w_in, ssm_conv_w, ssm_conv_b, ssm_dt_bias, ssm_a_log, ssm_d,
           ssm_norm, hyena_conv_w, hyena_w1, hyena_b1, hyena_w2, hyena_b2, hyena_w3, hyena_freq, hyena_bias,
           sc_conv_w, w_branch, w_out, norm_ffn, ffn_conv_w, w_up, w_down, norm_final):
    depth = w_in.shape[0]
    layers = []
    for l in range(depth):
        layers.append(dict(
            norm_mix=norm_mix[l], w_in=_prep_w_in(w_in[l]), ssm_conv_w=ssm_conv_w[l], ssm_conv_b=ssm_conv_b[l],
            ssm_dt_bias=ssm_dt_bias[l], ssm_a_log=ssm_a_log[l], ssm_d=ssm_d[l], ssm_norm=ssm_norm[l],
            hyena_conv_w=hyena_conv_w[l], hyena_w1=hyena_w1[l], hyena_b1=hyena_b1[l], hyena_w2=hyena_w2[l],
            hyena_b2=hyena_b2[l], hyena_w3=hyena_w3[l], hyena_freq=hyena_freq[l], hyena_bias=hyena_bias[l],
            sc_conv_w=sc_conv_w[l], w_branch=w_branch[l].astype(BF16), w_out=w_out[l].astype(BF16),
            norm_ffn=norm_ffn[l], ffn_conv_w=ffn_conv_w[l], w_up=w_up[l].astype(BF16), w_down=w_down[l].astype(BF16)))
    y_prompt = _run_trunk(x_prompt, meta_tokens, norm_final, layers)
    y_sample = _run_trunk(x_sample, meta_tokens, norm_final, layers)
    return (y_prompt, y_sample)
```

```python
import functools
import math

import jax
import jax.numpy as jnp
from jax import lax
from jax.experimental import pallas as pl
from jax.experimental.pallas import tpu as pltpu

F32 = jnp.float32
BF16 = jnp.bfloat16
HI = lax.Precision.HIGHEST

NORM_EPS = 1e-6
N_META = 16
BRANCH_W = 512
FNET_GW = 128
SSM_GROUPS = 2
SSM_HPG = 4
SSM_HEAD_DIM = 64
SSM_STATE = 128
SSM_CHUNK = 128
HYENA_ORDER = 2
HYENA_EMB = 33
HYENA_TARGET = 1e-2
HYENA_FAST = 0.3
HYENA_SLOW = 1.5

ROW_TILE = 700
TOKEN_TILE = 1024
MXU_COLS = 256
SSD_UNROLL = 4
HALO = 16
VMEM_LIMIT = 56 * 1024 * 1024

CV_XBC = 0
CV_HY = 1024
N_CONV = 2560
COL_FN = 0
COL_Z = 512
COL_GATE = 1024
COL_SC = 5120
COL_DT = 6656
N_PLAIN = 6912


def _cp(*sem):
    return pltpu.CompilerParams(dimension_semantics=sem, vmem_limit_bytes=VMEM_LIMIT)


def _pick_tile(n, target, mult=16):
    k = max(1, -(-n // target))
    t = -(-n // k)
    return -(-t // mult) * mult


def _silu(x):
    return x * (1.0 / (1.0 + jnp.exp(-x)))


def _sigmoid(x):
    return 1.0 / (1.0 + jnp.exp(-x))


def _norm_matmul_kernel(x_ref, g_ref, w_ref, o_ref, *, tn):
    x = x_ref[...]
    ms = jnp.mean(x * x, axis=-1, keepdims=True)
    xn = (x * lax.rsqrt(ms + NORM_EPS) * g_ref[...]).astype(BF16)
    for s0 in range(0, w_ref.shape[1], tn):
        o_ref[:, s0:s0 + tn] = jnp.dot(xn, w_ref[:, s0:s0 + tn], preferred_element_type=F32).astype(o_ref.dtype)


def _norm_matmul(x, g, w, tm, tn):
    n, d = x.shape
    m = w.shape[1]
    assert m % tn == 0
    tm = min(tm, -(-n // HALO) * HALO)
    return pl.pallas_call(
        functools.partial(_norm_matmul_kernel, tn=tn),
        grid=(pl.cdiv(n, tm),),
        in_specs=[pl.BlockSpec((tm, d), lambda i: (i, 0)),
                  pl.BlockSpec((1, d), lambda i: (0, 0), pipeline_mode=pl.Buffered(1)),
                  pl.BlockSpec((d, m), lambda i: (0, 0), pipeline_mode=pl.Buffered(1))],
        out_specs=pl.BlockSpec((tm, m), lambda i: (i, 0)),
        out_shape=jax.ShapeDtypeStruct((n, m), BF16),
        compiler_params=_cp("arbitrary"),
        name="norm_matmul",
    )(x, g.reshape(1, d), w)


def _conv3(x, prev_row, next_row, w, row0, t_len):
    n = x.shape[0]
    li = lax.broadcasted_iota(jnp.int32, (n, 1), 0)
    gi = li + row0
    xm = pltpu.roll(x, 1, 0)
    xm = jnp.where(li == 0, prev_row, xm)
    xm = jnp.where(gi >= 1, xm, 0.0)
    xp = pltpu.roll(x, n - 1, 0)
    xp = jnp.where(li == n - 1, next_row, xp)
    xp = jnp.where(gi + 1 < t_len, xp, 0.0)
    return xm * w[0:1] + x * w[1:2] + xp * w[2:3]


def _last_row(halo_ref):
    return halo_ref[0].astype(F32)[HALO - 1:HALO]


def _first_row(halo_ref):
    return halo_ref[0].astype(F32)[0:1]


def _halo_specs(tr, cw, t_len, col_blk):
    nh = tr // HALO
    last = t_len // HALO - 1
    main = pl.BlockSpec((1, tr, cw), lambda b, i, j: (b, i, col_blk + j))
    prev = pl.BlockSpec((1, HALO, cw), lambda b, i, j: (b, jnp.maximum(i * nh - 1, 0), col_blk + j))
    nxt = pl.BlockSpec((1, HALO, cw), lambda b, i, j: (b, jnp.minimum((i + 1) * nh, last), col_blk + j))
    return main, prev, nxt


def _sc_kernel(bg_ref, cg_ref, xi_ref, cgp_ref, xip_ref, cgn_ref, xin_ref, w_ref, o_ref, *, tr, t_len):
    row0 = pl.program_id(1) * tr
    u = cg_ref[0].astype(F32) * xi_ref[0].astype(F32)
    prev = _last_row(cgp_ref) * _last_row(xip_ref)
    nxt = _first_row(cgn_ref) * _first_row(xin_ref)
    y = _conv3(u, prev, nxt, w_ref[...], row0, t_len)
    o_ref[0] = (bg_ref[0].astype(F32) * y).astype(o_ref.dtype)


def _sc_branch(proj, w, tr):
    b, t, _ = proj.shape
    cw = BRANCH_W
    c0 = COL_SC // cw
    bg, _, _ = _halo_specs(tr, cw, t, c0)
    cg, cgp, cgn = _halo_specs(tr, cw, t, c0 + 1)
    xi, xip, xin = _halo_specs(tr, cw, t, c0 + 2)
    return pl.pallas_call(
        functools.partial(_sc_kernel, tr=tr, t_len=t),
        grid=(b, pl.cdiv(t, tr), 1),
        in_specs=[bg, cg, xi, cgp, xip, cgn, xin, pl.BlockSpec((3, cw), lambda b_, i, j: (0, 0))],
        out_specs=pl.BlockSpec((1, tr, cw), lambda b_, i, j: (b_, i, 0)),
        out_shape=jax.ShapeDtypeStruct((b, t, cw), BF16),
        compiler_params=_cp("arbitrary", "arbitrary", "arbitrary"),
        name="sc_branch",
    )(proj, proj, proj, proj, proj, proj, proj, w)


def _fnet_kernel(a_ref, u_ref, cs_ref, o_ref, *, tmf):
    p = jnp.dot(a_ref[0], u_ref[0], preferred_element_type=F32)
    pc = p[:tmf].astype(BF16)
    ps = p[tmf:].astype(BF16)
    y = jnp.dot(pc, cs_ref[0], preferred_element_type=F32) - jnp.dot(ps, cs_ref[1], preferred_element_type=F32)
    o_ref[0] = y.astype(o_ref.dtype)


def _fnet_tables(t, tmf):
    nm = -(-t // tmf)
    j = jnp.arange(nm * tmf, dtype=jnp.int32)[:, None]
    k = jnp.arange(t, dtype=jnp.int32)[None, :]
    ang = ((j * k) % t).astype(F32) * (2.0 * math.pi / t)
    valid = j < t
    c = jnp.where(valid, jnp.cos(ang), 0.0).astype(BF16).reshape(nm, tmf, t)
    s = jnp.where(valid, jnp.sin(ang), 0.0).astype(BF16).reshape(nm, tmf, t)
    a = jnp.concatenate([c, s], axis=1)
    jj = jnp.arange(BRANCH_W, dtype=jnp.int32)[:, None]
    kk = jnp.arange(BRANCH_W, dtype=jnp.int32)[None, :]
    same = (jj // FNET_GW) == (kk // FNET_GW)
    ang2 = (((jj % FNET_GW) * (kk % FNET_GW)) % FNET_GW).astype(F32) * (2.0 * math.pi / FNET_GW)
    scale = 1.0 / math.sqrt(t * FNET_GW)
    cc = jnp.where(same, jnp.cos(ang2), 0.0) * scale
    sc = jnp.where(same, jnp.sin(ang2), 0.0) * scale
    return a, jnp.stack([cc, sc]).astype(BF16)


def _fnet_branch(proj, tmf):
    b, t, _ = proj.shape
    a, cs = _fnet_tables(t, tmf)
    nm = a.shape[0]
    return pl.pallas_call(
        functools.partial(_fnet_kernel, tmf=tmf),
        grid=(nm, b),
        in_specs=[pl.BlockSpec((1, 2 * tmf, t), lambda i, j: (i, 0, 0)),
                  pl.BlockSpec((1, t, BRANCH_W), lambda i, j: (j, 0, COL_FN // BRANCH_W)),
                  pl.BlockSpec((2, BRANCH_W, BRANCH_W), lambda i, j: (0, 0, 0))],
        out_specs=pl.BlockSpec((1, tmf, BRANCH_W), lambda i, j: (j, i, 0)),
        out_shape=jax.ShapeDtypeStruct((b, t, BRANCH_W), BF16),
        compiler_params=_cp("arbitrary", "arbitrary"),
        name="fnet",
    )(a, proj, cs)


def _hyena_filter_kernel(w1t_ref, w1c_ref, w1s_ref, b1_ref, w2_ref, b2_ref, fq_ref, w3f_ref, w3b_ref, dl_ref,
                         o_ref, hid_ref, *, t):
    @pl.when(pl.program_id(0) == 0)
    def _():
        n = lax.broadcasted_iota(jnp.int32, (t, 1), 0).astype(F32)
        tt = n * (1.0 / (t - 1))
        wv = n * (2.0 * math.pi / t)
        bands = (HYENA_EMB - 1) // 2
        fi = lax.broadcasted_iota(jnp.int32, (1, bands), 1).astype(F32)
        fr = 1e-4 + fi * ((bands - 1 - 1e-4) / (bands - 1))
        arg = wv * fr
        pre = (tt * w1t_ref[...]
               + jnp.dot(jnp.cos(arg), w1c_ref[...], precision=HI, preferred_element_type=F32)
               - jnp.dot(jnp.sin(arg), w1s_ref[...], precision=HI, preferred_element_type=F32)
               + b1_ref[...])
        fq = fq_ref[...]
        h1 = jnp.sin(fq * pre)
        h2 = jnp.sin(fq * (jnp.dot(h1, w2_ref[...], precision=HI, preferred_element_type=F32) + b2_ref[...]))
        hid_ref[...] = h2

    n = lax.broadcasted_iota(jnp.int32, (t, 1), 0)
    tt = n.astype(F32) * (1.0 / (t - 1))
    dec = jnp.exp(-tt * dl_ref[...])
    hid = hid_ref[...]
    hf = jnp.dot(hid, w3f_ref[...], precision=HI, preferred_element_type=F32) * dec
    hb = jnp.dot(hid, w3b_ref[...], precision=HI, preferred_element_type=F32) * dec
    hb = jnp.where(n >= 1, hb, 0.0)
    l1 = jnp.sum(jnp.abs(hf), axis=0, keepdims=True) + jnp.sum(jnp.abs(hb), axis=0, keepdims=True)
    inv = 1.0 / l1
    o_ref[0, 0] = (hf * inv).astype(o_ref.dtype)
    o_ref[0, 1] = (hb * inv).astype(o_ref.dtype)


def _hyena_filters(t, w1, b1, w2, b2, w3, freq):
    cw = 128
    nb = BRANCH_W // cw
    bands = (HYENA_EMB - 1) // 2
    nf = w2.shape[0]
    max_decay = math.log(HYENA_TARGET) / HYENA_FAST
    min_decay = math.log(HYENA_TARGET) / HYENA_SLOW
    deltas = jnp.abs(jnp.linspace(min_decay, max_decay, BRANCH_W, dtype=F32)).reshape(1, BRANCH_W)
    full = lambda shape: pl.BlockSpec(shape, lambda g: (0,) * len(shape))
    return pl.pallas_call(
        functools.partial(_hyena_filter_kernel, t=t),
        grid=(HYENA_ORDER * nb,),
        in_specs=[full((1, nf)), full((bands, nf)), full((bands, nf)), full((1, nf)), full((nf, nf)),
                  full((1, nf)), full((1, nf)),
                  pl.BlockSpec((nf, cw), lambda g: (0, (g // nb) * 2 * nb + g % nb)),
                  pl.BlockSpec((nf, cw), lambda g: (0, (g // nb) * 2 * nb + nb + g % nb)),
                  pl.BlockSpec((1, cw), lambda g: (0, g % nb))],
        out_specs=pl.BlockSpec((1, 2, t, cw), lambda g: (g // nb, 0, 0, g % nb)),
        out_shape=jax.ShapeDtypeStruct((HYENA_ORDER, 2, t, BRANCH_W), BF16),
        scratch_shapes=[pltpu.VMEM((t, nf), F32)],
        compiler_params=_cp("arbitrary"),
        name="hyena_filter",
    )(w1[0:1], w1[1:1 + bands], w1[1 + bands:], b1.reshape(1, nf), w2, b2.reshape(1, nf), freq.reshape(1, nf),
      w3, w3, deltas)


def _hyena_tables(t, n_fft, tf):
    nfb = (n_fft // 2) // tf
    f = jnp.arange(n_fft // 2, dtype=jnp.int32)[:, None]
    n = jnp.arange(t, dtype=jnp.int32)[None, :]
    ang = (((2 * f + 1) * n) % (2 * n_fft)).astype(F32) * (math.pi / n_fft)
    c = jnp.cos(ang).astype(BF16).reshape(nfb, tf, t)
    s = (-jnp.sin(ang)).astype(BF16).reshape(nfb, tf, t)
    fwd = jnp.concatenate([c, s], axis=1)
    inv = fwd.reshape(n_fft, t).T
    return fwd, inv


def _hy_spec_kernel(a_ref, x_ref, o_ref):
    o_ref[0, 0] = jnp.dot(a_ref[0], x_ref[0], preferred_element_type=F32)


def _hy_filter_spectrum(fwd, filt):
    nfb, tf2, t = fwd.shape
    nb = filt.shape[0]
    return pl.pallas_call(
        _hy_spec_kernel,
        grid=(nfb, nb),
        in_specs=[pl.BlockSpec((1, tf2, t), lambda i, j: (i, 0, 0)),
                  pl.BlockSpec((1, t, BRANCH_W), lambda i, j: (j, 0, 0))],
        out_specs=pl.BlockSpec((1, 1, tf2, BRANCH_W), lambda i, j: (j, i, 0, 0)),
        out_shape=jax.ShapeDtypeStruct((nb, nfb, tf2, BRANCH_W), F32),
        compiler_params=_cp("arbitrary", "arbitrary"),
        name="hyena_filter_spectrum",
    )(fwd, filt)


def _hy_fwd_kernel(a_ref, x_ref, kf_ref, kb_ref, o_ref, *, tf, scale):
    p = jnp.dot(a_ref[0], x_ref[0], preferred_element_type=F32)
    ur, ui = p[:tf], p[tf:]
    kr = (kf_ref[0, 0, :tf] + kb_ref[0, 0, :tf]) * scale
    ki = (kf_ref[0, 0, tf:] - kb_ref[0, 0, tf:]) * scale
    o_ref[0, 0, :tf] = (ur * kr - ui * ki).astype(o_ref.dtype)
    o_ref[0, 0, tf:] = (ur * ki + ui * kr).astype(o_ref.dtype)


def _hy_forward(fwd, x, x_col, spec, order, n_fft):
    nfb, tf2, t = fwd.shape
    b = x.shape[0]
    return pl.pallas_call(
        functools.partial(_hy_fwd_kernel, tf=tf2 // 2, scale=2.0 / n_fft),
        grid=(nfb, b),
        in_specs=[pl.BlockSpec((1, tf2, t), lambda i, j: (i, 0, 0)),
                  pl.BlockSpec((1, t, BRANCH_W), lambda i, j: (j, 0, x_col)),
                  pl.BlockSpec((1, 1, tf2, BRANCH_W), lambda i, j: (2 * order, i, 0, 0)),
                  pl.BlockSpec((1, 1, tf2, BRANCH_W), lambda i, j: (2 * order + 1, i, 0, 0))],
        out_specs=pl.BlockSpec((1, 1, tf2, BRANCH_W), lambda i, j: (j, i, 0, 0)),
        out_shape=jax.ShapeDtypeStruct((b, nfb, tf2, BRANCH_W), BF16),
        compiler_params=_cp("arbitrary", "arbitrary"),
        name="hyena_fwd",
    )(fwd, x, spec, spec)


def _hy_inv_kernel(a_ref, y_ref, g_ref, v_ref, bias_ref, o_ref):
    conv = jnp.dot(a_ref[...], y_ref[0], preferred_element_type=F32)
    v = v_ref[0].astype(F32)
    o_ref[0] = (g_ref[0].astype(F32) * (conv + v * bias_ref[0])).astype(o_ref.dtype)


def _hy_inverse(inv, yhat, gate, gate_col, v, v_col, bias, order, tt):
    t, n_fft = inv.shape
    b = yhat.shape[0]
    yh = yhat.reshape(b, n_fft, BRANCH_W)
    return pl.pallas_call(
        _hy_inv_kernel,
        grid=(pl.cdiv(t, tt), b),
        in_specs=[pl.BlockSpec((tt, n_fft), lambda i, j: (i, 0)),
                  pl.BlockSpec((1, n_fft, BRANCH_W), lambda i, j: (j, 0, 0)),
                  pl.BlockSpec((1, tt, BRANCH_W), lambda i, j: (j, i, gate_col)),
                  pl.BlockSpec((1, tt, BRANCH_W), lambda i, j: (j, i, v_col)),
                  pl.BlockSpec((1, 1, BRANCH_W), lambda i, j: (order, 0, 0))],
        out_specs=pl.BlockSpec((1, tt, BRANCH_W), lambda i, j: (j, i, 0)),
        out_shape=jax.ShapeDtypeStruct((b, t, BRANCH_W), BF16),
        compiler_params=_cp("arbitrary", "arbitrary"),
        name="hyena_inv",
    )(inv, yh, gate, v, bias.reshape(HYENA_ORDER, 1, BRANCH_W))


def _softplus(x):
    return jnp.maximum(x, 0.0) + jnp.log(1.0 + jnp.exp(-jnp.abs(x)))


def _ssd_kernel(xs_ref, b_ref, c_ref, z_ref, dt_ref, dtb_ref, an_ref, dsk_ref, nw_ref, tri_ref, exp_ref, o_ref,
                yf_scr, yb_scr, *, t_len, nc, unroll):
    q = SSM_CHUNK
    gw = SSM_HPG * SSM_HEAD_DIM
    front = q - N_META
    ri = lax.broadcasted_iota(jnp.int32, (q, q), 0)
    ci = lax.broadcasted_iota(jnp.int32, (q, q), 1)
    tri = (ri >= ci, ri <= ci)
    head_of_lane = lax.broadcasted_iota(jnp.int32, (1, gw), 1) // SSM_HEAD_DIM
    lane = lax.broadcasted_iota(jnp.int32, (1, q), 1)
    dt_lane_mask = (lane < 2 * SSM_HPG).astype(F32)
    low_half = lane < SSM_HEAD_DIM
    dtb = dtb_ref[...]
    an = an_ref[...]
    masked_out = -1e30

    def widen(cols):
        return jnp.concatenate([jnp.where(low_half, cols[0], cols[1]), jnp.where(low_half, cols[2], cols[3])], axis=1)

    def cumsum(d, x):
        hi = x.astype(BF16)
        r1 = x - hi.astype(F32)
        mid = r1.astype(BF16)
        lo = (r1 - mid.astype(F32)).astype(BF16)
        s = jnp.dot(tri_ref[d], jnp.concatenate([hi, mid, lo], axis=1), preferred_element_type=F32)
        return s[:, :q] + s[:, q:2 * q] + s[:, 2 * q:]

    def load(ref, r0, nrows):
        return ref[0, pl.ds(r0, nrows), :]

    def dt_of(raw):
        return _softplus(raw.astype(F32) + dtb) * dt_lane_mask

    def chunk_data(c):
        r0 = pl.multiple_of(c * q - front, HALO)
        return load(xs_ref, r0, q), load(b_ref, r0, q), load(c_ref, r0, q), dt_of(load(dt_ref, r0, q))

    def chunk0_data():
        def pad(x):
            return jnp.concatenate([jnp.zeros((front, x.shape[1]), x.dtype), x], axis=0)
        return (pad(load(xs_ref, 0, N_META)), pad(load(b_ref, 0, N_META)), pad(load(c_ref, 0, N_META)),
                pad(dt_of(load(dt_ref, 0, N_META))))

    def stage_local(job, cs_n):
        (xs, bm, cm, dt), d = job
        lanes = [d * SSM_HPG + r for r in range(SSM_HPG)]
        cs_cols = [jnp.broadcast_to(cs_n[:, ln:ln + 1], (q, q)) for ln in lanes]
        cs_w = widen(cs_cols)
        dt_w = jnp.dot(dt.astype(BF16), exp_ref[d], preferred_element_type=F32).astype(BF16)
        tot = cs_w[q - 1:q, :] if d == 0 else cs_w[0:1, :]
        xt = xs * dt_w
        xd = xt * jnp.exp2(tot - cs_w).astype(BF16)
        s_new = lax.dot_general(bm, xd, (((0,), (0,)), ((), ())), preferred_element_type=F32)
        cb = lax.dot_general(cm, bm, (((1,), (1,)), ((), ())), preferred_element_type=F32)
        return cs_cols, xt, cb, cm, s_new, jnp.exp2(cs_w), jnp.exp2(tot), cs_n.T

    def stage_carry(loc, h):
        _, _, _, cm, s_new, ecs, etot, _ = loc
        y_off = jnp.dot(cm, h.astype(BF16), preferred_element_type=F32) * ecs
        return y_off, h * etot + s_new

    def stage_diag(job, cs_n, loc, y_off):
        d = job[1]
        cs_cols, xt, cb = loc[:3]
        cs_t = loc[7]
        ms, xm = [], []
        for r in range(SSM_HPG):
            ln = d * SSM_HPG + r
            dec = jnp.exp2(jnp.where(tri[d], cs_cols[r] - cs_t[ln:ln + 1, :], masked_out))
            ms.append((cb * dec).astype(BF16))
            xm.append(jnp.where(head_of_lane == r, xt, jnp.zeros_like(xt)))
        return y_off + jnp.dot(jnp.concatenate(ms, axis=1), jnp.concatenate(xm, axis=0), preferred_element_type=F32)

    def process_all(fwd_data, bwd_data, hf, hb):
        jobs = [(x, 0) for x in fwd_data] + [(x, 1) for x in bwd_data]
        cs = [cumsum(d, data[3] * an) for data, d in jobs]
        ys, pending = [], None
        for k, (job, c) in enumerate(zip(jobs, cs)):
            loc = stage_local(job, c)
            if pending is not None:
                ys.append(stage_diag(*pending))
            y_off, h = stage_carry(loc, hf if job[1] == 0 else hb)
            hf, hb = (h, hb) if job[1] == 0 else (hf, h)
            pending = (job, c, loc, y_off)
        ys.append(stage_diag(*pending))
        return ys[:len(fwd_data)], ys[len(fwd_data):], hf, hb

    def finish(y, r_out, nrows):
        y = y + load(xs_ref, r_out, nrows).astype(F32) * dsk_ref[...]
        y = y * _silu(load(z_ref, r_out, nrows).astype(F32))
        y = y * lax.rsqrt(jnp.mean(y * y, axis=-1, keepdims=True) + NORM_EPS)
        o_ref[0, pl.ds(r_out, nrows), :] = (y * nw_ref[...]).astype(o_ref.dtype)

    def rows(c):
        return pl.ds(pl.multiple_of(c * q, q), q)

    h0 = jnp.zeros((SSM_STATE, gw), F32)
    (y,), _, hf, _ = process_all([chunk0_data()], [], h0, h0)
    yf_scr[0:q, :] = y

    def scan_body(i, carry):
        hf, hb = carry
        cf = [1 + unroll * i + k for k in range(unroll)]
        cb_ = [nc - c for c in cf]
        yf, yb, hf, hb = process_all([chunk_data(c) for c in cf], [chunk_data(c) for c in cb_], hf, hb)
        for c, y in zip(cf, yf):
            yf_scr[rows(c), :] = y
        for c, y in zip(cb_, yb):
            yb_scr[rows(c), :] = y
        return hf, hb

    _, hb = lax.fori_loop(0, (nc - 1) // unroll, scan_body, (hf, h0))
    _, (y,), _, _ = process_all([], [chunk0_data()], h0, hb)
    yb_scr[0:q, :] = y

    finish(yf_scr[front:q, :] + yb_scr[front:q, :], 0, N_META)

    def finish_body(c, carry):
        finish(yf_scr[rows(c), :] + yb_scr[rows(c), :], pl.multiple_of(c * q - front, HALO), q)
        return carry

    lax.fori_loop(1, nc, finish_body, 0)


def _ssd_branch(xbc, proj, dt_bias, a_log, d_skip, norm_w):
    b, t, _ = xbc.shape
    q = SSM_CHUNK
    gw = SSM_HPG * SSM_HEAD_DIM
    nc = (q - N_META + t) // q
    assert nc * q == q - N_META + t, "sequence length minus meta tokens must be a multiple of the SSD chunk"
    a = -jnp.exp(a_log.astype(F32)).reshape(2, SSM_GROUPS, SSM_HPG)
    dtb = dt_bias.astype(F32).reshape(2, SSM_GROUPS, SSM_HPG)
    pad = lambda v: jnp.pad(jnp.transpose(v, (1, 0, 2)).reshape(SSM_GROUPS, 1, 2 * SSM_HPG),
                            ((0, 0), (0, 0), (0, q - 2 * SSM_HPG)))
    a_n = pad(a * math.log2(math.e))
    dtb_n = pad(dtb)
    ri = jnp.arange(q, dtype=jnp.int32)[:, None]
    ci = jnp.arange(q, dtype=jnp.int32)[None, :]
    tri = jnp.stack([ri >= ci, ri <= ci]).astype(BF16)
    cj = jnp.arange(gw, dtype=jnp.int32)[None, :] // SSM_HEAD_DIM
    expand = jnp.stack([ri == cj, ri == cj + SSM_HPG]).astype(BF16)
    dsk = jnp.repeat(d_skip.astype(F32).reshape(SSM_GROUPS, 1, SSM_HPG), SSM_HEAD_DIM, axis=-1)
    nw = norm_w.astype(F32).reshape(SSM_GROUPS, 1, gw)
    par = lambda shape: pl.BlockSpec((None,) + shape, lambda i, g: (g, 0, 0))
    return pl.pallas_call(
        functools.partial(_ssd_kernel, t_len=t, nc=nc, unroll=math.gcd(nc - 1, SSD_UNROLL)),
        grid=(b, SSM_GROUPS),
        in_specs=[pl.BlockSpec((1, t, gw), lambda i, g: (i, 0, g)),
                  pl.BlockSpec((1, t, SSM_STATE), lambda i, g: (i, 0, 4 + g)),
                  pl.BlockSpec((1, t, SSM_STATE), lambda i, g: (i, 0, 6 + g)),
                  pl.BlockSpec((1, t, gw), lambda i, g: (i, 0, COL_Z // gw + g)),
                  pl.BlockSpec((1, t, q), lambda i, g: (i, 0, COL_DT // q + g)),
                  par((1, q)), par((1, q)), par((1, gw)), par((1, gw)),
                  pl.BlockSpec((2, q, q), lambda i, g: (0, 0, 0)),
                  pl.BlockSpec((2, q, gw), lambda i, g: (0, 0, 0))],
        out_specs=pl.BlockSpec((1, t, gw), lambda i, g: (i, 0, g)),
        out_shape=jax.ShapeDtypeStruct((b, t, BRANCH_W), BF16),
        scratch_shapes=[pltpu.VMEM((nc * q, gw), F32), pltpu.VMEM((nc * q, gw), F32)],
        compiler_params=_cp("arbitrary", "arbitrary"),
        name="ssd",
    )(xbc, xbc, xbc, proj, proj, dtb_n, a_n, dsk, nw, tri, expand)


def _merge_kernel(h_ref, y0_ref, y1_ref, y2_ref, y3_ref, g0_ref, g1_ref, g2_ref, g3_ref, wb_ref, wo_ref, o_ref):
    merged = None
    for k, (y_ref, g_ref) in enumerate(zip((y0_ref, y1_ref, y2_ref, y3_ref), (g0_ref, g1_ref, g2_ref, g3_ref))):
        gate = _sigmoid(g_ref[...].astype(F32))
        term = gate * jnp.dot(y_ref[...], wb_ref[k], preferred_element_type=F32)
        merged = term if merged is None else merged + term
    o_ref[...] = h_ref[...] + jnp.dot(merged.astype(BF16), wo_ref[...], preferred_element_type=F32)


def _merge(h, ys, proj, wb, wo, tm):
    n, d = h.shape
    nb = len(ys)
    row = lambda w: pl.BlockSpec((tm, w), lambda i: (i, 0))
    return pl.pallas_call(
        _merge_kernel,
        grid=(pl.cdiv(n, tm),),
        in_specs=[row(d)] + [row(BRANCH_W)] * nb
                 + [pl.BlockSpec((tm, d), functools.partial(lambda i, k: (i, COL_GATE // d + k), k=k)) for k in range(nb)]
                 + [pl.BlockSpec((nb, BRANCH_W, d), lambda i: (0, 0, 0)),
                    pl.BlockSpec((d, d), lambda i: (0, 0))],
        out_specs=row(d),
        out_shape=jax.ShapeDtypeStruct((n, d), F32),
        compiler_params=_cp("arbitrary"),
        name="merge",
    )(h, *ys, *([proj] * nb), wb, wo)


def _rms_rows(x, g):
    return x * lax.rsqrt(jnp.mean(x * x, axis=-1, keepdims=True) + NORM_EPS) * g


def _stage_normed_rows(xs_ref, x_ref, xp_ref, xn_ref, g, row0, tr, t_len):
    li = lax.broadcasted_iota(jnp.int32, (tr, 1), 0)
    xs_ref[HALO:HALO + tr, :] = jnp.where(row0 + li < t_len, _rms_rows(x_ref[0], g), 0.0).astype(BF16)
    xs_ref[0:HALO, :] = jnp.where(row0 > 0, _rms_rows(xp_ref[0], g), 0.0).astype(BF16)
    xs_ref[HALO + tr:, :] = jnp.where(row0 + tr < t_len, _rms_rows(xn_ref[0], g), 0.0).astype(BF16)


def _conv3_rows(u, w):
    n = u.shape[0]
    return pltpu.roll(u, 1, 0) * w[0:1] + u * w[1:2] + pltpu.roll(u, n - 1, 0) * w[2:3]


def _row_halo_specs(tr, d, t_len):
    nh = tr // HALO
    last = t_len // HALO - 1
    return (pl.BlockSpec((1, tr, d), lambda b, i: (b, i, 0)),
            pl.BlockSpec((1, HALO, d), lambda b, i: (b, jnp.maximum(i * nh - 1, 0), 0)),
            pl.BlockSpec((1, HALO, d), lambda b, i: (b, jnp.minimum((i + 1) * nh, last), 0)))


def _resident(shape):
    return pl.BlockSpec(shape, lambda b, i: (0,) * len(shape), pipeline_mode=pl.Buffered(1))


def _col_chunks(n):
    return tuple((s0, min(MXU_COLS, n - s0)) for s0 in range(0, n, MXU_COLS))


def _conv_proj_kernel(x_ref, xp_ref, xn_ref, g_ref, w_ref, c_ref, b_ref, o_ref, xs_ref, *, tr, t_len, n_act):
    _stage_normed_rows(xs_ref, x_ref, xp_ref, xn_ref, g_ref[...], pl.program_id(1) * tr, tr, t_len)
    xs = xs_ref[...]
    for s0, w in _col_chunks(w_ref.shape[1]):
        u = jnp.dot(xs, w_ref[:, s0:s0 + w], preferred_element_type=F32)
        y = _conv3_rows(u, c_ref[:, s0:s0 + w])[HALO:HALO + tr] + b_ref[:, s0:s0 + w]
        o_ref[0, :, s0:s0 + w] = (_silu(y) if s0 < n_act else y).astype(o_ref.dtype)


def _conv_proj(h, g, w, taps, bias, n_act, tr):
    b, t, d = h.shape
    m = w.shape[1]
    main, prev, nxt = _row_halo_specs(tr, d, t)
    return pl.pallas_call(
        functools.partial(_conv_proj_kernel, tr=tr, t_len=t, n_act=n_act),
        grid=(b, pl.cdiv(t, tr)),
        in_specs=[main, prev, nxt, _resident((1, d)), _resident((d, m)), _resident((3, m)), _resident((1, m))],
        out_specs=pl.BlockSpec((1, tr, m), lambda b_, i: (b_, i, 0)),
        out_shape=jax.ShapeDtypeStruct((b, t, m), BF16),
        scratch_shapes=[pltpu.VMEM((tr + 2 * HALO, d), BF16)],
        compiler_params=_cp("arbitrary", "arbitrary"),
        name="conv_proj",
    )(h, h, h, g.reshape(1, d), w, taps, bias)


def _ffn_kernel(x_ref, xp_ref, xn_ref, g_ref, wu_ref, cw_ref, wd_ref, o_ref, xs_ref, gate_ref, *, tr, t_len):
    _stage_normed_rows(xs_ref, x_ref, xp_ref, xn_ref, g_ref[...], pl.program_id(1) * tr, tr, t_len)
    xs = xs_ref[...]
    dff = wd_ref.shape[0]
    for s0, w in _col_chunks(dff):
        a = _conv3_rows(jnp.dot(xs, wu_ref[:, s0:s0 + w], preferred_element_type=F32), cw_ref[:, s0:s0 + w])
        v = _conv3_rows(jnp.dot(xs, wu_ref[:, dff + s0:dff + s0 + w], preferred_element_type=F32),
                        cw_ref[:, dff + s0:dff + s0 + w])
        gate_ref[:, s0:s0 + w] = (_silu(a) * v)[HALO:HALO + tr].astype(BF16)
    o_ref[0] = x_ref[0] + jnp.dot(gate_ref[...], wd_ref[...], preferred_element_type=F32)


def _ffn(h, g, w_up, conv_w, w_down, tr):
    b, t, d = h.shape
    dff = w_down.shape[0]
    main, prev, nxt = _row_halo_specs(tr, d, t)
    return pl.pallas_call(
        functools.partial(_ffn_kernel, tr=tr, t_len=t),
        grid=(b, pl.cdiv(t, tr)),
        in_specs=[main, prev, nxt, _resident((1, d)), _resident((d, 2 * dff)), _resident((3, 2 * dff)),
                  _resident((dff, d))],
        out_specs=pl.BlockSpec((1, tr, d), lambda b_, i: (b_, i, 0)),
        out_shape=jax.ShapeDtypeStruct((b, t, d), F32),
        scratch_shapes=[pltpu.VMEM((tr + 2 * HALO, d), BF16), pltpu.VMEM((tr, dff), BF16)],
        compiler_params=_cp("arbitrary", "arbitrary"),
        name="ffn",
    )(h, h, h, g.reshape(1, d), w_up, conv_w, w_down)


def _final_norm_kernel(x_ref, g_ref, o_ref):
    x = x_ref[...]
    ms = jnp.mean(x * x, axis=-1, keepdims=True)
    o_ref[...] = x * lax.rsqrt(ms + NORM_EPS) * g_ref[...]


def _final_norm(x, g, tm):
    n, d = x.shape
    return pl.pallas_call(
        _final_norm_kernel,
        grid=(pl.cdiv(n, tm),),
        in_specs=[pl.BlockSpec((tm, d), lambda i: (i, 0)), pl.BlockSpec((1, d), lambda i: (0, 0))],
        out_specs=pl.BlockSpec((tm, d), lambda i: (i, 0)),
        out_shape=jax.ShapeDtypeStruct((n, d), F32),
        compiler_params=_cp("arbitrary"),
        name="final_norm",
    )(x, g.reshape(1, d))


def _prep_w_in(w_in):
    d = w_in.shape[0]
    o_z, o_xbc, o_dt = BRANCH_W, 2 * BRANCH_W, 2 * BRANCH_W + 1024
    ndt = 2 * SSM_GROUPS * SSM_HPG
    o_hy = o_dt + ndt
    o_sc = o_hy + 3 * BRANCH_W
    o_gate = o_sc + 3 * BRANCH_W
    w_dt = w_in[:, o_dt:o_hy].reshape(d, 2, SSM_GROUPS, SSM_HPG)
    w_dt = jnp.transpose(w_dt, (0, 2, 1, 3)).reshape(d, SSM_GROUPS, 2 * SSM_HPG)
    w_dt = jnp.pad(w_dt, ((0, 0), (0, 0), (0, SSM_CHUNK - 2 * SSM_HPG))).reshape(d, SSM_GROUPS * SSM_CHUNK)
    w_conv = jnp.concatenate([w_in[:, o_xbc:o_dt], w_in[:, o_hy:o_sc]], axis=1).astype(BF16)
    w_plain = jnp.concatenate([w_in[:, :o_xbc], w_in[:, o_gate:], w_in[:, o_sc:o_gate], w_dt], axis=1).astype(BF16)
    return w_conv, w_plain


def _hyena_fft_len(t):
    tf = 528
    nfb = -(-(2 * t - 1) // (2 * tf))
    return 2 * tf * nfb, tf


def _run_trunk(x, meta_tokens, norm_final, layers):
    b, seq, d = x.shape
    t = seq + N_META
    meta = jnp.broadcast_to(meta_tokens[None].astype(x.dtype), (b, N_META, d))
    h = jnp.concatenate([meta, x], axis=1).reshape(b * t, d)
    tr = _pick_tile(t, ROW_TILE)
    n_fft, tf = _hyena_fft_len(t)
    hy_fwd, hy_inv = _hyena_tables(t, n_fft, tf)
    tmf = tti = tr
    tok = min(TOKEN_TILE, b * t)
    hv, hx1, hx2 = (CV_HY // BRANCH_W + k for k in range(3))
    for p in layers:
        w_conv, w_plain = p["w_in"]
        taps = jnp.concatenate([p["ssm_conv_w"], p["hyena_conv_w"]], axis=1)
        bias = jnp.concatenate([p["ssm_conv_b"], jnp.zeros((3 * BRANCH_W,), F32)]).reshape(1, N_CONV)
        projc = _conv_proj(h.reshape(b, t, d), p["norm_mix"], w_conv, taps, bias, CV_HY, tr)
        proj = _norm_matmul(h, p["norm_mix"], w_plain, min(512, tok), 768).reshape(b, t, N_PLAIN)
        y_fn = _fnet_branch(proj, tmf)
        y_ssm = _ssd_branch(projc, proj, p["ssm_dt_bias"], p["ssm_a_log"], p["ssm_d"], p["ssm_norm"])
        filt = _hyena_filters(t, p["hyena_w1"], p["hyena_b1"], p["hyena_w2"], p["hyena_b2"], p["hyena_w3"],
                              p["hyena_freq"]).reshape(2 * HYENA_ORDER, t, BRANCH_W)
        spec = _hy_filter_spectrum(hy_fwd, filt)
        z = _hy_inverse(hy_inv, _hy_forward(hy_fwd, projc, hv, spec, 0, n_fft), projc, hx1, projc, hv,
                        p["hyena_bias"], 0, tti)
        y_hy = _hy_inverse(hy_inv, _hy_forward(hy_fwd, z, 0, spec, 1, n_fft), projc, hx2, z, 0, p["hyena_bias"], 1, tti)
        y_sc = _sc_branch(proj, p["sc_conv_w"], tr)
        flat = lambda a: a.reshape(b * t, a.shape[-1])
        h = _merge(h, [flat(y_fn), flat(y_ssm), flat(y_hy), flat(y_sc)], flat(proj), p["w_branch"], p["w_out"], min(512, tok))
        h = _ffn(h.reshape(b, t, d), p["norm_ffn"], p["w_up"], p["ffn_conv_w"], p["w_down"], tr).reshape(b * t, d)
    out = _final_norm(h, norm_final, tok).reshape(b, t, d)
    return out[:, N_META:]


def kernel(x_prompt, x_sample, meta_tokens, norm_mix, w_in, ssm_conv_w, ssm_conv_b, ssm_dt_bias, ssm_a_log, ssm_d,
           ssm_norm, hyena_conv_w, hyena_w1, hyena_b1, hyena_w2, hyena_b2, hyena_w3, hyena_freq, hyena_bias,
           sc_conv_w, w_branch, w_out, norm_ffn, ffn_conv_w, w_up, w_down, norm_final):
    depth = w_in.shape[0]
    layers = []
    for l in range(depth):
        layers.append(dict(
            norm_mix=norm_mix[l], w_in=_prep_w_in(w_in[l]), ssm_conv_w=ssm_conv_w[l], ssm_conv_b=ssm_conv_b[l],
            ssm_dt_bias=ssm_dt_bias[l], ssm_a_log=ssm_a_log[l], ssm_d=ssm_d[l], ssm_norm=ssm_norm[l],
            hyena_conv_w=hyena_conv_w[l], hyena_w1=hyena_w1[l], hyena_b1=hyena_b1[l], hyena_w2=hyena_w2[l],
            hyena_b2=hyena_b2[l], hyena_w3=hyena_w3[l], hyena_freq=hyena_freq[l], hyena_bias=hyena_bias[l],
            sc_conv_w=sc_conv_w[l], w_branch=w_branch[l].astype(BF16), w_out=w_out[l].astype(BF16),
            norm_ffn=norm_ffn[l], ffn_conv_w=ffn_conv_w[l], w_up=w_up[l].astype(BF16), w_down=w_down[l].astype(BF16)))
    y_prompt = _run_trunk(x_prompt, meta_tokens, norm_final, layers)
    y_sample = _run_trunk(x_sample, meta_tokens, norm_final, layers)
    return (y_prompt, y_sample)
```

```python
import functools
import math

import jax
import jax.numpy as jnp
from jax import lax
from jax.experimental import pallas as pl
from jax.experimental.pallas import tpu as pltpu

F32 = jnp.float32
BF16 = jnp.bfloat16
HI = lax.Precision.HIGHEST

NORM_EPS = 1e-6
N_META = 16
BRANCH_W = 512
FNET_GW = 128
SSM_GROUPS = 2
SSM_HPG = 4
SSM_HEAD_DIM = 64
SSM_STATE = 128
SSM_CHUNK = 128
HYENA_ORDER = 2
HYENA_EMB = 33
HYENA_TARGET = 1e-2
HYENA_FAST = 0.3
HYENA_SLOW = 1.5

ROW_TILE = 700
TOKEN_TILE = 1024
HYENA_BLOCK = 704
HYENA_FREQ_CHUNK = 32
MXU_COLS = 256
SSD_UNROLL = 4
HALO = 16
VMEM_LIMIT = 56 * 1024 * 1024

CV_XBC = 0
CV_HY = 1024
N_CONV = 2560
COL_FN = 0
COL_Z = 512
COL_GATE = 1024
COL_SC = 5120
COL_DT = 6656
N_PLAIN = 6912


def _cp(*sem):
    return pltpu.CompilerParams(dimension_semantics=sem, vmem_limit_bytes=VMEM_LIMIT)


def _pick_tile(n, target, mult=16):
    k = max(1, -(-n // target))
    t = -(-n // k)
    return -(-t // mult) * mult


def _silu(x):
    return x * (1.0 / (1.0 + jnp.exp(-x)))


def _sigmoid(x):
    return 1.0 / (1.0 + jnp.exp(-x))


def _norm_matmul_kernel(x_ref, g_ref, w_ref, o_ref, *, tn):
    x = x_ref[...]
    ms = jnp.mean(x * x, axis=-1, keepdims=True)
    xn = (x * lax.rsqrt(ms + NORM_EPS) * g_ref[...]).astype(BF16)
    for s0 in range(0, w_ref.shape[1], tn):
        o_ref[:, s0:s0 + tn] = jnp.dot(xn, w_ref[:, s0:s0 + tn], preferred_element_type=F32).astype(o_ref.dtype)


def _norm_matmul(x, g, w, tm, tn):
    n, d = x.shape
    m = w.shape[1]
    assert m % tn == 0
    tm = min(tm, -(-n // HALO) * HALO)
    return pl.pallas_call(
        functools.partial(_norm_matmul_kernel, tn=tn),
        grid=(pl.cdiv(n, tm),),
        in_specs=[pl.BlockSpec((tm, d), lambda i: (i, 0)),
                  pl.BlockSpec((1, d), lambda i: (0, 0), pipeline_mode=pl.Buffered(1)),
                  pl.BlockSpec((d, m), lambda i: (0, 0), pipeline_mode=pl.Buffered(1))],
        out_specs=pl.BlockSpec((tm, m), lambda i: (i, 0)),
        out_shape=jax.ShapeDtypeStruct((n, m), BF16),
        compiler_params=_cp("arbitrary"),
        name="norm_matmul",
    )(x, g.reshape(1, d), w)


def _conv3(x, prev_row, next_row, w, row0, t_len):
    n = x.shape[0]
    li = lax.broadcasted_iota(jnp.int32, (n, 1), 0)
    gi = li + row0
    xm = pltpu.roll(x, 1, 0)
    xm = jnp.where(li == 0, prev_row, xm)
    xm = jnp.where(gi >= 1, xm, 0.0)
    xp = pltpu.roll(x, n - 1, 0)
    xp = jnp.where(li == n - 1, next_row, xp)
    xp = jnp.where(gi + 1 < t_len, xp, 0.0)
    return xm * w[0:1] + x * w[1:2] + xp * w[2:3]


def _last_row(halo_ref):
    return halo_ref[0].astype(F32)[HALO - 1:HALO]


def _first_row(halo_ref):
    return halo_ref[0].astype(F32)[0:1]


def _halo_specs(tr, cw, t_len, col_blk):
    nh = tr // HALO
    last = t_len // HALO - 1
    main = pl.BlockSpec((1, tr, cw), lambda b, i, j: (b, i, col_blk + j))
    prev = pl.BlockSpec((1, HALO, cw), lambda b, i, j: (b, jnp.maximum(i * nh - 1, 0), col_blk + j))
    nxt = pl.BlockSpec((1, HALO, cw), lambda b, i, j: (b, jnp.minimum((i + 1) * nh, last), col_blk + j))
    return main, prev, nxt


def _sc_kernel(bg_ref, cg_ref, xi_ref, cgp_ref, xip_ref, cgn_ref, xin_ref, w_ref, o_ref, *, tr, t_len):
    row0 = pl.program_id(1) * tr
    u = cg_ref[0].astype(F32) * xi_ref[0].astype(F32)
    prev = _last_row(cgp_ref) * _last_row(xip_ref)
    nxt = _first_row(cgn_ref) * _first_row(xin_ref)
    y = _conv3(u, prev, nxt, w_ref[...], row0, t_len)
    o_ref[0] = (bg_ref[0].astype(F32) * y).astype(o_ref.dtype)


def _sc_branch(proj, w, tr):
    b, t, _ = proj.shape
    cw = BRANCH_W
    c0 = COL_SC // cw
    bg, _, _ = _halo_specs(tr, cw, t, c0)
    cg, cgp, cgn = _halo_specs(tr, cw, t, c0 + 1)
    xi, xip, xin = _halo_specs(tr, cw, t, c0 + 2)
    return pl.pallas_call(
        functools.partial(_sc_kernel, tr=tr, t_len=t),
        grid=(b, pl.cdiv(t, tr), 1),
        in_specs=[bg, cg, xi, cgp, xip, cgn, xin, pl.BlockSpec((3, cw), lambda b_, i, j: (0, 0))],
        out_specs=pl.BlockSpec((1, tr, cw), lambda b_, i, j: (b_, i, 0)),
        out_shape=jax.ShapeDtypeStruct((b, t, cw), BF16),
        compiler_params=_cp("arbitrary", "arbitrary", "arbitrary"),
        name="sc_branch",
    )(proj, proj, proj, proj, proj, proj, proj, w)


def _fnet_kernel(a_ref, u_ref, cs_ref, o_ref, *, tmf):
    p = jnp.dot(a_ref[0], u_ref[0], preferred_element_type=F32)
    pc = p[:tmf].astype(BF16)
    ps = p[tmf:].astype(BF16)
    y = jnp.dot(pc, cs_ref[0], preferred_element_type=F32) - jnp.dot(ps, cs_ref[1], preferred_element_type=F32)
    o_ref[0] = y.astype(o_ref.dtype)


def _fnet_tables(t, tmf):
    nm = -(-t // tmf)
    j = jnp.arange(nm * tmf, dtype=jnp.int32)[:, None]
    k = jnp.arange(t, dtype=jnp.int32)[None, :]
    ang = ((j * k) % t).astype(F32) * (2.0 * math.pi / t)
    valid = j < t
    c = jnp.where(valid, jnp.cos(ang), 0.0).astype(BF16).reshape(nm, tmf, t)
    s = jnp.where(valid, jnp.sin(ang), 0.0).astype(BF16).reshape(nm, tmf, t)
    a = jnp.concatenate([c, s], axis=1)
    jj = jnp.arange(BRANCH_W, dtype=jnp.int32)[:, None]
    kk = jnp.arange(BRANCH_W, dtype=jnp.int32)[None, :]
    same = (jj // FNET_GW) == (kk // FNET_GW)
    ang2 = (((jj % FNET_GW) * (kk % FNET_GW)) % FNET_GW).astype(F32) * (2.0 * math.pi / FNET_GW)
    scale = 1.0 / math.sqrt(t * FNET_GW)
    cc = jnp.where(same, jnp.cos(ang2), 0.0) * scale
    sc = jnp.where(same, jnp.sin(ang2), 0.0) * scale
    return a, jnp.stack([cc, sc]).astype(BF16)


def _fnet_branch(proj, tmf):
    b, t, _ = proj.shape
    a, cs = _fnet_tables(t, tmf)
    nm = a.shape[0]
    return pl.pallas_call(
        functools.partial(_fnet_kernel, tmf=tmf),
        grid=(nm, b),
        in_specs=[pl.BlockSpec((1, 2 * tmf, t), lambda i, j: (i, 0, 0)),
                  pl.BlockSpec((1, t, BRANCH_W), lambda i, j: (j, 0, COL_FN // BRANCH_W)),
                  pl.BlockSpec((2, BRANCH_W, BRANCH_W), lambda i, j: (0, 0, 0))],
        out_specs=pl.BlockSpec((1, tmf, BRANCH_W), lambda i, j: (j, i, 0)),
        out_shape=jax.ShapeDtypeStruct((b, t, BRANCH_W), BF16),
        compiler_params=_cp("arbitrary", "arbitrary"),
        name="fnet",
    )(a, proj, cs)


def _hyena_filter_kernel(w1t_ref, w1c_ref, w1s_ref, b1_ref, w2_ref, b2_ref, fq_ref, w3f_ref, w3b_ref, dl_ref,
                         o_ref, hid_ref, *, t):
    @pl.when(pl.program_id(0) == 0)
    def _():
        n = lax.broadcasted_iota(jnp.int32, (t, 1), 0).astype(F32)
        tt = n * (1.0 / (t - 1))
        wv = n * (2.0 * math.pi / t)
        bands = (HYENA_EMB - 1) // 2
        fi = lax.broadcasted_iota(jnp.int32, (1, bands), 1).astype(F32)
        fr = 1e-4 + fi * ((bands - 1 - 1e-4) / (bands - 1))
        arg = wv * fr
        pre = (tt * w1t_ref[...]
               + jnp.dot(jnp.cos(arg), w1c_ref[...], precision=HI, preferred_element_type=F32)
               - jnp.dot(jnp.sin(arg), w1s_ref[...], precision=HI, preferred_element_type=F32)
               + b1_ref[...])
        fq = fq_ref[...]
        h1 = jnp.sin(fq * pre)
        h2 = jnp.sin(fq * (jnp.dot(h1, w2_ref[...], precision=HI, preferred_element_type=F32) + b2_ref[...]))
        hid_ref[...] = h2

    n = lax.broadcasted_iota(jnp.int32, (t, 1), 0)
    tt = n.astype(F32) * (1.0 / (t - 1))
    dec = jnp.exp(-tt * dl_ref[...])
    hid = hid_ref[...]
    hf = jnp.dot(hid, w3f_ref[...], precision=HI, preferred_element_type=F32) * dec
    hb = jnp.dot(hid, w3b_ref[...], precision=HI, preferred_element_type=F32) * dec
    hb = jnp.where(n >= 1, hb, 0.0)
    l1 = jnp.sum(jnp.abs(hf), axis=0, keepdims=True) + jnp.sum(jnp.abs(hb), axis=0, keepdims=True)
    inv = 1.0 / l1
    o_ref[0, 0] = (hf * inv).astype(o_ref.dtype)
    o_ref[0, 1] = (hb * inv).astype(o_ref.dtype)


def _hyena_filters(t, w1, b1, w2, b2, w3, freq):
    cw = 128
    nb = BRANCH_W // cw
    bands = (HYENA_EMB - 1) // 2
    nf = w2.shape[0]
    max_decay = math.log(HYENA_TARGET) / HYENA_FAST
    min_decay = math.log(HYENA_TARGET) / HYENA_SLOW
    deltas = jnp.abs(jnp.linspace(min_decay, max_decay, BRANCH_W, dtype=F32)).reshape(1, BRANCH_W)
    full = lambda shape: pl.BlockSpec(shape, lambda g: (0,) * len(shape))
    return pl.pallas_call(
        functools.partial(_hyena_filter_kernel, t=t),
        grid=(HYENA_ORDER * nb,),
        in_specs=[full((1, nf)), full((bands, nf)), full((bands, nf)), full((1, nf)), full((nf, nf)),
                  full((1, nf)), full((1, nf)),
                  pl.BlockSpec((nf, cw), lambda g: (0, (g // nb) * 2 * nb + g % nb)),
                  pl.BlockSpec((nf, cw), lambda g: (0, (g // nb) * 2 * nb + nb + g % nb)),
                  pl.BlockSpec((1, cw), lambda g: (0, g % nb))],
        out_specs=pl.BlockSpec((1, 2, t, cw), lambda g: (g // nb, 0, 0, g % nb)),
        out_shape=jax.ShapeDtypeStruct((HYENA_ORDER, 2, t, BRANCH_W), BF16),
        scratch_shapes=[pltpu.VMEM((t, nf), F32)],
        compiler_params=_cp("arbitrary"),
        name="hyena_filter",
    )(w1[0:1], w1[1:1 + bands], w1[1 + bands:], b1.reshape(1, nf), w2, b2.reshape(1, nf), freq.reshape(1, nf),
      w3, w3, deltas)


def _hyena_tables(p_len):
    f = jnp.arange(p_len, dtype=jnp.int32)[:, None]
    n = jnp.arange(p_len, dtype=jnp.int32)[None, :]
    ang = (((2 * f + 1) * n) % (4 * p_len)).astype(F32) * (math.pi / (2 * p_len))
    a = jnp.concatenate([jnp.cos(ang), -jnp.sin(ang)], axis=0).astype(BF16)
    return a, a.T


def _hy_spec_kernel(a_ref, seg_ref, prev_ref, o_ref, *, p_len):
    g = jnp.dot(a_ref[...], seg_ref[0, 0], preferred_element_type=F32)
    gp = jnp.dot(a_ref[...], prev_ref[0, 0], preferred_element_type=F32)
    first = prev_ref[0, 0, 0:1, :].astype(F32)
    f = lax.broadcasted_iota(jnp.int32, (p_len, 1), 0)
    sign = jnp.where(f % 2 == 0, 1.0, -1.0)
    scale = 1.0 / p_len
    o_ref[0, 0, :p_len] = (g[:p_len] - sign * gp[p_len:]) * scale
    o_ref[0, 0, p_len:] = (g[p_len:] + sign * (gp[:p_len] - first)) * scale


def _hy_filter_spectrum(a, filt, p_len, nb):
    order, _, t, w = filt.shape
    half = nb * p_len
    neg = jnp.flip(filt[:, 1, 1:], axis=1)
    lags = jnp.concatenate([jnp.zeros((order, half - (t - 1), w), BF16), neg, filt[:, 0],
                            jnp.zeros((order, half - t, w), BF16)], axis=1).reshape(order, 2 * nb, p_len, w)
    return pl.pallas_call(
        functools.partial(_hy_spec_kernel, p_len=p_len),
        grid=(order, 2 * nb - 1),
        in_specs=[pl.BlockSpec((2 * p_len, p_len), lambda o, d: (0, 0)),
                  pl.BlockSpec((1, 1, p_len, w), lambda o, d: (o, d + 1, 0, 0)),
                  pl.BlockSpec((1, 1, p_len, w), lambda o, d: (o, d, 0, 0))],
        out_specs=pl.BlockSpec((1, 1, 2 * p_len, w), lambda o, d: (o, d, 0, 0)),
        out_shape=jax.ShapeDtypeStruct((order, 2 * nb - 1, 2 * p_len, w), F32),
        compiler_params=_cp("arbitrary", "arbitrary"),
        name="hyena_filter_spectrum",
    )(a, lags, lags)


def _hy_conv_kernel(a_ref, at_ref, k_ref, v_ref, g_ref, bias_ref, o_ref, u_scr, y_scr, *, t_len, p_len, nb, fc):
    cw = v_ref.shape[2]

    def block_rows(i):
        return i * p_len, min(p_len, t_len - i * p_len)

    for j in range(nb):
        r0, n = block_rows(j)
        u = v_ref[0, r0:r0 + n, :]
        if n < p_len:
            u = jnp.concatenate([u, jnp.zeros((p_len - n, cw), u.dtype)], axis=0)
        u_scr[j] = jnp.dot(a_ref[...], u, preferred_element_type=F32)

    for i in range(nb):
        for r in range(0, p_len, fc):
            acc_r = jnp.zeros((fc, cw), F32)
            acc_i = jnp.zeros((fc, cw), F32)
            for j in range(nb):
                d = i - j + nb - 1
                ur, ui = u_scr[j, r:r + fc, :], u_scr[j, p_len + r:p_len + r + fc, :]
                kr, ki = k_ref[0, d, r:r + fc, :], k_ref[0, d, p_len + r:p_len + r + fc, :]
                acc_r = acc_r + ur * kr - ui * ki
                acc_i = acc_i + ur * ki + ui * kr
            y_scr[i % 2, r:r + fc, :] = acc_r.astype(BF16)
            y_scr[i % 2, p_len + r:p_len + r + fc, :] = acc_i.astype(BF16)
        conv = jnp.dot(at_ref[...], y_scr[i % 2], preferred_element_type=F32)
        r0, n = block_rows(i)
        v = v_ref[0, r0:r0 + n, :].astype(F32)
        o_ref[0, r0:r0 + n, :] = (g_ref[0, r0:r0 + n, :].astype(F32) * (conv[:n] + v * bias_ref[0])).astype(o_ref.dtype)


def _hy_long_conv(a, at, spec, order, v, v_col, gate, gate_col, bias, p_len):
    b, t, _ = v.shape
    nb = -(-t // p_len)
    cw = BRANCH_W // 2
    nh = BRANCH_W // cw
    return pl.pallas_call(
        functools.partial(_hy_conv_kernel, t_len=t, p_len=p_len, nb=nb, fc=HYENA_FREQ_CHUNK),
        grid=(nh, b),
        in_specs=[pl.BlockSpec((2 * p_len, p_len), lambda c, i: (0, 0), pipeline_mode=pl.Buffered(1)),
                  pl.BlockSpec((p_len, 2 * p_len), lambda c, i: (0, 0), pipeline_mode=pl.Buffered(1)),
                  pl.BlockSpec((1, 2 * nb - 1, 2 * p_len, cw), lambda c, i: (order, 0, 0, c),
                               pipeline_mode=pl.Buffered(1)),
                  pl.BlockSpec((1, t, cw), lambda c, i: (i, 0, v_col * nh + c)),
                  pl.BlockSpec((1, t, cw), lambda c, i: (i, 0, gate_col * nh + c)),
                  pl.BlockSpec((1, 1, cw), lambda c, i: (order, 0, c))],
        out_specs=pl.BlockSpec((1, t, cw), lambda c, i: (i, 0, c)),
        out_shape=jax.ShapeDtypeStruct((b, t, BRANCH_W), BF16),
        scratch_shapes=[pltpu.VMEM((nb, 2 * p_len, cw), F32), pltpu.VMEM((2, 2 * p_len, cw), BF16)],
        compiler_params=_cp("arbitrary", "arbitrary"),
        name="hyena_conv",
    )(a, at, spec, v, gate, bias.reshape(HYENA_ORDER, 1, BRANCH_W))


def _softplus(x):
    return jnp.maximum(x, 0.0) + jnp.log(1.0 + jnp.exp(-jnp.abs(x)))


def _ssd_kernel(xs_ref, b_ref, c_ref, z_ref, dt_ref, dtb_ref, an_ref, dsk_ref, nw_ref, tri_ref, exp_ref, o_ref,
                yf_scr, yb_scr, *, t_len, nc, unroll):
    q = SSM_CHUNK
    gw = SSM_HPG * SSM_HEAD_DIM
    front = q - N_META
    ri = lax.broadcasted_iota(jnp.int32, (q, q), 0)
    ci = lax.broadcasted_iota(jnp.int32, (q, q), 1)
    tri = (ri >= ci, ri <= ci)
    head_of_lane = lax.broadcasted_iota(jnp.int32, (1, gw), 1) // SSM_HEAD_DIM
    lane = lax.broadcasted_iota(jnp.int32, (1, q), 1)
    dt_lane_mask = (lane < 2 * SSM_HPG).astype(F32)
    low_half = lane < SSM_HEAD_DIM
    dtb = dtb_ref[...]
    an = an_ref[...]
    masked_out = -1e30

    def widen(cols):
        return jnp.concatenate([jnp.where(low_half, cols[0], cols[1]), jnp.where(low_half, cols[2], cols[3])], axis=1)

    def cumsum(d, x):
        hi = x.astype(BF16)
        r1 = x - hi.astype(F32)
        mid = r1.astype(BF16)
        lo = (r1 - mid.astype(F32)).astype(BF16)
        s = jnp.dot(tri_ref[d], jnp.concatenate([hi, mid, lo], axis=1), preferred_element_type=F32)
        return s[:, :q] + s[:, q:2 * q] + s[:, 2 * q:]

    def load(ref, r0, nrows):
        return ref[0, pl.ds(r0, nrows), :]

    def dt_of(raw):
        return _softplus(raw.astype(F32) + dtb) * dt_lane_mask

    def chunk_data(c):
        r0 = pl.multiple_of(c * q - front, HALO)
        return load(xs_ref, r0, q), load(b_ref, r0, q), load(c_ref, r0, q), dt_of(load(dt_ref, r0, q))

    def chunk0_data():
        def pad(x):
            return jnp.concatenate([jnp.zeros((front, x.shape[1]), x.dtype), x], axis=0)
        return (pad(load(xs_ref, 0, N_META)), pad(load(b_ref, 0, N_META)), pad(load(c_ref, 0, N_META)),
                pad(dt_of(load(dt_ref, 0, N_META))))

    def stage_local(job, cs_n):
        (xs, bm, cm, dt), d = job
        lanes = [d * SSM_HPG + r for r in range(SSM_HPG)]
        cs_cols = [jnp.broadcast_to(cs_n[:, ln:ln + 1], (q, q)) for ln in lanes]
        cs_w = widen(cs_cols)
        dt_w = jnp.dot(dt.astype(BF16), exp_ref[d], preferred_element_type=F32).astype(BF16)
        tot = cs_w[q - 1:q, :] if d == 0 else cs_w[0:1, :]
        xt = xs * dt_w
        xd = xt * jnp.exp2(tot - cs_w).astype(BF16)
        s_new = lax.dot_general(bm, xd, (((0,), (0,)), ((), ())), preferred_element_type=F32)
        cb = lax.dot_general(cm, bm, (((1,), (1,)), ((), ())), preferred_element_type=F32)
        return cs_cols, xt, cb, cm, s_new, jnp.exp2(cs_w), jnp.exp2(tot), cs_n.T

    def stage_carry(loc, h):
        _, _, _, cm, s_new, ecs, etot, _ = loc
        y_off = jnp.dot(cm, h.astype(BF16), preferred_element_type=F32) * ecs
        return y_off, h * etot + s_new

    def stage_diag(job, cs_n, loc, y_off):
        d = job[1]
        cs_cols, xt, cb = loc[:3]
        cs_t = loc[7]
        ms, xm = [], []
        for r in range(SSM_HPG):
            ln = d * SSM_HPG + r
            dec = jnp.exp2(jnp.where(tri[d], cs_cols[r] - cs_t[ln:ln + 1, :], masked_out))
            ms.append((cb * dec).astype(BF16))
            xm.append(jnp.where(head_of_lane == r, xt, jnp.zeros_like(xt)))
        return y_off + jnp.dot(jnp.concatenate(ms, axis=1), jnp.concatenate(xm, axis=0), preferred_element_type=F32)

    def process_all(fwd_data, bwd_data, hf, hb):
        jobs = [(x, 0) for x in fwd_data] + [(x, 1) for x in bwd_data]
        cs = [cumsum(d, data[3] * an) for data, d in jobs]
        ys, pending = [], None
        for k, (job, c) in enumerate(zip(jobs, cs)):
            loc = stage_local(job, c)
            if pending is not None:
                ys.append(stage_diag(*pending))
            y_off, h = stage_carry(loc, hf if job[1] == 0 else hb)
            hf, hb = (h, hb) if job[1] == 0 else (hf, h)
            pending = (job, c, loc, y_off)
        ys.append(stage_diag(*pending))
        return ys[:len(fwd_data)], ys[len(fwd_data):], hf, hb

    def finish(y, r_out, nrows):
        y = y + load(xs_ref, r_out, nrows).astype(F32) * dsk_ref[...]
        y = y * _silu(load(z_ref, r_out, nrows).astype(F32))
        y = y * lax.rsqrt(jnp.mean(y * y, axis=-1, keepdims=True) + NORM_EPS)
        o_ref[0, pl.ds(r_out, nrows), :] = (y * nw_ref[...]).astype(o_ref.dtype)

    def rows(c):
        return pl.ds(pl.multiple_of(c * q, q), q)

    h0 = jnp.zeros((SSM_STATE, gw), F32)
    (y,), _, hf, _ = process_all([chunk0_data()], [], h0, h0)
    yf_scr[0:q, :] = y

    def scan_body(i, carry):
        hf, hb = carry
        cf = [1 + unroll * i + k for k in range(unroll)]
        cb_ = [nc - c for c in cf]
        yf, yb, hf, hb = process_all([chunk_data(c) for c in cf], [chunk_data(c) for c in cb_], hf, hb)
        for c, y in zip(cf, yf):
            yf_scr[rows(c), :] = y
        for c, y in zip(cb_, yb):
            yb_scr[rows(c), :] = y
        return hf, hb

    _, hb = lax.fori_loop(0, (nc - 1) // unroll, scan_body, (hf, h0))
    _, (y,), _, _ = process_all([], [chunk0_data()], h0, hb)
    yb_scr[0:q, :] = y

    finish(yf_scr[front:q, :] + yb_scr[front:q, :], 0, N_META)

    def finish_body(c, carry):
        finish(yf_scr[rows(c), :] + yb_scr[rows(c), :], pl.multiple_of(c * q - front, HALO), q)
        return carry

    lax.fori_loop(1, nc, finish_body, 0)


def _ssd_branch(xbc, proj, dt_bias, a_log, d_skip, norm_w):
    b, t, _ = xbc.shape
    q = SSM_CHUNK
    gw = SSM_HPG * SSM_HEAD_DIM
    nc = (q - N_META + t) // q
    assert nc * q == q - N_META + t, "sequence length minus meta tokens must be a multiple of the SSD chunk"
    a = -jnp.exp(a_log.astype(F32)).reshape(2, SSM_GROUPS, SSM_HPG)
    dtb = dt_bias.astype(F32).reshape(2, SSM_GROUPS, SSM_HPG)
    pad = lambda v: jnp.pad(jnp.transpose(v, (1, 0, 2)).reshape(SSM_GROUPS, 1, 2 * SSM_HPG),
                            ((0, 0), (0, 0), (0, q - 2 * SSM_HPG)))
    a_n = pad(a * math.log2(math.e))
    dtb_n = pad(dtb)
    ri = jnp.arange(q, dtype=jnp.int32)[:, None]
    ci = jnp.arange(q, dtype=jnp.int32)[None, :]
    tri = jnp.stack([ri >= ci, ri <= ci]).astype(BF16)
    cj = jnp.arange(gw, dtype=jnp.int32)[None, :] // SSM_HEAD_DIM
    expand = jnp.stack([ri == cj, ri == cj + SSM_HPG]).astype(BF16)
    dsk = jnp.repeat(d_skip.astype(F32).reshape(SSM_GROUPS, 1, SSM_HPG), SSM_HEAD_DIM, axis=-1)
    nw = norm_w.astype(F32).reshape(SSM_GROUPS, 1, gw)
    par = lambda shape: pl.BlockSpec((None,) + shape, lambda i, g: (g, 0, 0))
    return pl.pallas_call(
        functools.partial(_ssd_kernel, t_len=t, nc=nc, unroll=math.gcd(nc - 1, SSD_UNROLL)),
        grid=(b, SSM_GROUPS),
        in_specs=[pl.BlockSpec((1, t, gw), lambda i, g: (i, 0, g)),
                  pl.BlockSpec((1, t, SSM_STATE), lambda i, g: (i, 0, 4 + g)),
                  pl.BlockSpec((1, t, SSM_STATE), lambda i, g: (i, 0, 6 + g)),
                  pl.BlockSpec((1, t, gw), lambda i, g: (i, 0, COL_Z // gw + g)),
                  pl.BlockSpec((1, t, q), lambda i, g: (i, 0, COL_DT // q + g)),
                  par((1, q)), par((1, q)), par((1, gw)), par((1, gw)),
                  pl.BlockSpec((2, q, q), lambda i, g: (0, 0, 0)),
                  pl.BlockSpec((2, q, gw), lambda i, g: (0, 0, 0))],
        out_specs=pl.BlockSpec((1, t, gw), lambda i, g: (i, 0, g)),
        out_shape=jax.ShapeDtypeStruct((b, t, BRANCH_W), BF16),
        scratch_shapes=[pltpu.VMEM((nc * q, gw), F32), pltpu.VMEM((nc * q, gw), F32)],
        compiler_params=_cp("arbitrary", "arbitrary"),
        name="ssd",
    )(xbc, xbc, xbc, proj, proj, dtb_n, a_n, dsk, nw, tri, expand)


def _merge_kernel(h_ref, y0_ref, y1_ref, y2_ref, y3_ref, g0_ref, g1_ref, g2_ref, g3_ref, wb_ref, wo_ref, o_ref):
    merged = None
    for k, (y_ref, g_ref) in enumerate(zip((y0_ref, y1_ref, y2_ref, y3_ref), (g0_ref, g1_ref, g2_ref, g3_ref))):
        gate = _sigmoid(g_ref[...].astype(F32))
        term = gate * jnp.dot(y_ref[...], wb_ref[k], preferred_element_type=F32)
        merged = term if merged is None else merged + term
    o_ref[...] = h_ref[...] + jnp.dot(merged.astype(BF16), wo_ref[...], preferred_element_type=F32)


def _merge(h, ys, proj, wb, wo, tm):
    n, d = h.shape
    nb = len(ys)
    row = lambda w: pl.BlockSpec((tm, w), lambda i: (i, 0))
    return pl.pallas_call(
        _merge_kernel,
        grid=(pl.cdiv(n, tm),),
        in_specs=[row(d)] + [row(BRANCH_W)] * nb
                 + [pl.BlockSpec((tm, d), functools.partial(lambda i, k: (i, COL_GATE // d + k), k=k)) for k in range(nb)]
                 + [pl.BlockSpec((nb, BRANCH_W, d), lambda i: (0, 0, 0)),
                    pl.BlockSpec((d, d), lambda i: (0, 0))],
        out_specs=row(d),
        out_shape=jax.ShapeDtypeStruct((n, d), F32),
        compiler_params=_cp("arbitrary"),
        name="merge",
    )(h, *ys, *([proj] * nb), wb, wo)


def _rms_rows(x, g):
    return x * lax.rsqrt(jnp.mean(x * x, axis=-1, keepdims=True) + NORM_EPS) * g


def _stage_normed_rows(xs_ref, x_ref, xp_ref, xn_ref, g, row0, tr, t_len):
    li = lax.broadcasted_iota(jnp.int32, (tr, 1), 0)
    xs_ref[HALO:HALO + tr, :] = jnp.where(row0 + li < t_len, _rms_rows(x_ref[0], g), 0.0).astype(BF16)
    xs_ref[0:HALO, :] = jnp.where(row0 > 0, _rms_rows(xp_ref[0], g), 0.0).astype(BF16)
    xs_ref[HALO + tr:, :] = jnp.where(row0 + tr < t_len, _rms_rows(xn_ref[0], g), 0.0).astype(BF16)


def _conv3_rows(u, w):
    n = u.shape[0]
    return pltpu.roll(u, 1, 0) * w[0:1] + u * w[1:2] + pltpu.roll(u, n - 1, 0) * w[2:3]


def _row_halo_specs(tr, d, t_len):
    nh = tr // HALO
    last = t_len // HALO - 1
    return (pl.BlockSpec((1, tr, d), lambda b, i: (b, i, 0)),
            pl.BlockSpec((1, HALO, d), lambda b, i: (b, jnp.maximum(i * nh - 1, 0), 0)),
            pl.BlockSpec((1, HALO, d), lambda b, i: (b, jnp.minimum((i + 1) * nh, last), 0)))


def _resident(shape):
    return pl.BlockSpec(shape, lambda b, i: (0,) * len(shape), pipeline_mode=pl.Buffered(1))


def _col_chunks(n):
    return tuple((s0, min(MXU_COLS, n - s0)) for s0 in range(0, n, MXU_COLS))


def _conv_proj_kernel(x_ref, xp_ref, xn_ref, g_ref, w_ref, c_ref, b_ref, o_ref, xs_ref, *, tr, t_len, n_act):
    _stage_normed_rows(xs_ref, x_ref, xp_ref, xn_ref, g_ref[...], pl.program_id(1) * tr, tr, t_len)
    xs = xs_ref[...]
    for s0, w in _col_chunks(w_ref.shape[1]):
        u = jnp.dot(xs, w_ref[:, s0:s0 + w], preferred_element_type=F32)
        y = _conv3_rows(u, c_ref[:, s0:s0 + w])[HALO:HALO + tr] + b_ref[:, s0:s0 + w]
        o_ref[0, :, s0:s0 + w] = (_silu(y) if s0 < n_act else y).astype(o_ref.dtype)


def _conv_proj(h, g, w, taps, bias, n_act, tr):
    b, t, d = h.shape
    m = w.shape[1]
    main, prev, nxt = _row_halo_specs(tr, d, t)
    return pl.pallas_call(
        functools.partial(_conv_proj_kernel, tr=tr, t_len=t, n_act=n_act),
        grid=(b, pl.cdiv(t, tr)),
        in_specs=[main, prev, nxt, _resident((1, d)), _resident((d, m)), _resident((3, m)), _resident((1, m))],
        out_specs=pl.BlockSpec((1, tr, m), lambda b_, i: (b_, i, 0)),
        out_shape=jax.ShapeDtypeStruct((b, t, m), BF16),
        scratch_shapes=[pltpu.VMEM((tr + 2 * HALO, d), BF16)],
        compiler_params=_cp("arbitrary", "arbitrary"),
        name="conv_proj",
    )(h, h, h, g.reshape(1, d), w, taps, bias)


def _ffn_kernel(x_ref, xp_ref, xn_ref, g_ref, wu_ref, cw_ref, wd_ref, o_ref, xs_ref, gate_ref, *, tr, t_len):
    _stage_normed_rows(xs_ref, x_ref, xp_ref, xn_ref, g_ref[...], pl.program_id(1) * tr, tr, t_len)
    xs = xs_ref[...]
    dff = wd_ref.shape[0]
    for s0, w in _col_chunks(dff):
        a = _conv3_rows(jnp.dot(xs, wu_ref[:, s0:s0 + w], preferred_element_type=F32), cw_ref[:, s0:s0 + w])
        v = _conv3_rows(jnp.dot(xs, wu_ref[:, dff + s0:dff + s0 + w], preferred_element_type=F32),
                        cw_ref[:, dff + s0:dff + s0 + w])
        gate_ref[:, s0:s0 + w] = (_silu(a) * v)[HALO:HALO + tr].astype(BF16)
    o_ref[0] = x_ref[0] + jnp.dot(gate_ref[...], wd_ref[...], preferred_element_type=F32)


def _ffn(h, g, w_up, conv_w, w_down, tr):
    b, t, d = h.shape
    dff = w_down.shape[0]
    main, prev, nxt = _row_halo_specs(tr, d, t)
    return pl.pallas_call(
        functools.partial(_ffn_kernel, tr=tr, t_len=t),
        grid=(b, pl.cdiv(t, tr)),
        in_specs=[main, prev, nxt, _resident((1, d)), _resident((d, 2 * dff)), _resident((3, 2 * dff)),
                  _resident((dff, d))],
        out_specs=pl.BlockSpec((1, tr, d), lambda b_, i: (b_, i, 0)),
        out_shape=jax.ShapeDtypeStruct((b, t, d), F32),
        scratch_shapes=[pltpu.VMEM((tr + 2 * HALO, d), BF16), pltpu.VMEM((tr, dff), BF16)],
        compiler_params=_cp("arbitrary", "arbitrary"),
        name="ffn",
    )(h, h, h, g.reshape(1, d), w_up, conv_w, w_down)


def _final_norm_kernel(x_ref, g_ref, o_ref):
    x = x_ref[...]
    ms = jnp.mean(x * x, axis=-1, keepdims=True)
    o_ref[...] = x * lax.rsqrt(ms + NORM_EPS) * g_ref[...]


def _final_norm(x, g, tm):
    n, d = x.shape
    return pl.pallas_call(
        _final_norm_kernel,
        grid=(pl.cdiv(n, tm),),
        in_specs=[pl.BlockSpec((tm, d), lambda i: (i, 0)), pl.BlockSpec((1, d), lambda i: (0, 0))],
        out_specs=pl.BlockSpec((tm, d), lambda i: (i, 0)),
        out_shape=jax.ShapeDtypeStruct((n, d), F32),
        compiler_params=_cp("arbitrary"),
        name="final_norm",
    )(x, g.reshape(1, d))


def _prep_w_in(w_in):
    d = w_in.shape[0]
    o_z, o_xbc, o_dt = BRANCH_W, 2 * BRANCH_W, 2 * BRANCH_W + 1024
    ndt = 2 * SSM_GROUPS * SSM_HPG
    o_hy = o_dt + ndt
    o_sc = o_hy + 3 * BRANCH_W
    o_gate = o_sc + 3 * BRANCH_W
    w_dt = w_in[:, o_dt:o_hy].reshape(d, 2, SSM_GROUPS, SSM_HPG)
    w_dt = jnp.transpose(w_dt, (0, 2, 1, 3)).reshape(d, SSM_GROUPS, 2 * SSM_HPG)
    w_dt = jnp.pad(w_dt, ((0, 0), (0, 0), (0, SSM_CHUNK - 2 * SSM_HPG))).reshape(d, SSM_GROUPS * SSM_CHUNK)
    w_conv = jnp.concatenate([w_in[:, o_xbc:o_dt], w_in[:, o_hy:o_sc]], axis=1).astype(BF16)
    w_plain = jnp.concatenate([w_in[:, :o_xbc], w_in[:, o_gate:], w_in[:, o_sc:o_gate], w_dt], axis=1).astype(BF16)
    return w_conv, w_plain


def _hyena_fft_len(t):
    tf = 528
    nfb = -(-(2 * t - 1) // (2 * tf))
    return 2 * tf * nfb, tf


def _run_trunk(x, meta_tokens, norm_final, layers):
    b, seq, d = x.shape
    t = seq + N_META
    meta = jnp.broadcast_to(meta_tokens[None].astype(x.dtype), (b, N_META, d))
    h = jnp.concatenate([meta, x], axis=1).reshape(b * t, d)
    tr = _pick_tile(t, ROW_TILE)
    hy_a, hy_at = _hyena_tables(HYENA_BLOCK)
    hy_nb = -(-t // HYENA_BLOCK)
    tmf = tr
    tok = min(TOKEN_TILE, b * t)
    hv, hx1, hx2 = (CV_HY // BRANCH_W + k for k in range(3))
    for p in layers:
        w_conv, w_plain = p["w_in"]
        taps = jnp.concatenate([p["ssm_conv_w"], p["hyena_conv_w"]], axis=1)
        bias = jnp.concatenate([p["ssm_conv_b"], jnp.zeros((3 * BRANCH_W,), F32)]).reshape(1, N_CONV)
        projc = _conv_proj(h.reshape(b, t, d), p["norm_mix"], w_conv, taps, bias, CV_HY, tr)
        proj = _norm_matmul(h, p["norm_mix"], w_plain, min(512, tok), 768).reshape(b, t, N_PLAIN)
        y_fn = _fnet_branch(proj, tmf)
        y_ssm = _ssd_branch(projc, proj, p["ssm_dt_bias"], p["ssm_a_log"], p["ssm_d"], p["ssm_norm"])
        filt = _hyena_filters(t, p["hyena_w1"], p["hyena_b1"], p["hyena_w2"], p["hyena_b2"], p["hyena_w3"],
                              p["hyena_freq"])
        spec = _hy_filter_spectrum(hy_a, filt, HYENA_BLOCK, hy_nb)
        z = _hy_long_conv(hy_a, hy_at, spec, 0, projc, hv, projc, hx1, p["hyena_bias"], HYENA_BLOCK)
        y_hy = _hy_long_conv(hy_a, hy_at, spec, 1, z, 0, projc, hx2, p["hyena_bias"], HYENA_BLOCK)
        y_sc = _sc_branch(proj, p["sc_conv_w"], tr)
        flat = lambda a: a.reshape(b * t, a.shape[-1])
        h = _merge(h, [flat(y_fn), flat(y_ssm), flat(y_hy), flat(y_sc)], flat(proj), p["w_branch"], p["w_out"], min(512, tok))
        h = _ffn(h.reshape(b, t, d), p["norm_ffn"], p["w_up"], p["ffn_conv_w"], p["w_down"], tr).reshape(b * t, d)
    out = _final_norm(h, norm_final, tok).reshape(b, t, d)
    return out[:, N_META:]


def kernel(x_prompt, x_sample, meta_tokens, norm_mix, w_in, ssm_conv_w, ssm_conv_b, ssm_dt_bias, ssm_a_log, ssm_d,
           ssm_norm, hyena_conv_w, hyena_w1, hyena_b1, hyena_w2, hyena_b2, hyena_w3, hyena_freq, hyena_bias,
           sc_conv_w, w_branch, w_out, norm_ffn, ffn_conv_w, w_up, w_down, norm_final):
    depth = w_in.shape[0]
    layers = []
    for l in range(depth):
        layers.append(dict(
            norm_mix=norm_mix[l], w_in=_prep_w_in(w_in[l]), ssm_conv_w=ssm_conv_w[l], ssm_conv_b=ssm_conv_b[l],
            ssm_dt_bias=ssm_dt_bias[l], ssm_a_log=ssm_a_log[l], ssm_d=ssm_d[l], ssm_norm=ssm_norm[l],
            hyena_conv_w=hyena_conv_w[l], hyena_w1=hyena_w1[l], hyena_b1=hyena_b1[l], hyena_w2=hyena_w2[l],
            hyena_b2=hyena_b2[l], hyena_w3=hyena_w3[l], hyena_freq=hyena_freq[l], hyena_bias=hyena_bias[l],
            sc_conv_w=sc_conv_w[l], w_branch=w_branch[l].astype(BF16), w_out=w_out[l].astype(BF16),
            norm_ffn=norm_ffn[l], ffn_conv_w=ffn_conv_w[l], w_up=w_up[l].astype(BF16), w_down=w_down[l].astype(BF16)))
    y_prompt = _run_trunk(x_prompt, meta_tokens, norm_final, layers)
    y_sample = _run_trunk(x_sample, meta_tokens, norm_final, layers)
    return (y_prompt, y_sample)
```

```python
import functools
import math

import jax
import jax.numpy as jnp
from jax import lax
from jax.experimental import pallas as pl
from jax.experimental.pallas import tpu as pltpu

F32 = jnp.float32
BF16 = jnp.bfloat16
HI = lax.Precision.HIGHEST

NORM_EPS = 1e-6
N_META = 16
BRANCH_W = 512
FNET_GW = 128
SSM_GROUPS = 2
SSM_HPG = 4
SSM_HEAD_DIM = 64
SSM_STATE = 128
SSM_CHUNK = 128
HYENA_ORDER = 2
HYENA_EMB = 33
HYENA_TARGET = 1e-2
HYENA_FAST = 0.3
HYENA_SLOW = 1.5

ROW_TILE = 700
TOKEN_TILE = 1024
HYENA_BLOCK = 704
HYENA_FREQ_CHUNK = 32
MXU_COLS = 256
SSD_UNROLL = 4
HALO = 16
VMEM_LIMIT = 56 * 1024 * 1024

CV_XBC = 0
CV_HY = 1024
N_CONV = 2560
COL_FN = 0
COL_Z = 512
COL_GATE = 1024
COL_SC = 5120
COL_DT = 6656
N_PLAIN = 6912


def _cp(*sem):
    return pltpu.CompilerParams(dimension_semantics=sem, vmem_limit_bytes=VMEM_LIMIT)


def _pick_tile(n, target, mult=16):
    k = max(1, -(-n // target))
    t = -(-n // k)
    return -(-t // mult) * mult


def _silu(x):
    return x * (1.0 / (1.0 + jnp.exp(-x)))


def _sigmoid(x):
    return 1.0 / (1.0 + jnp.exp(-x))


def _norm_matmul_kernel(x_ref, g_ref, w_ref, o_ref, *, tn):
    x = x_ref[...]
    ms = jnp.mean(x * x, axis=-1, keepdims=True)
    xn = (x * lax.rsqrt(ms + NORM_EPS) * g_ref[...]).astype(BF16)
    for s0 in range(0, w_ref.shape[1], tn):
        o_ref[:, s0:s0 + tn] = jnp.dot(xn, w_ref[:, s0:s0 + tn], preferred_element_type=F32).astype(o_ref.dtype)


def _norm_matmul(x, g, w, tm, tn):
    n, d = x.shape
    m = w.shape[1]
    assert m % tn == 0
    tm = min(tm, -(-n // HALO) * HALO)
    return pl.pallas_call(
        functools.partial(_norm_matmul_kernel, tn=tn),
        grid=(pl.cdiv(n, tm),),
        in_specs=[pl.BlockSpec((tm, d), lambda i: (i, 0)),
                  pl.BlockSpec((1, d), lambda i: (0, 0), pipeline_mode=pl.Buffered(1)),
                  pl.BlockSpec((d, m), lambda i: (0, 0), pipeline_mode=pl.Buffered(1))],
        out_specs=pl.BlockSpec((tm, m), lambda i: (i, 0)),
        out_shape=jax.ShapeDtypeStruct((n, m), BF16),
        compiler_params=_cp("arbitrary"),
        name="norm_matmul",
    )(x, g.reshape(1, d), w)


def _conv3(x, prev_row, next_row, w, row0, t_len):
    n = x.shape[0]
    li = lax.broadcasted_iota(jnp.int32, (n, 1), 0)
    gi = li + row0
    xm = pltpu.roll(x, 1, 0)
    xm = jnp.where(li == 0, prev_row, xm)
    xm = jnp.where(gi >= 1, xm, 0.0)
    xp = pltpu.roll(x, n - 1, 0)
    xp = jnp.where(li == n - 1, next_row, xp)
    xp = jnp.where(gi + 1 < t_len, xp, 0.0)
    return xm * w[0:1] + x * w[1:2] + xp * w[2:3]


def _last_row(halo_ref):
    return halo_ref[0].astype(F32)[HALO - 1:HALO]


def _first_row(halo_ref):
    return halo_ref[0].astype(F32)[0:1]


def _halo_specs(tr, cw, t_len, col_blk):
    nh = tr // HALO
    last = t_len // HALO - 1
    main = pl.BlockSpec((1, tr, cw), lambda b, i, j: (b, i, col_blk + j))
    prev = pl.BlockSpec((1, HALO, cw), lambda b, i, j: (b, jnp.maximum(i * nh - 1, 0), col_blk + j))
    nxt = pl.BlockSpec((1, HALO, cw), lambda b, i, j: (b, jnp.minimum((i + 1) * nh, last), col_blk + j))
    return main, prev, nxt


def _sc_kernel(bg_ref, cg_ref, xi_ref, cgp_ref, xip_ref, cgn_ref, xin_ref, w_ref, o_ref, *, tr, t_len):
    row0 = pl.program_id(1) * tr
    u = cg_ref[0].astype(F32) * xi_ref[0].astype(F32)
    prev = _last_row(cgp_ref) * _last_row(xip_ref)
    nxt = _first_row(cgn_ref) * _first_row(xin_ref)
    y = _conv3(u, prev, nxt, w_ref[...], row0, t_len)
    o_ref[0] = (bg_ref[0].astype(F32) * y).astype(o_ref.dtype)


def _sc_branch(proj, w, tr):
    b, t, _ = proj.shape
    cw = BRANCH_W
    c0 = COL_SC // cw
    bg, _, _ = _halo_specs(tr, cw, t, c0)
    cg, cgp, cgn = _halo_specs(tr, cw, t, c0 + 1)
    xi, xip, xin = _halo_specs(tr, cw, t, c0 + 2)
    return pl.pallas_call(
        functools.partial(_sc_kernel, tr=tr, t_len=t),
        grid=(b, pl.cdiv(t, tr), 1),
        in_specs=[bg, cg, xi, cgp, xip, cgn, xin, pl.BlockSpec((3, cw), lambda b_, i, j: (0, 0))],
        out_specs=pl.BlockSpec((1, tr, cw), lambda b_, i, j: (b_, i, 0)),
        out_shape=jax.ShapeDtypeStruct((b, t, cw), BF16),
        compiler_params=_cp("arbitrary", "arbitrary", "arbitrary"),
        name="sc_branch",
    )(proj, proj, proj, proj, proj, proj, proj, w)


def _fnet_kernel(a_ref, u_ref, cs_ref, o_ref, *, tmf):
    p = jnp.dot(a_ref[0], u_ref[0], preferred_element_type=F32)
    pc = p[:tmf].astype(BF16)
    ps = p[tmf:].astype(BF16)
    y = jnp.dot(pc, cs_ref[0], preferred_element_type=F32) - jnp.dot(ps, cs_ref[1], preferred_element_type=F32)
    o_ref[0] = y.astype(o_ref.dtype)


def _fnet_tables(t, tmf):
    nm = -(-t // tmf)
    assert t % HALO == 0
    j = jnp.arange(nm * tmf, dtype=jnp.int32)[:, None]
    ang = lambda k: ((j * k[None, :]) % t).astype(F32) * (2.0 * math.pi / t)
    ang_hi = ang(jnp.arange(t // HALO, dtype=jnp.int32) * HALO)[:, :, None]
    ang_lo = ang(jnp.arange(HALO, dtype=jnp.int32))[:, None, :]
    valid = (j < t)[:, :, None]
    c = jnp.where(valid, jnp.cos(ang_hi) * jnp.cos(ang_lo) - jnp.sin(ang_hi) * jnp.sin(ang_lo), 0.0)
    s = jnp.where(valid, jnp.sin(ang_hi) * jnp.cos(ang_lo) + jnp.cos(ang_hi) * jnp.sin(ang_lo), 0.0)
    a = jnp.concatenate([c.astype(BF16).reshape(nm, tmf, t), s.astype(BF16).reshape(nm, tmf, t)], axis=1)
    jj = jnp.arange(BRANCH_W, dtype=jnp.int32)[:, None]
    kk = jnp.arange(BRANCH_W, dtype=jnp.int32)[None, :]
    same = (jj // FNET_GW) == (kk // FNET_GW)
    ang2 = (((jj % FNET_GW) * (kk % FNET_GW)) % FNET_GW).astype(F32) * (2.0 * math.pi / FNET_GW)
    scale = 1.0 / math.sqrt(t * FNET_GW)
    cc = jnp.where(same, jnp.cos(ang2), 0.0) * scale
    sc = jnp.where(same, jnp.sin(ang2), 0.0) * scale
    return a, jnp.stack([cc, sc]).astype(BF16)


def _fnet_branch(proj, tmf):
    b, t, _ = proj.shape
    a, cs = _fnet_tables(t, tmf)
    nm = a.shape[0]
    return pl.pallas_call(
        functools.partial(_fnet_kernel, tmf=tmf),
        grid=(nm, b),
        in_specs=[pl.BlockSpec((1, 2 * tmf, t), lambda i, j: (i, 0, 0)),
                  pl.BlockSpec((1, t, BRANCH_W), lambda i, j: (j, 0, COL_FN // BRANCH_W)),
                  pl.BlockSpec((2, BRANCH_W, BRANCH_W), lambda i, j: (0, 0, 0))],
        out_specs=pl.BlockSpec((1, tmf, BRANCH_W), lambda i, j: (j, i, 0)),
        out_shape=jax.ShapeDtypeStruct((b, t, BRANCH_W), BF16),
        compiler_params=_cp("arbitrary", "arbitrary"),
        name="fnet",
    )(a, proj, cs)


def _hyena_filter_kernel(w1t_ref, w1c_ref, w1s_ref, b1_ref, w2_ref, b2_ref, fq_ref, w3f_ref, w3b_ref, dl_ref,
                         o_ref, hidf_ref, hidb_ref, *, t, half):
    row = lax.broadcasted_iota(jnp.int32, (t, 1), 0)
    pos_f = row.astype(F32)
    pos_b = (t - row).astype(F32)

    def hidden(pos):
        tt = pos * (1.0 / (t - 1))
        wv = pos * (2.0 * math.pi / t)
        bands = (HYENA_EMB - 1) // 2
        fi = lax.broadcasted_iota(jnp.int32, (1, bands), 1).astype(F32)
        fr = 1e-4 + fi * ((bands - 1 - 1e-4) / (bands - 1))
        arg = wv * fr
        pre = (tt * w1t_ref[...]
               + jnp.dot(jnp.cos(arg), w1c_ref[...], precision=HI, preferred_element_type=F32)
               - jnp.dot(jnp.sin(arg), w1s_ref[...], precision=HI, preferred_element_type=F32)
               + b1_ref[...])
        fq = fq_ref[...]
        h1 = jnp.sin(fq * pre)
        return jnp.sin(fq * (jnp.dot(h1, w2_ref[...], precision=HI, preferred_element_type=F32) + b2_ref[...]))

    @pl.when(pl.program_id(0) == 0)
    def _():
        hidf_ref[...] = hidden(pos_f)
        hidb_ref[...] = hidden(pos_b)

    def taps(hid_ref, w3_ref, pos):
        dec = jnp.exp(-(pos * (1.0 / (t - 1))) * dl_ref[...])
        return jnp.dot(hid_ref[...], w3_ref[...], precision=HI, preferred_element_type=F32) * dec

    hf = taps(hidf_ref, w3f_ref, pos_f)
    hb = jnp.where(row >= 1, taps(hidb_ref, w3b_ref, pos_b), 0.0)
    l1 = jnp.sum(jnp.abs(hf), axis=0, keepdims=True) + jnp.sum(jnp.abs(hb), axis=0, keepdims=True)
    inv = 1.0 / l1
    cw = o_ref.shape[2]
    o_ref[0, half - t:half, :] = (hb * inv).astype(o_ref.dtype)
    o_ref[0, half:half + t, :] = (hf * inv).astype(o_ref.dtype)
    if half > t:
        o_ref[0, 0:half - t, :] = jnp.zeros((half - t, cw), o_ref.dtype)
        o_ref[0, half + t:, :] = jnp.zeros((half - t, cw), o_ref.dtype)


def _hyena_filters(t, half, w1, b1, w2, b2, w3, freq):
    cw = 128
    nb = BRANCH_W // cw
    bands = (HYENA_EMB - 1) // 2
    nf = w2.shape[0]
    max_decay = math.log(HYENA_TARGET) / HYENA_FAST
    min_decay = math.log(HYENA_TARGET) / HYENA_SLOW
    deltas = jnp.abs(jnp.linspace(min_decay, max_decay, BRANCH_W, dtype=F32)).reshape(1, BRANCH_W)
    full = lambda shape: pl.BlockSpec(shape, lambda g: (0,) * len(shape))
    return pl.pallas_call(
        functools.partial(_hyena_filter_kernel, t=t, half=half),
        grid=(HYENA_ORDER * nb,),
        in_specs=[full((1, nf)), full((bands, nf)), full((bands, nf)), full((1, nf)), full((nf, nf)),
                  full((1, nf)), full((1, nf)),
                  pl.BlockSpec((nf, cw), lambda g: (0, (g // nb) * 2 * nb + g % nb)),
                  pl.BlockSpec((nf, cw), lambda g: (0, (g // nb) * 2 * nb + nb + g % nb)),
                  pl.BlockSpec((1, cw), lambda g: (0, g % nb))],
        out_specs=pl.BlockSpec((1, 2 * half, cw), lambda g: (g // nb, 0, g % nb)),
        out_shape=jax.ShapeDtypeStruct((HYENA_ORDER, 2 * half, BRANCH_W), BF16),
        scratch_shapes=[pltpu.VMEM((t, nf), F32), pltpu.VMEM((t, nf), F32)],
        compiler_params=_cp("arbitrary"),
        name="hyena_filter",
    )(w1[0:1], w1[1:1 + bands], w1[1 + bands:], b1.reshape(1, nf), w2, b2.reshape(1, nf), freq.reshape(1, nf),
      w3, w3, deltas)


def _hyena_tables(p_len):
    f = jnp.arange(p_len, dtype=jnp.int32)[:, None]
    n = jnp.arange(p_len, dtype=jnp.int32)[None, :]
    ang = (((2 * f + 1) * n) % (4 * p_len)).astype(F32) * (math.pi / (2 * p_len))
    a = jnp.concatenate([jnp.cos(ang), -jnp.sin(ang)], axis=0).astype(BF16)
    return a, a.T


def _hy_spec_kernel(a_ref, seg_ref, prev_ref, o_ref, *, p_len):
    g = jnp.dot(a_ref[...], seg_ref[0, 0], preferred_element_type=F32)
    gp = jnp.dot(a_ref[...], prev_ref[0, 0], preferred_element_type=F32)
    first = prev_ref[0, 0, 0:1, :].astype(F32)
    f = lax.broadcasted_iota(jnp.int32, (p_len, 1), 0)
    sign = jnp.where(f % 2 == 0, 1.0, -1.0)
    scale = 1.0 / p_len
    o_ref[0, 0, :p_len] = (g[:p_len] - sign * gp[p_len:]) * scale
    o_ref[0, 0, p_len:] = (g[p_len:] + sign * (gp[:p_len] - first)) * scale


def _hy_filter_spectrum(a, filt, p_len, nb):
    order, _, w = filt.shape
    lags = filt.reshape(order, 2 * nb, p_len, w)
    return pl.pallas_call(
        functools.partial(_hy_spec_kernel, p_len=p_len),
        grid=(order, 2 * nb - 1),
        in_specs=[pl.BlockSpec((2 * p_len, p_len), lambda o, d: (0, 0)),
                  pl.BlockSpec((1, 1, p_len, w), lambda o, d: (o, d + 1, 0, 0)),
                  pl.BlockSpec((1, 1, p_len, w), lambda o, d: (o, d, 0, 0))],
        out_specs=pl.BlockSpec((1, 1, 2 * p_len, w), lambda o, d: (o, d, 0, 0)),
        out_shape=jax.ShapeDtypeStruct((order, 2 * nb - 1, 2 * p_len, w), F32),
        compiler_params=_cp("arbitrary", "arbitrary"),
        name="hyena_filter_spectrum",
    )(a, lags, lags)


def _hy_conv_kernel(a_ref, at_ref, k_ref, v_ref, g_ref, bias_ref, o_ref, u_scr, y_scr, *, t_len, p_len, nb, fc):
    cw = v_ref.shape[2]

    def block_rows(i):
        return i * p_len, min(p_len, t_len - i * p_len)

    for j in range(nb):
        r0, n = block_rows(j)
        u = v_ref[0, r0:r0 + n, :]
        if n < p_len:
            u = jnp.concatenate([u, jnp.zeros((p_len - n, cw), u.dtype)], axis=0)
        u_scr[j] = jnp.dot(a_ref[...], u, preferred_element_type=F32)

    for i in range(nb):
        for r in range(0, p_len, fc):
            acc_r = jnp.zeros((fc, cw), F32)
            acc_i = jnp.zeros((fc, cw), F32)
            for j in range(nb):
                d = i - j + nb - 1
                ur, ui = u_scr[j, r:r + fc, :], u_scr[j, p_len + r:p_len + r + fc, :]
                kr, ki = k_ref[0, d, r:r + fc, :], k_ref[0, d, p_len + r:p_len + r + fc, :]
                acc_r = acc_r + ur * kr - ui * ki
                acc_i = acc_i + ur * ki + ui * kr
            y_scr[i % 2, r:r + fc, :] = acc_r.astype(BF16)
            y_scr[i % 2, p_len + r:p_len + r + fc, :] = acc_i.astype(BF16)
        conv = jnp.dot(at_ref[...], y_scr[i % 2], preferred_element_type=F32)
        r0, n = block_rows(i)
        v = v_ref[0, r0:r0 + n, :].astype(F32)
        o_ref[0, r0:r0 + n, :] = (g_ref[0, r0:r0 + n, :].astype(F32) * (conv[:n] + v * bias_ref[0])).astype(o_ref.dtype)


def _hy_long_conv(a, at, spec, order, v, v_col, gate, gate_col, bias, p_len):
    b, t, _ = v.shape
    nb = -(-t // p_len)
    cw = BRANCH_W // 2
    nh = BRANCH_W // cw
    return pl.pallas_call(
        functools.partial(_hy_conv_kernel, t_len=t, p_len=p_len, nb=nb, fc=HYENA_FREQ_CHUNK),
        grid=(nh, b),
        in_specs=[pl.BlockSpec((2 * p_len, p_len), lambda c, i: (0, 0), pipeline_mode=pl.Buffered(1)),
                  pl.BlockSpec((p_len, 2 * p_len), lambda c, i: (0, 0), pipeline_mode=pl.Buffered(1)),
                  pl.BlockSpec((1, 2 * nb - 1, 2 * p_len, cw), lambda c, i: (order, 0, 0, c),
                               pipeline_mode=pl.Buffered(1)),
                  pl.BlockSpec((1, t, cw), lambda c, i: (i, 0, v_col * nh + c)),
                  pl.BlockSpec((1, t, cw), lambda c, i: (i, 0, gate_col * nh + c)),
                  pl.BlockSpec((1, 1, cw), lambda c, i: (order, 0, c))],
        out_specs=pl.BlockSpec((1, t, cw), lambda c, i: (i, 0, c)),
        out_shape=jax.ShapeDtypeStruct((b, t, BRANCH_W), BF16),
        scratch_shapes=[pltpu.VMEM((nb, 2 * p_len, cw), F32), pltpu.VMEM((2, 2 * p_len, cw), BF16)],
        compiler_params=_cp("arbitrary", "arbitrary"),
        name="hyena_conv",
    )(a, at, spec, v, gate, bias.reshape(HYENA_ORDER, 1, BRANCH_W))


def _softplus(x):
    return jnp.maximum(x, 0.0) + jnp.log(1.0 + jnp.exp(-jnp.abs(x)))


def _ssd_kernel(xs_ref, b_ref, c_ref, z_ref, dt_ref, dtb_ref, an_ref, dsk_ref, nw_ref, tri_ref, exp_ref, o_ref,
                yf_scr, yb_scr, *, t_len, nc, unroll):
    q = SSM_CHUNK
    gw = SSM_HPG * SSM_HEAD_DIM
    front = q - N_META
    ri = lax.broadcasted_iota(jnp.int32, (q, q), 0)
    ci = lax.broadcasted_iota(jnp.int32, (q, q), 1)
    tri = (ri >= ci, ri <= ci)
    head_of_lane = lax.broadcasted_iota(jnp.int32, (1, gw), 1) // SSM_HEAD_DIM
    lane = lax.broadcasted_iota(jnp.int32, (1, q), 1)
    dt_lane_mask = (lane < 2 * SSM_HPG).astype(F32)
    low_half = lane < SSM_HEAD_DIM
    dtb = dtb_ref[...]
    an = an_ref[...]
    masked_out = -1e30

    def widen(cols):
        return jnp.concatenate([jnp.where(low_half, cols[0], cols[1]), jnp.where(low_half, cols[2], cols[3])], axis=1)

    def cumsum(d, x):
        hi = x.astype(BF16)
        r1 = x - hi.astype(F32)
        mid = r1.astype(BF16)
        lo = (r1 - mid.astype(F32)).astype(BF16)
        s = jnp.dot(tri_ref[d], jnp.concatenate([hi, mid, lo], axis=1), preferred_element_type=F32)
        return s[:, :q] + s[:, q:2 * q] + s[:, 2 * q:]

    def load(ref, r0, nrows):
        return ref[0, pl.ds(r0, nrows), :]

    def dt_of(raw):
        return _softplus(raw.astype(F32) + dtb) * dt_lane_mask

    def chunk_data(c):
        r0 = pl.multiple_of(c * q - front, HALO)
        return load(xs_ref, r0, q), load(b_ref, r0, q), load(c_ref, r0, q), dt_of(load(dt_ref, r0, q))

    def chunk0_data():
        def pad(x):
            return jnp.concatenate([jnp.zeros((front, x.shape[1]), x.dtype), x], axis=0)
        return (pad(load(xs_ref, 0, N_META)), pad(load(b_ref, 0, N_META)), pad(load(c_ref, 0, N_META)),
                pad(dt_of(load(dt_ref, 0, N_META))))

    def stage_local(job, cs_n):
        (xs, bm, cm, dt), d = job
        lanes = [d * SSM_HPG + r for r in range(SSM_HPG)]
        cs_cols = [jnp.broadcast_to(cs_n[:, ln:ln + 1], (q, q)) for ln in lanes]
        cs_w = widen(cs_cols)
        dt_w = jnp.dot(dt.astype(BF16), exp_ref[d], preferred_element_type=F32).astype(BF16)
        tot = cs_w[q - 1:q, :] if d == 0 else cs_w[0:1, :]
        xt = xs * dt_w
        xd = xt * jnp.exp2(tot - cs_w).astype(BF16)
        s_new = lax.dot_general(bm, xd, (((0,), (0,)), ((), ())), preferred_element_type=F32)
        cb = lax.dot_general(cm, bm, (((1,), (1,)), ((), ())), preferred_element_type=F32)
        return cs_cols, xt, cb, cm, s_new, jnp.exp2(cs_w), jnp.exp2(tot), cs_n.T

    def stage_carry(loc, h):
        _, _, _, cm, s_new, ecs, etot, _ = loc
        y_off = jnp.dot(cm, h.astype(BF16), preferred_element_type=F32) * ecs
        return y_off, h * etot + s_new

    def stage_diag(job, cs_n, loc, y_off):
        d = job[1]
        cs_cols, xt, cb = loc[:3]
        cs_t = loc[7]
        ms, xm = [], []
        for r in range(SSM_HPG):
            ln = d * SSM_HPG + r
            dec = jnp.exp2(jnp.where(tri[d], cs_cols[r] - cs_t[ln:ln + 1, :], masked_out))
            ms.append((cb * dec).astype(BF16))
            xm.append(jnp.where(head_of_lane == r, xt, jnp.zeros_like(xt)))
        return y_off + jnp.dot(jnp.concatenate(ms, axis=1), jnp.concatenate(xm, axis=0), preferred_element_type=F32)

    def process_all(fwd_data, bwd_data, hf, hb):
        jobs = [(x, 0) for x in fwd_data] + [(x, 1) for x in bwd_data]
        cs = [cumsum(d, data[3] * an) for data, d in jobs]
        ys, pending = [], None
        for k, (job, c) in enumerate(zip(jobs, cs)):
            loc = stage_local(job, c)
            if pending is not None:
                ys.append(stage_diag(*pending))
            y_off, h = stage_carry(loc, hf if job[1] == 0 else hb)
            hf, hb = (h, hb) if job[1] == 0 else (hf, h)
            pending = (job, c, loc, y_off)
        ys.append(stage_diag(*pending))
        return ys[:len(fwd_data)], ys[len(fwd_data):], hf, hb

    def finish(y, r_out, nrows):
        y = y + load(xs_ref, r_out, nrows).astype(F32) * dsk_ref[...]
        y = y * _silu(load(z_ref, r_out, nrows).astype(F32))
        y = y * lax.rsqrt(jnp.mean(y * y, axis=-1, keepdims=True) + NORM_EPS)
        o_ref[0, pl.ds(r_out, nrows), :] = (y * nw_ref[...]).astype(o_ref.dtype)

    def rows(c):
        return pl.ds(pl.multiple_of(c * q, q), q)

    h0 = jnp.zeros((SSM_STATE, gw), F32)
    (y,), _, hf, _ = process_all([chunk0_data()], [], h0, h0)
    yf_scr[0:q, :] = y

    def scan_body(i, carry):
        hf, hb = carry
        cf = [1 + unroll * i + k for k in range(unroll)]
        cb_ = [nc - c for c in cf]
        yf, yb, hf, hb = process_all([chunk_data(c) for c in cf], [chunk_data(c) for c in cb_], hf, hb)
        for c, y in zip(cf, yf):
            yf_scr[rows(c), :] = y
        for c, y in zip(cb_, yb):
            yb_scr[rows(c), :] = y
        return hf, hb

    _, hb = lax.fori_loop(0, (nc - 1) // unroll, scan_body, (hf, h0))
    _, (y,), _, _ = process_all([], [chunk0_data()], h0, hb)
    yb_scr[0:q, :] = y

    finish(yf_scr[front:q, :] + yb_scr[front:q, :], 0, N_META)

    def finish_body(c, carry):
        finish(yf_scr[rows(c), :] + yb_scr[rows(c), :], pl.multiple_of(c * q - front, HALO), q)
        return carry

    lax.fori_loop(1, nc, finish_body, 0)


def _ssd_branch(xbc, proj, dt_bias, a_log, d_skip, norm_w):
    b, t, _ = xbc.shape
    q = SSM_CHUNK
    gw = SSM_HPG * SSM_HEAD_DIM
    nc = (q - N_META + t) // q
    assert nc * q == q - N_META + t, "sequence length minus meta tokens must be a multiple of the SSD chunk"
    a = -jnp.exp(a_log.astype(F32)).reshape(2, SSM_GROUPS, SSM_HPG)
    dtb = dt_bias.astype(F32).reshape(2, SSM_GROUPS, SSM_HPG)
    pad = lambda v: jnp.pad(jnp.transpose(v, (1, 0, 2)).reshape(SSM_GROUPS, 1, 2 * SSM_HPG),
                            ((0, 0), (0, 0), (0, q - 2 * SSM_HPG)))
    a_n = pad(a * math.log2(math.e))
    dtb_n = pad(dtb)
    ri = jnp.arange(q, dtype=jnp.int32)[:, None]
    ci = jnp.arange(q, dtype=jnp.int32)[None, :]
    tri = jnp.stack([ri >= ci, ri <= ci]).astype(BF16)
    cj = jnp.arange(gw, dtype=jnp.int32)[None, :] // SSM_HEAD_DIM
    expand = jnp.stack([ri == cj, ri == cj + SSM_HPG]).astype(BF16)
    dsk = jnp.repeat(d_skip.astype(F32).reshape(SSM_GROUPS, 1, SSM_HPG), SSM_HEAD_DIM, axis=-1)
    nw = norm_w.astype(F32).reshape(SSM_GROUPS, 1, gw)
    par = lambda shape: pl.BlockSpec((None,) + shape, lambda i, g: (g, 0, 0))
    return pl.pallas_call(
        functools.partial(_ssd_kernel, t_len=t, nc=nc, unroll=math.gcd(nc - 1, SSD_UNROLL)),
        grid=(b, SSM_GROUPS),
        in_specs=[pl.BlockSpec((1, t, gw), lambda i, g: (i, 0, g)),
                  pl.BlockSpec((1, t, SSM_STATE), lambda i, g: (i, 0, 4 + g)),
                  pl.BlockSpec((1, t, SSM_STATE), lambda i, g: (i, 0, 6 + g)),
                  pl.BlockSpec((1, t, gw), lambda i, g: (i, 0, COL_Z // gw + g)),
                  pl.BlockSpec((1, t, q), lambda i, g: (i, 0, COL_DT // q + g)),
                  par((1, q)), par((1, q)), par((1, gw)), par((1, gw)),
                  pl.BlockSpec((2, q, q), lambda i, g: (0, 0, 0)),
                  pl.BlockSpec((2, q, gw), lambda i, g: (0, 0, 0))],
        out_specs=pl.BlockSpec((1, t, gw), lambda i, g: (i, 0, g)),
        out_shape=jax.ShapeDtypeStruct((b, t, BRANCH_W), BF16),
        scratch_shapes=[pltpu.VMEM((nc * q, gw), F32), pltpu.VMEM((nc * q, gw), F32)],
        compiler_params=_cp("arbitrary", "arbitrary"),
        name="ssd",
    )(xbc, xbc, xbc, proj, proj, dtb_n, a_n, dsk, nw, tri, expand)


def _merge_kernel(h_ref, y0_ref, y1_ref, y2_ref, y3_ref, g0_ref, g1_ref, g2_ref, g3_ref, wb_ref, wo_ref, o_ref):
    merged = None
    for k, (y_ref, g_ref) in enumerate(zip((y0_ref, y1_ref, y2_ref, y3_ref), (g0_ref, g1_ref, g2_ref, g3_ref))):
        gate = _sigmoid(g_ref[...].astype(F32))
        term = gate * jnp.dot(y_ref[...], wb_ref[k], preferred_element_type=F32)
        merged = term if merged is None else merged + term
    o_ref[...] = h_ref[...] + jnp.dot(merged.astype(BF16), wo_ref[...], preferred_element_type=F32)


def _merge(h, ys, proj, wb, wo, tm):
    n, d = h.shape
    nb = len(ys)
    row = lambda w: pl.BlockSpec((tm, w), lambda i: (i, 0))
    return pl.pallas_call(
        _merge_kernel,
        grid=(pl.cdiv(n, tm),),
        in_specs=[row(d)] + [row(BRANCH_W)] * nb
                 + [pl.BlockSpec((tm, d), functools.partial(lambda i, k: (i, COL_GATE // d + k), k=k)) for k in range(nb)]
                 + [pl.BlockSpec((nb, BRANCH_W, d), lambda i: (0, 0, 0)),
                    pl.BlockSpec((d, d), lambda i: (0, 0))],
        out_specs=row(d),
        out_shape=jax.ShapeDtypeStruct((n, d), F32),
        compiler_params=_cp("arbitrary"),
        name="merge",
    )(h, *ys, *([proj] * nb), wb, wo)


def _rms_rows(x, g):
    return x * lax.rsqrt(jnp.mean(x * x, axis=-1, keepdims=True) + NORM_EPS) * g


def _stage_normed_rows(xs_ref, x_ref, xp_ref, xn_ref, g, row0, tr, t_len):
    li = lax.broadcasted_iota(jnp.int32, (tr, 1), 0)
    xs_ref[HALO:HALO + tr, :] = jnp.where(row0 + li < t_len, _rms_rows(x_ref[0], g), 0.0).astype(BF16)
    xs_ref[0:HALO, :] = jnp.where(row0 > 0, _rms_rows(xp_ref[0], g), 0.0).astype(BF16)
    xs_ref[HALO + tr:, :] = jnp.where(row0 + tr < t_len, _rms_rows(xn_ref[0], g), 0.0).astype(BF16)


def _conv3_rows(u, w):
    n = u.shape[0]
    return pltpu.roll(u, 1, 0) * w[0:1] + u * w[1:2] + pltpu.roll(u, n - 1, 0) * w[2:3]


def _row_halo_specs(tr, d, t_len):
    nh = tr // HALO
    last = t_len // HALO - 1
    return (pl.BlockSpec((1, tr, d), lambda b, i: (b, i, 0)),
            pl.BlockSpec((1, HALO, d), lambda b, i: (b, jnp.maximum(i * nh - 1, 0), 0)),
            pl.BlockSpec((1, HALO, d), lambda b, i: (b, jnp.minimum((i + 1) * nh, last), 0)))


def _resident(shape):
    return pl.BlockSpec(shape, lambda b, i: (0,) * len(shape), pipeline_mode=pl.Buffered(1))


def _col_chunks(n):
    return tuple((s0, min(MXU_COLS, n - s0)) for s0 in range(0, n, MXU_COLS))


def _conv_proj_kernel(x_ref, xp_ref, xn_ref, g_ref, w_ref, c_ref, b_ref, o_ref, xs_ref, *, tr, t_len, n_act):
    _stage_normed_rows(xs_ref, x_ref, xp_ref, xn_ref, g_ref[...], pl.program_id(1) * tr, tr, t_len)
    xs = xs_ref[...]
    for s0, w in _col_chunks(w_ref.shape[1]):
        u = jnp.dot(xs, w_ref[:, s0:s0 + w], preferred_element_type=F32)
        y = _conv3_rows(u, c_ref[:, s0:s0 + w])[HALO:HALO + tr] + b_ref[:, s0:s0 + w]
        o_ref[0, :, s0:s0 + w] = (_silu(y) if s0 < n_act else y).astype(o_ref.dtype)


def _conv_proj(h, g, w, taps, bias, n_act, tr):
    b, t, d = h.shape
    m = w.shape[1]
    main, prev, nxt = _row_halo_specs(tr, d, t)
    return pl.pallas_call(
        functools.partial(_conv_proj_kernel, tr=tr, t_len=t, n_act=n_act),
        grid=(b, pl.cdiv(t, tr)),
        in_specs=[main, prev, nxt, _resident((1, d)), _resident((d, m)), _resident((3, m)), _resident((1, m))],
        out_specs=pl.BlockSpec((1, tr, m), lambda b_, i: (b_, i, 0)),
        out_shape=jax.ShapeDtypeStruct((b, t, m), BF16),
        scratch_shapes=[pltpu.VMEM((tr + 2 * HALO, d), BF16)],
        compiler_params=_cp("arbitrary", "arbitrary"),
        name="conv_proj",
    )(h, h, h, g.reshape(1, d), w, taps, bias)


def _ffn_kernel(x_ref, xp_ref, xn_ref, g_ref, wu_ref, cw_ref, wd_ref, o_ref, xs_ref, gate_ref, *, tr, t_len):
    _stage_normed_rows(xs_ref, x_ref, xp_ref, xn_ref, g_ref[...], pl.program_id(1) * tr, tr, t_len)
    xs = xs_ref[...]
    dff = wd_ref.shape[0]
    for s0, w in _col_chunks(dff):
        a = _conv3_rows(jnp.dot(xs, wu_ref[:, s0:s0 + w], preferred_element_type=F32), cw_ref[:, s0:s0 + w])
        v = _conv3_rows(jnp.dot(xs, wu_ref[:, dff + s0:dff + s0 + w], preferred_element_type=F32),
                        cw_ref[:, dff + s0:dff + s0 + w])
        gate_ref[:, s0:s0 + w] = (_silu(a) * v)[HALO:HALO + tr].astype(BF16)
    o_ref[0] = x_ref[0] + jnp.dot(gate_ref[...], wd_ref[...], preferred_element_type=F32)


def _ffn(h, g, w_up, conv_w, w_down, tr):
    b, t, d = h.shape
    dff = w_down.shape[0]
    main, prev, nxt = _row_halo_specs(tr, d, t)
    return pl.pallas_call(
        functools.partial(_ffn_kernel, tr=tr, t_len=t),
        grid=(b, pl.cdiv(t, tr)),
        in_specs=[main, prev, nxt, _resident((1, d)), _resident((d, 2 * dff)), _resident((3, 2 * dff)),
                  _resident((dff, d))],
        out_specs=pl.BlockSpec((1, tr, d), lambda b_, i: (b_, i, 0)),
        out_shape=jax.ShapeDtypeStruct((b, t, d), F32),
        scratch_shapes=[pltpu.VMEM((tr + 2 * HALO, d), BF16), pltpu.VMEM((tr, dff), BF16)],
        compiler_params=_cp("arbitrary", "arbitrary"),
        name="ffn",
    )(h, h, h, g.reshape(1, d), w_up, conv_w, w_down)


def _final_norm_kernel(x_ref, xn_ref, g_ref, o_ref, *, tr):
    g = g_ref[...]
    o_ref[0, 0:tr - N_META, :] = _rms_rows(x_ref[0], g)[N_META:]
    o_ref[0, tr - N_META:, :] = _rms_rows(xn_ref[0], g)


def _final_norm(x, g, tr):
    b, t, d = x.shape
    assert N_META == HALO
    main, _, nxt = _row_halo_specs(tr, d, t)
    return pl.pallas_call(
        functools.partial(_final_norm_kernel, tr=tr),
        grid=(b, pl.cdiv(t - N_META, tr)),
        in_specs=[main, nxt, pl.BlockSpec((1, d), lambda b_, i: (0, 0))],
        out_specs=pl.BlockSpec((1, tr, d), lambda b_, i: (b_, i, 0)),
        out_shape=jax.ShapeDtypeStruct((b, t - N_META, d), F32),
        compiler_params=_cp("arbitrary", "arbitrary"),
        name="final_norm",
    )(x, x, g.reshape(1, d))


def _prep_w_in(w_in):
    d = w_in.shape[0]
    o_z, o_xbc, o_dt = BRANCH_W, 2 * BRANCH_W, 2 * BRANCH_W + 1024
    ndt = 2 * SSM_GROUPS * SSM_HPG
    o_hy = o_dt + ndt
    o_sc = o_hy + 3 * BRANCH_W
    o_gate = o_sc + 3 * BRANCH_W
    w_dt = w_in[:, o_dt:o_hy].reshape(d, 2, SSM_GROUPS, SSM_HPG)
    w_dt = jnp.transpose(w_dt, (0, 2, 1, 3)).reshape(d, SSM_GROUPS, 2 * SSM_HPG)
    w_dt = jnp.pad(w_dt, ((0, 0), (0, 0), (0, SSM_CHUNK - 2 * SSM_HPG))).reshape(d, SSM_GROUPS * SSM_CHUNK)
    w_conv = jnp.concatenate([w_in[:, o_xbc:o_dt], w_in[:, o_hy:o_sc]], axis=1).astype(BF16)
    w_plain = jnp.concatenate([w_in[:, :o_xbc], w_in[:, o_gate:], w_in[:, o_sc:o_gate], w_dt], axis=1).astype(BF16)
    return w_conv, w_plain


def _hyena_fft_len(t):
    tf = 528
    nfb = -(-(2 * t - 1) // (2 * tf))
    return 2 * tf * nfb, tf


def _run_trunk(x, meta_tokens, norm_final, layers):
    b, seq, d = x.shape
    t = seq + N_META
    meta = jnp.broadcast_to(meta_tokens[None].astype(x.dtype), (b, N_META, d))
    h = jnp.concatenate([meta, x], axis=1).reshape(b * t, d)
    tr = _pick_tile(t, ROW_TILE)
    hy_a, hy_at = _hyena_tables(HYENA_BLOCK)
    hy_nb = -(-t // HYENA_BLOCK)
    tmf = tr
    tok = min(TOKEN_TILE, b * t)
    hv, hx1, hx2 = (CV_HY // BRANCH_W + k for k in range(3))
    for p in layers:
        w_conv, w_plain = p["w_in"]
        taps = jnp.concatenate([p["ssm_conv_w"], p["hyena_conv_w"]], axis=1)
        bias = jnp.concatenate([p["ssm_conv_b"], jnp.zeros((3 * BRANCH_W,), F32)]).reshape(1, N_CONV)
        projc = _conv_proj(h.reshape(b, t, d), p["norm_mix"], w_conv, taps, bias, CV_HY, tr)
        proj = _norm_matmul(h, p["norm_mix"], w_plain, min(512, tok), 768).reshape(b, t, N_PLAIN)
        y_fn = _fnet_branch(proj, tmf)
        y_ssm = _ssd_branch(projc, proj, p["ssm_dt_bias"], p["ssm_a_log"], p["ssm_d"], p["ssm_norm"])
        filt = _hyena_filters(t, hy_nb * HYENA_BLOCK, p["hyena_w1"], p["hyena_b1"], p["hyena_w2"], p["hyena_b2"],
                              p["hyena_w3"], p["hyena_freq"])
        spec = _hy_filter_spectrum(hy_a, filt, HYENA_BLOCK, hy_nb)
        z = _hy_long_conv(hy_a, hy_at, spec, 0, projc, hv, projc, hx1, p["hyena_bias"], HYENA_BLOCK)
        y_hy = _hy_long_conv(hy_a, hy_at, spec, 1, z, 0, projc, hx2, p["hyena_bias"], HYENA_BLOCK)
        y_sc = _sc_branch(proj, p["sc_conv_w"], tr)
        flat = lambda a: a.reshape(b * t, a.shape[-1])
        h = _merge(h, [flat(y_fn), flat(y_ssm), flat(y_hy), flat(y_sc)], flat(proj), p["w_branch"], p["w_out"], min(512, tok))
        h = _ffn(h.reshape(b, t, d), p["norm_ffn"], p["w_up"], p["ffn_conv_w"], p["w_down"], tr).reshape(b * t, d)
    return _final_norm(h.reshape(b, t, d), norm_final, tr)


def kernel(x_prompt, x_sample, meta_tokens, norm_mix, w_in, ssm_conv_w, ssm_conv_b, ssm_dt_bias, ssm_a_log, ssm_d,
           ssm_norm, hyena_conv_w, hyena_w1, hyena_b1, hyena_w2, hyena_b2, hyena_w3, hyena_freq, hyena_bias,
           sc_conv_w, w_branch, w_out, norm_ffn, ffn_conv_w, w_up, w_down, norm_final):
    depth = w_in.shape[0]
    layers = []
    for l in range(depth):
        layers.append(dict(
            norm_mix=norm_mix[l], w_in=_prep_w_in(w_in[l]), ssm_conv_w=ssm_conv_w[l], ssm_conv_b=ssm_conv_b[l],
            ssm_dt_bias=ssm_dt_bias[l], ssm_a_log=ssm_a_log[l], ssm_d=ssm_d[l], ssm_norm=ssm_norm[l],
            hyena_conv_w=hyena_conv_w[l], hyena_w1=hyena_w1[l], hyena_b1=hyena_b1[l], hyena_w2=hyena_w2[l],
            hyena_b2=hyena_b2[l], hyena_w3=hyena_w3[l], hyena_freq=hyena_freq[l], hyena_bias=hyena_bias[l],
            sc_conv_w=sc_conv_w[l], w_branch=w_branch[l].astype(BF16), w_out=w_out[l].astype(BF16),
            norm_ffn=norm_ffn[l], ffn_conv_w=ffn_conv_w[l], w_up=w_up[l].astype(BF16), w_down=w_down[l].astype(BF16)))
    y_prompt = _run_trunk(x_prompt, meta_tokens, norm_final, layers)
    y_sample = _run_trunk(x_sample, meta_tokens, norm_final, layers)
    return (y_prompt, y_sample)
```

```python
import functools
import math

import jax
import jax.numpy as jnp
from jax import lax
from jax.experimental import pallas as pl
from jax.experimental.pallas import tpu as pltpu

F32 = jnp.float32
BF16 = jnp.bfloat16
HI = lax.Precision.HIGHEST

NORM_EPS = 1e-6
N_META = 16
BRANCH_W = 512
FNET_GW = 128
SSM_GROUPS = 2
SSM_HPG = 4
SSM_HEAD_DIM = 64
SSM_STATE = 128
SSM_CHUNK = 128
HYENA_ORDER = 2
HYENA_EMB = 33
HYENA_TARGET = 1e-2
HYENA_FAST = 0.3
HYENA_SLOW = 1.5

ROW_TILE = 700
TOKEN_TILE = 1024
HYENA_BLOCK = 704
HYENA_FREQ_CHUNK = 32
MXU_COLS = 256
SSD_UNROLL = 4
HALO = 16
VMEM_LIMIT = 56 * 1024 * 1024

CV_XBC = 0
CV_HY = 1024
N_CONV = 2560
COL_FN = 0
COL_Z = 512
COL_GATE = 1024
COL_SC = 5120
COL_DT = 6656
N_PLAIN = 6912


def _cp(*sem):
    return pltpu.CompilerParams(dimension_semantics=sem, vmem_limit_bytes=VMEM_LIMIT)


def _pick_tile(n, target, mult=16):
    k = max(1, -(-n // target))
    t = -(-n // k)
    return -(-t // mult) * mult


def _silu(x):
    return x * (1.0 / (1.0 + jnp.exp(-x)))


def _sigmoid(x):
    return 1.0 / (1.0 + jnp.exp(-x))


def _norm_matmul_kernel(x_ref, g_ref, w_ref, o_ref, *, tn):
    x = x_ref[...]
    ms = jnp.mean(x * x, axis=-1, keepdims=True)
    xn = (x * lax.rsqrt(ms + NORM_EPS) * g_ref[...]).astype(BF16)
    for s0 in range(0, w_ref.shape[1], tn):
        o_ref[:, s0:s0 + tn] = jnp.dot(xn, w_ref[:, s0:s0 + tn], preferred_element_type=F32).astype(o_ref.dtype)


def _norm_matmul(x, g, w, tm, tn):
    n, d = x.shape
    m = w.shape[1]
    assert m % tn == 0
    tm = min(tm, -(-n // HALO) * HALO)
    return pl.pallas_call(
        functools.partial(_norm_matmul_kernel, tn=tn),
        grid=(pl.cdiv(n, tm),),
        in_specs=[pl.BlockSpec((tm, d), lambda i: (i, 0)),
                  pl.BlockSpec((1, d), lambda i: (0, 0), pipeline_mode=pl.Buffered(1)),
                  pl.BlockSpec((d, m), lambda i: (0, 0), pipeline_mode=pl.Buffered(1))],
        out_specs=pl.BlockSpec((tm, m), lambda i: (i, 0)),
        out_shape=jax.ShapeDtypeStruct((n, m), BF16),
        compiler_params=_cp("arbitrary"),
        name="norm_matmul",
    )(x, g.reshape(1, d), w)


def _conv3(x, prev_row, next_row, w, row0, t_len):
    n = x.shape[0]
    li = lax.broadcasted_iota(jnp.int32, (n, 1), 0)
    gi = li + row0
    xm = pltpu.roll(x, 1, 0)
    xm = jnp.where(li == 0, prev_row, xm)
    xm = jnp.where(gi >= 1, xm, 0.0)
    xp = pltpu.roll(x, n - 1, 0)
    xp = jnp.where(li == n - 1, next_row, xp)
    xp = jnp.where(gi + 1 < t_len, xp, 0.0)
    return xm * w[0:1] + x * w[1:2] + xp * w[2:3]


def _last_row(halo_ref):
    return halo_ref[0].astype(F32)[HALO - 1:HALO]


def _first_row(halo_ref):
    return halo_ref[0].astype(F32)[0:1]


def _halo_specs(tr, cw, t_len, col_blk):
    nh = tr // HALO
    last = t_len // HALO - 1
    main = pl.BlockSpec((1, tr, cw), lambda b, i: (b, i, col_blk))
    prev = pl.BlockSpec((1, HALO, cw), lambda b, i: (b, jnp.maximum(i * nh - 1, 0), col_blk))
    nxt = pl.BlockSpec((1, HALO, cw), lambda b, i: (b, jnp.minimum((i + 1) * nh, last), col_blk))
    return main, prev, nxt


def _fnet_kernel(a_ref, u_ref, cs_ref, o_ref, *, tmf):
    p = jnp.dot(a_ref[0], u_ref[0], preferred_element_type=F32)
    pc = p[:tmf].astype(BF16)
    ps = p[tmf:].astype(BF16)
    y = jnp.dot(pc, cs_ref[0], preferred_element_type=F32) - jnp.dot(ps, cs_ref[1], preferred_element_type=F32)
    o_ref[0] = y.astype(o_ref.dtype)


def _fnet_tables(t, tmf):
    nm = -(-t // tmf)
    assert tmf % HALO == 0
    k = jnp.arange(t, dtype=jnp.int32)[None, :]
    ang = lambda j: ((j[:, None] * k) % t).astype(F32) * (2.0 * math.pi / t)
    ang_hi = ang(jnp.arange(nm * tmf // HALO, dtype=jnp.int32) * HALO)[:, None, :]
    ang_lo = ang(jnp.arange(HALO, dtype=jnp.int32))[None, :, :]
    c = (jnp.cos(ang_hi) * jnp.cos(ang_lo) - jnp.sin(ang_hi) * jnp.sin(ang_lo)).reshape(nm * tmf, t)
    s = (jnp.sin(ang_hi) * jnp.cos(ang_lo) + jnp.cos(ang_hi) * jnp.sin(ang_lo)).reshape(nm * tmf, t)
    valid = jnp.arange(nm * tmf, dtype=jnp.int32)[:, None] < t
    c = jnp.where(valid, c, 0.0).astype(BF16).reshape(nm, tmf, t)
    s = jnp.where(valid, s, 0.0).astype(BF16).reshape(nm, tmf, t)
    a = jnp.concatenate([c, s], axis=1)
    jj = jnp.arange(BRANCH_W, dtype=jnp.int32)[:, None]
    kk = jnp.arange(BRANCH_W, dtype=jnp.int32)[None, :]
    same = (jj // FNET_GW) == (kk // FNET_GW)
    ang2 = (((jj % FNET_GW) * (kk % FNET_GW)) % FNET_GW).astype(F32) * (2.0 * math.pi / FNET_GW)
    scale = 1.0 / math.sqrt(t * FNET_GW)
    cc = jnp.where(same, jnp.cos(ang2), 0.0) * scale
    sc = jnp.where(same, jnp.sin(ang2), 0.0) * scale
    return a, jnp.stack([cc, sc]).astype(BF16)


def _fnet_branch(proj, tmf):
    b, t, _ = proj.shape
    a, cs = _fnet_tables(t, tmf)
    nm = a.shape[0]
    return pl.pallas_call(
        functools.partial(_fnet_kernel, tmf=tmf),
        grid=(nm, b),
        in_specs=[pl.BlockSpec((1, 2 * tmf, t), lambda i, j: (i, 0, 0)),
                  pl.BlockSpec((1, t, BRANCH_W), lambda i, j: (j, 0, COL_FN // BRANCH_W)),
                  pl.BlockSpec((2, BRANCH_W, BRANCH_W), lambda i, j: (0, 0, 0))],
        out_specs=pl.BlockSpec((1, tmf, BRANCH_W), lambda i, j: (j, i, 0)),
        out_shape=jax.ShapeDtypeStruct((b, t, BRANCH_W), BF16),
        compiler_params=_cp("arbitrary", "arbitrary"),
        name="fnet",
    )(a, proj, cs)


def _hyena_filter_kernel(w1t_ref, w1c_ref, w1s_ref, b1_ref, w2_ref, b2_ref, fq_ref, w3f_ref, w3b_ref, dl_ref,
                         o_ref, hidf_ref, hidb_ref, *, t, half):
    row = lax.broadcasted_iota(jnp.int32, (t, 1), 0)
    pos_f = row.astype(F32)
    pos_b = (t - row).astype(F32)

    def hidden(pos):
        tt = pos * (1.0 / (t - 1))
        wv = pos * (2.0 * math.pi / t)
        bands = (HYENA_EMB - 1) // 2
        fi = lax.broadcasted_iota(jnp.int32, (1, bands), 1).astype(F32)
        fr = 1e-4 + fi * ((bands - 1 - 1e-4) / (bands - 1))
        arg = wv * fr
        pre = (tt * w1t_ref[...]
               + jnp.dot(jnp.cos(arg), w1c_ref[...], precision=HI, preferred_element_type=F32)
               - jnp.dot(jnp.sin(arg), w1s_ref[...], precision=HI, preferred_element_type=F32)
               + b1_ref[...])
        fq = fq_ref[...]
        h1 = jnp.sin(fq * pre)
        return jnp.sin(fq * (jnp.dot(h1, w2_ref[...], precision=HI, preferred_element_type=F32) + b2_ref[...]))

    @pl.when(pl.program_id(0) == 0)
    def _():
        hidf_ref[...] = hidden(pos_f)
        hidb_ref[...] = hidden(pos_b)

    def taps(hid_ref, w3_ref, pos):
        dec = jnp.exp(-(pos * (1.0 / (t - 1))) * dl_ref[...])
        return jnp.dot(hid_ref[...], w3_ref[...], precision=HI, preferred_element_type=F32) * dec

    hf = taps(hidf_ref, w3f_ref, pos_f)
    hb = jnp.where(row >= 1, taps(hidb_ref, w3b_ref, pos_b), 0.0)
    l1 = jnp.sum(jnp.abs(hf), axis=0, keepdims=True) + jnp.sum(jnp.abs(hb), axis=0, keepdims=True)
    inv = 1.0 / l1
    cw = o_ref.shape[2]
    o_ref[0, half - t:half, :] = (hb * inv).astype(o_ref.dtype)
    o_ref[0, half:half + t, :] = (hf * inv).astype(o_ref.dtype)
    if half > t:
        o_ref[0, 0:half - t, :] = jnp.zeros((half - t, cw), o_ref.dtype)
        o_ref[0, half + t:, :] = jnp.zeros((half - t, cw), o_ref.dtype)


def _hyena_filters(t, half, w1, b1, w2, b2, w3, freq):
    cw = 256
    nb = BRANCH_W // cw
    bands = (HYENA_EMB - 1) // 2
    nf = w2.shape[0]
    max_decay = math.log(HYENA_TARGET) / HYENA_FAST
    min_decay = math.log(HYENA_TARGET) / HYENA_SLOW
    deltas = jnp.abs(jnp.linspace(min_decay, max_decay, BRANCH_W, dtype=F32)).reshape(1, BRANCH_W)
    full = lambda shape: pl.BlockSpec(shape, lambda g: (0,) * len(shape))
    return pl.pallas_call(
        functools.partial(_hyena_filter_kernel, t=t, half=half),
        grid=(HYENA_ORDER * nb,),
        in_specs=[full((1, nf)), full((bands, nf)), full((bands, nf)), full((1, nf)), full((nf, nf)),
                  full((1, nf)), full((1, nf)),
                  pl.BlockSpec((nf, cw), lambda g: (0, (g // nb) * 2 * nb + g % nb)),
                  pl.BlockSpec((nf, cw), lambda g: (0, (g // nb) * 2 * nb + nb + g % nb)),
                  pl.BlockSpec((1, cw), lambda g: (0, g % nb))],
        out_specs=pl.BlockSpec((1, 2 * half, cw), lambda g: (g // nb, 0, g % nb)),
        out_shape=jax.ShapeDtypeStruct((HYENA_ORDER, 2 * half, BRANCH_W), BF16),
        scratch_shapes=[pltpu.VMEM((t, nf), F32), pltpu.VMEM((t, nf), F32)],
        compiler_params=_cp("arbitrary"),
        name="hyena_filter",
    )(w1[0:1], w1[1:1 + bands], w1[1 + bands:], b1.reshape(1, nf), w2, b2.reshape(1, nf), freq.reshape(1, nf),
      w3, w3, deltas)


def _hyena_tables(p_len):
    f = jnp.arange(p_len, dtype=jnp.int32)[:, None]
    n = jnp.arange(p_len, dtype=jnp.int32)[None, :]
    ang = (((2 * f + 1) * n) % (4 * p_len)).astype(F32) * (math.pi / (2 * p_len))
    a = jnp.concatenate([jnp.cos(ang), -jnp.sin(ang)], axis=0).astype(BF16)
    return a, a.T


def _hy_spec_kernel(a_ref, seg_ref, prev_ref, o_ref, *, p_len):
    g = jnp.dot(a_ref[...], seg_ref[0, 0], preferred_element_type=F32)
    gp = jnp.dot(a_ref[...], prev_ref[0, 0], preferred_element_type=F32)
    first = prev_ref[0, 0, 0:1, :].astype(F32)
    f = lax.broadcasted_iota(jnp.int32, (p_len, 1), 0)
    sign = jnp.where(f % 2 == 0, 1.0, -1.0)
    scale = 1.0 / p_len
    o_ref[0, 0, :p_len] = (g[:p_len] - sign * gp[p_len:]) * scale
    o_ref[0, 0, p_len:] = (g[p_len:] + sign * (gp[:p_len] - first)) * scale


def _hy_filter_spectrum(a, filt, p_len, nb):
    order, _, w = filt.shape
    lags = filt.reshape(order, 2 * nb, p_len, w)
    return pl.pallas_call(
        functools.partial(_hy_spec_kernel, p_len=p_len),
        grid=(order, 2 * nb - 1),
        in_specs=[pl.BlockSpec((2 * p_len, p_len), lambda o, d: (0, 0)),
                  pl.BlockSpec((1, 1, p_len, w), lambda o, d: (o, d + 1, 0, 0)),
                  pl.BlockSpec((1, 1, p_len, w), lambda o, d: (o, d, 0, 0))],
        out_specs=pl.BlockSpec((1, 1, 2 * p_len, w), lambda o, d: (o, d, 0, 0)),
        out_shape=jax.ShapeDtypeStruct((order, 2 * nb - 1, 2 * p_len, w), F32),
        compiler_params=_cp("arbitrary", "arbitrary"),
        name="hyena_filter_spectrum",
    )(a, lags, lags)


def _hy_conv_kernel(a_ref, at_ref, k_ref, v_ref, g_ref, bias_ref, o_ref, u_scr, y_scr, *, t_len, p_len, nb, fc):
    cw = v_ref.shape[2]

    def block_rows(i):
        return i * p_len, min(p_len, t_len - i * p_len)

    for j in range(nb):
        r0, n = block_rows(j)
        u = v_ref[0, r0:r0 + n, :]
        if n < p_len:
            u = jnp.concatenate([u, jnp.zeros((p_len - n, cw), u.dtype)], axis=0)
        u_scr[j] = jnp.dot(a_ref[...], u, preferred_element_type=F32)

    for i in range(nb):
        for r in range(0, p_len, fc):
            acc_r = jnp.zeros((fc, cw), F32)
            acc_i = jnp.zeros((fc, cw), F32)
            for j in range(nb):
                d = i - j + nb - 1
                ur, ui = u_scr[j, r:r + fc, :], u_scr[j, p_len + r:p_len + r + fc, :]
                kr, ki = k_ref[0, d, r:r + fc, :], k_ref[0, d, p_len + r:p_len + r + fc, :]
                acc_r = acc_r + ur * kr - ui * ki
                acc_i = acc_i + ur * ki + ui * kr
            y_scr[i % 2, r:r + fc, :] = acc_r.astype(BF16)
            y_scr[i % 2, p_len + r:p_len + r + fc, :] = acc_i.astype(BF16)
        conv = jnp.dot(at_ref[...], y_scr[i % 2], preferred_element_type=F32)
        r0, n = block_rows(i)
        v = v_ref[0, r0:r0 + n, :].astype(F32)
        o_ref[0, r0:r0 + n, :] = (g_ref[0, r0:r0 + n, :].astype(F32) * (conv[:n] + v * bias_ref[0])).astype(o_ref.dtype)


def _hy_long_conv(a, at, spec, order, v, v_col, gate, gate_col, bias, p_len):
    b, t, _ = v.shape
    nb = -(-t // p_len)
    cw = BRANCH_W // 2
    nh = BRANCH_W // cw
    return pl.pallas_call(
        functools.partial(_hy_conv_kernel, t_len=t, p_len=p_len, nb=nb, fc=HYENA_FREQ_CHUNK),
        grid=(nh, b),
        in_specs=[pl.BlockSpec((2 * p_len, p_len), lambda c, i: (0, 0), pipeline_mode=pl.Buffered(1)),
                  pl.BlockSpec((p_len, 2 * p_len), lambda c, i: (0, 0), pipeline_mode=pl.Buffered(1)),
                  pl.BlockSpec((1, 2 * nb - 1, 2 * p_len, cw), lambda c, i: (order, 0, 0, c),
                               pipeline_mode=pl.Buffered(1)),
                  pl.BlockSpec((1, t, cw), lambda c, i: (i, 0, v_col * nh + c)),
                  pl.BlockSpec((1, t, cw), lambda c, i: (i, 0, gate_col * nh + c)),
                  pl.BlockSpec((1, 1, cw), lambda c, i: (order, 0, c))],
        out_specs=pl.BlockSpec((1, t, cw), lambda c, i: (i, 0, c)),
        out_shape=jax.ShapeDtypeStruct((b, t, BRANCH_W), BF16),
        scratch_shapes=[pltpu.VMEM((nb, 2 * p_len, cw), F32), pltpu.VMEM((2, 2 * p_len, cw), BF16)],
        compiler_params=_cp("arbitrary", "arbitrary"),
        name="hyena_conv",
    )(a, at, spec, v, gate, bias.reshape(HYENA_ORDER, 1, BRANCH_W))


def _softplus(x):
    return jnp.maximum(x, 0.0) + jnp.log(1.0 + jnp.exp(-jnp.abs(x)))


def _ssd_kernel(xs_ref, b_ref, c_ref, z_ref, dt_ref, dtb_ref, an_ref, dsk_ref, nw_ref, tri_ref, exp_ref, o_ref,
                yf_scr, yb_scr, *, t_len, nc, unroll):
    q = SSM_CHUNK
    gw = SSM_HPG * SSM_HEAD_DIM
    front = q - N_META
    ri = lax.broadcasted_iota(jnp.int32, (q, q), 0)
    ci = lax.broadcasted_iota(jnp.int32, (q, q), 1)
    tri = (ri >= ci, ri <= ci)
    head_of_lane = lax.broadcasted_iota(jnp.int32, (1, gw), 1) // SSM_HEAD_DIM
    lane = lax.broadcasted_iota(jnp.int32, (1, q), 1)
    dt_lane_mask = (lane < 2 * SSM_HPG).astype(F32)
    low_half = lane < SSM_HEAD_DIM
    dtb = dtb_ref[...]
    an = an_ref[...]
    masked_out = -1e30

    def widen(cols):
        return jnp.concatenate([jnp.where(low_half, cols[0], cols[1]), jnp.where(low_half, cols[2], cols[3])], axis=1)

    def cumsum(d, x):
        hi = x.astype(BF16)
        r1 = x - hi.astype(F32)
        mid = r1.astype(BF16)
        lo = (r1 - mid.astype(F32)).astype(BF16)
        s = jnp.dot(tri_ref[d], jnp.concatenate([hi, mid, lo], axis=1), preferred_element_type=F32)
        return s[:, :q] + s[:, q:2 * q] + s[:, 2 * q:]

    def load(ref, r0, nrows):
        return ref[0, pl.ds(r0, nrows), :]

    def dt_of(raw):
        return _softplus(raw.astype(F32) + dtb) * dt_lane_mask

    def chunk_data(c):
        r0 = pl.multiple_of(c * q - front, HALO)
        return load(xs_ref, r0, q), load(b_ref, r0, q), load(c_ref, r0, q), dt_of(load(dt_ref, r0, q))

    def chunk0_data():
        def pad(x):
            return jnp.concatenate([jnp.zeros((front, x.shape[1]), x.dtype), x], axis=0)
        return (pad(load(xs_ref, 0, N_META)), pad(load(b_ref, 0, N_META)), pad(load(c_ref, 0, N_META)),
                pad(dt_of(load(dt_ref, 0, N_META))))

    def stage_local(job, cs_n):
        (xs, bm, cm, dt), d = job
        lanes = [d * SSM_HPG + r for r in range(SSM_HPG)]
        cs_cols = [jnp.broadcast_to(cs_n[:, ln:ln + 1], (q, q)) for ln in lanes]
        cs_w = widen(cs_cols)
        dt_w = jnp.dot(dt.astype(BF16), exp_ref[d], preferred_element_type=F32).astype(BF16)
        tot = cs_w[q - 1:q, :] if d == 0 else cs_w[0:1, :]
        xt = xs * dt_w
        xd = xt * jnp.exp2(tot - cs_w).astype(BF16)
        s_new = lax.dot_general(bm, xd, (((0,), (0,)), ((), ())), preferred_element_type=F32)
        cb = lax.dot_general(cm, bm, (((1,), (1,)), ((), ())), preferred_element_type=F32)
        return cs_cols, xt, cb, cm, s_new, jnp.exp2(cs_w), jnp.exp2(tot), cs_n.T

    def stage_carry(loc, h):
        _, _, _, cm, s_new, ecs, etot, _ = loc
        y_off = jnp.dot(cm, h.astype(BF16), preferred_element_type=F32) * ecs
        return y_off, h * etot + s_new

    def stage_diag(job, cs_n, loc, y_off):
        d = job[1]
        cs_cols, xt, cb = loc[:3]
        cs_t = loc[7]
        ms, xm = [], []
        for r in range(SSM_HPG):
            ln = d * SSM_HPG + r
            dec = jnp.exp2(jnp.where(tri[d], cs_cols[r] - cs_t[ln:ln + 1, :], masked_out))
            ms.append((cb * dec).astype(BF16))
            xm.append(jnp.where(head_of_lane == r, xt, jnp.zeros_like(xt)))
        return y_off + jnp.dot(jnp.concatenate(ms, axis=1), jnp.concatenate(xm, axis=0), preferred_element_type=F32)

    def process_all(fwd_data, bwd_data, hf, hb):
        jobs = [(x, 0) for x in fwd_data] + [(x, 1) for x in bwd_data]
        cs = [cumsum(d, data[3] * an) for data, d in jobs]
        ys, pending = [], None
        for k, (job, c) in enumerate(zip(jobs, cs)):
            loc = stage_local(job, c)
            if pending is not None:
                ys.append(stage_diag(*pending))
            y_off, h = stage_carry(loc, hf if job[1] == 0 else hb)
            hf, hb = (h, hb) if job[1] == 0 else (hf, h)
            pending = (job, c, loc, y_off)
        ys.append(stage_diag(*pending))
        return ys[:len(fwd_data)], ys[len(fwd_data):], hf, hb

    def finish(y, r_out, nrows):
        y = y + load(xs_ref, r_out, nrows).astype(F32) * dsk_ref[...]
        y = y * _silu(load(z_ref, r_out, nrows).astype(F32))
        y = y * lax.rsqrt(jnp.mean(y * y, axis=-1, keepdims=True) + NORM_EPS)
        o_ref[0, pl.ds(r_out, nrows), :] = (y * nw_ref[...]).astype(o_ref.dtype)

    def rows(c):
        return pl.ds(pl.multiple_of(c * q, q), q)

    h0 = jnp.zeros((SSM_STATE, gw), F32)
    (y,), _, hf, _ = process_all([chunk0_data()], [], h0, h0)
    yf_scr[0:q, :] = y

    def scan_body(i, carry):
        hf, hb = carry
        cf = [1 + unroll * i + k for k in range(unroll)]
        cb_ = [nc - c for c in cf]
        yf, yb, hf, hb = process_all([chunk_data(c) for c in cf], [chunk_data(c) for c in cb_], hf, hb)
        for c, y in zip(cf, yf):
            yf_scr[rows(c), :] = y
        for c, y in zip(cb_, yb):
            yb_scr[rows(c), :] = y
        return hf, hb

    _, hb = lax.fori_loop(0, (nc - 1) // unroll, scan_body, (hf, h0))
    _, (y,), _, _ = process_all([], [chunk0_data()], h0, hb)
    yb_scr[0:q, :] = y

    finish(yf_scr[front:q, :] + yb_scr[front:q, :], 0, N_META)

    def finish_body(c, carry):
        finish(yf_scr[rows(c), :] + yb_scr[rows(c), :], pl.multiple_of(c * q - front, HALO), q)
        return carry

    lax.fori_loop(1, nc, finish_body, 0)


def _ssd_branch(xbc, proj, dt_bias, a_log, d_skip, norm_w):
    b, t, _ = xbc.shape
    q = SSM_CHUNK
    gw = SSM_HPG * SSM_HEAD_DIM
    nc = (q - N_META + t) // q
    assert nc * q == q - N_META + t, "sequence length minus meta tokens must be a multiple of the SSD chunk"
    a = -jnp.exp(a_log.astype(F32)).reshape(2, SSM_GROUPS, SSM_HPG)
    dtb = dt_bias.astype(F32).reshape(2, SSM_GROUPS, SSM_HPG)
    pad = lambda v: jnp.pad(jnp.transpose(v, (1, 0, 2)).reshape(SSM_GROUPS, 1, 2 * SSM_HPG),
                            ((0, 0), (0, 0), (0, q - 2 * SSM_HPG)))
    a_n = pad(a * math.log2(math.e))
    dtb_n = pad(dtb)
    ri = jnp.arange(q, dtype=jnp.int32)[:, None]
    ci = jnp.arange(q, dtype=jnp.int32)[None, :]
    tri = jnp.stack([ri >= ci, ri <= ci]).astype(BF16)
    cj = jnp.arange(gw, dtype=jnp.int32)[None, :] // SSM_HEAD_DIM
    expand = jnp.stack([ri == cj, ri == cj + SSM_HPG]).astype(BF16)
    dsk = jnp.repeat(d_skip.astype(F32).reshape(SSM_GROUPS, 1, SSM_HPG), SSM_HEAD_DIM, axis=-1)
    nw = norm_w.astype(F32).reshape(SSM_GROUPS, 1, gw)
    par = lambda shape: pl.BlockSpec((None,) + shape, lambda i, g: (g, 0, 0))
    return pl.pallas_call(
        functools.partial(_ssd_kernel, t_len=t, nc=nc, unroll=math.gcd(nc - 1, SSD_UNROLL)),
        grid=(b, SSM_GROUPS),
        in_specs=[pl.BlockSpec((1, t, gw), lambda i, g: (i, 0, g)),
                  pl.BlockSpec((1, t, SSM_STATE), lambda i, g: (i, 0, 4 + g)),
                  pl.BlockSpec((1, t, SSM_STATE), lambda i, g: (i, 0, 6 + g)),
                  pl.BlockSpec((1, t, gw), lambda i, g: (i, 0, COL_Z // gw + g)),
                  pl.BlockSpec((1, t, q), lambda i, g: (i, 0, COL_DT // q + g)),
                  par((1, q)), par((1, q)), par((1, gw)), par((1, gw)),
                  pl.BlockSpec((2, q, q), lambda i, g: (0, 0, 0)),
                  pl.BlockSpec((2, q, gw), lambda i, g: (0, 0, 0))],
        out_specs=pl.BlockSpec((1, t, gw), lambda i, g: (i, 0, g)),
        out_shape=jax.ShapeDtypeStruct((b, t, BRANCH_W), BF16),
        scratch_shapes=[pltpu.VMEM((nc * q, gw), F32), pltpu.VMEM((nc * q, gw), F32)],
        compiler_params=_cp("arbitrary", "arbitrary"),
        name="ssd",
    )(xbc, xbc, xbc, proj, proj, dtb_n, a_n, dsk, nw, tri, expand)


def _merge_kernel(h_ref, y0_ref, y1_ref, y2_ref, g0_ref, g1_ref, g2_ref, g3_ref, bg_ref, cg_ref, xi_ref, cgp_ref,
                  xip_ref, cgn_ref, xin_ref, scw_ref, wb_ref, wo_ref, o_ref, *, tr, t_len):
    u = cg_ref[0].astype(F32) * xi_ref[0].astype(F32)
    prev = _last_row(cgp_ref) * _last_row(xip_ref)
    nxt = _first_row(cgn_ref) * _first_row(xin_ref)
    y_sc = (bg_ref[0].astype(F32) * _conv3(u, prev, nxt, scw_ref[...], pl.program_id(1) * tr, t_len)).astype(BF16)
    merged = None
    for k, (y, g_ref) in enumerate(zip((y0_ref[0], y1_ref[0], y2_ref[0], y_sc), (g0_ref, g1_ref, g2_ref, g3_ref))):
        term = _sigmoid(g_ref[0].astype(F32)) * jnp.dot(y, wb_ref[k], preferred_element_type=F32)
        merged = term if merged is None else merged + term
    o_ref[0] = h_ref[0] + jnp.dot(merged.astype(BF16), wo_ref[...], preferred_element_type=F32)


def _merge(h, ys, proj, sc_w, wb, wo, tr):
    b, t, d = h.shape
    nb = wb.shape[0]
    row = lambda w, c: pl.BlockSpec((1, tr, w), lambda b_, i: (b_, i, c))
    c0 = COL_SC // BRANCH_W
    bg, _, _ = _halo_specs(tr, BRANCH_W, t, c0)
    cg, cgp, cgn = _halo_specs(tr, BRANCH_W, t, c0 + 1)
    xi, xip, xin = _halo_specs(tr, BRANCH_W, t, c0 + 2)
    return pl.pallas_call(
        functools.partial(_merge_kernel, tr=tr, t_len=t),
        grid=(b, pl.cdiv(t, tr)),
        in_specs=[row(d, 0)] + [row(BRANCH_W, 0)] * len(ys) + [row(d, COL_GATE // d + k) for k in range(nb)]
                 + [bg, cg, xi, cgp, xip, cgn, xin, _resident((3, BRANCH_W)), _resident((nb, BRANCH_W, d)),
                    _resident((d, d))],
        out_specs=row(d, 0),
        out_shape=jax.ShapeDtypeStruct((b, t, d), F32),
        compiler_params=_cp("arbitrary", "arbitrary"),
        name="merge",
    )(h, *ys, *([proj] * (nb + 7)), sc_w, wb, wo)


def _rms_rows(x, g):
    return x * lax.rsqrt(jnp.mean(x * x, axis=-1, keepdims=True) + NORM_EPS) * g


def _stage_normed_rows(xs_ref, x_ref, xp_ref, xn_ref, g, row0, tr, t_len):
    li = lax.broadcasted_iota(jnp.int32, (tr, 1), 0)
    xs_ref[HALO:HALO + tr, :] = jnp.where(row0 + li < t_len, _rms_rows(x_ref[0], g), 0.0).astype(BF16)
    xs_ref[0:HALO, :] = jnp.where(row0 > 0, _rms_rows(xp_ref[0], g), 0.0).astype(BF16)
    xs_ref[HALO + tr:, :] = jnp.where(row0 + tr < t_len, _rms_rows(xn_ref[0], g), 0.0).astype(BF16)


def _conv3_rows(u, w):
    n = u.shape[0]
    return pltpu.roll(u, 1, 0) * w[0:1] + u * w[1:2] + pltpu.roll(u, n - 1, 0) * w[2:3]


def _row_halo_specs(tr, d, t_len):
    nh = tr // HALO
    last = t_len // HALO - 1
    return (pl.BlockSpec((1, tr, d), lambda b, i: (b, i, 0)),
            pl.BlockSpec((1, HALO, d), lambda b, i: (b, jnp.maximum(i * nh - 1, 0), 0)),
            pl.BlockSpec((1, HALO, d), lambda b, i: (b, jnp.minimum((i + 1) * nh, last), 0)))


def _resident(shape):
    return pl.BlockSpec(shape, lambda b, i: (0,) * len(shape), pipeline_mode=pl.Buffered(1))


def _col_chunks(n):
    return tuple((s0, min(MXU_COLS, n - s0)) for s0 in range(0, n, MXU_COLS))


def _conv_proj_kernel(x_ref, xp_ref, xn_ref, g_ref, w_ref, c_ref, b_ref, o_ref, xs_ref, *, tr, t_len, n_act):
    _stage_normed_rows(xs_ref, x_ref, xp_ref, xn_ref, g_ref[...], pl.program_id(1) * tr, tr, t_len)
    xs = xs_ref[...]
    for s0, w in _col_chunks(w_ref.shape[1]):
        u = jnp.dot(xs, w_ref[:, s0:s0 + w], preferred_element_type=F32)
        y = _conv3_rows(u, c_ref[:, s0:s0 + w])[HALO:HALO + tr] + b_ref[:, s0:s0 + w]
        o_ref[0, :, s0:s0 + w] = (_silu(y) if s0 < n_act else y).astype(o_ref.dtype)


def _conv_proj(h, g, w, taps, bias, n_act, tr):
    b, t, d = h.shape
    m = w.shape[1]
    main, prev, nxt = _row_halo_specs(tr, d, t)
    return pl.pallas_call(
        functools.partial(_conv_proj_kernel, tr=tr, t_len=t, n_act=n_act),
        grid=(b, pl.cdiv(t, tr)),
        in_specs=[main, prev, nxt, _resident((1, d)), _resident((d, m)), _resident((3, m)), _resident((1, m))],
        out_specs=pl.BlockSpec((1, tr, m), lambda b_, i: (b_, i, 0)),
        out_shape=jax.ShapeDtypeStruct((b, t, m), BF16),
        scratch_shapes=[pltpu.VMEM((tr + 2 * HALO, d), BF16)],
        compiler_params=_cp("arbitrary", "arbitrary"),
        name="conv_proj",
    )(h, h, h, g.reshape(1, d), w, taps, bias)


def _ffn_kernel(x_ref, xp_ref, xn_ref, g_ref, wu_ref, cw_ref, wd_ref, o_ref, xs_ref, gate_ref, *, tr, t_len):
    _stage_normed_rows(xs_ref, x_ref, xp_ref, xn_ref, g_ref[...], pl.program_id(1) * tr, tr, t_len)
    xs = xs_ref[...]
    dff = wd_ref.shape[0]
    for s0, w in _col_chunks(dff):
        a = _conv3_rows(jnp.dot(xs, wu_ref[:, s0:s0 + w], preferred_element_type=F32), cw_ref[:, s0:s0 + w])
        v = _conv3_rows(jnp.dot(xs, wu_ref[:, dff + s0:dff + s0 + w], preferred_element_type=F32),
                        cw_ref[:, dff + s0:dff + s0 + w])
        gate_ref[:, s0:s0 + w] = (_silu(a) * v)[HALO:HALO + tr].astype(BF16)
    o_ref[0] = x_ref[0] + jnp.dot(gate_ref[...], wd_ref[...], preferred_element_type=F32)


def _ffn(h, g, w_up, conv_w, w_down, tr):
    b, t, d = h.shape
    dff = w_down.shape[0]
    main, prev, nxt = _row_halo_specs(tr, d, t)
    return pl.pallas_call(
        functools.partial(_ffn_kernel, tr=tr, t_len=t),
        grid=(b, pl.cdiv(t, tr)),
        in_specs=[main, prev, nxt, _resident((1, d)), _resident((d, 2 * dff)), _resident((3, 2 * dff)),
                  _resident((dff, d))],
        out_specs=pl.BlockSpec((1, tr, d), lambda b_, i: (b_, i, 0)),
        out_shape=jax.ShapeDtypeStruct((b, t, d), F32),
        scratch_shapes=[pltpu.VMEM((tr + 2 * HALO, d), BF16), pltpu.VMEM((tr, dff), BF16)],
        compiler_params=_cp("arbitrary", "arbitrary"),
        name="ffn",
    )(h, h, h, g.reshape(1, d), w_up, conv_w, w_down)


def _final_norm_kernel(x_ref, xn_ref, g_ref, o_ref, *, tr):
    g = g_ref[...]
    o_ref[0, 0:tr - N_META, :] = _rms_rows(x_ref[0], g)[N_META:]
    o_ref[0, tr - N_META:, :] = _rms_rows(xn_ref[0], g)


def _final_norm(x, g, tr):
    b, t, d = x.shape
    assert N_META == HALO
    main, _, nxt = _row_halo_specs(tr, d, t)
    return pl.pallas_call(
        functools.partial(_final_norm_kernel, tr=tr),
        grid=(b, pl.cdiv(t - N_META, tr)),
        in_specs=[main, nxt, pl.BlockSpec((1, d), lambda b_, i: (0, 0))],
        out_specs=pl.BlockSpec((1, tr, d), lambda b_, i: (b_, i, 0)),
        out_shape=jax.ShapeDtypeStruct((b, t - N_META, d), F32),
        compiler_params=_cp("arbitrary", "arbitrary"),
        name="final_norm",
    )(x, x, g.reshape(1, d))


def _prep_w_in(w_in):
    d = w_in.shape[0]
    o_z, o_xbc, o_dt = BRANCH_W, 2 * BRANCH_W, 2 * BRANCH_W + 1024
    ndt = 2 * SSM_GROUPS * SSM_HPG
    o_hy = o_dt + ndt
    o_sc = o_hy + 3 * BRANCH_W
    o_gate = o_sc + 3 * BRANCH_W
    w_dt = w_in[:, o_dt:o_hy].reshape(d, 2, SSM_GROUPS, SSM_HPG)
    w_dt = jnp.transpose(w_dt, (0, 2, 1, 3)).reshape(d, SSM_GROUPS, 2 * SSM_HPG)
    w_dt = jnp.pad(w_dt, ((0, 0), (0, 0), (0, SSM_CHUNK - 2 * SSM_HPG))).reshape(d, SSM_GROUPS * SSM_CHUNK)
    w_conv = jnp.concatenate([w_in[:, o_xbc:o_dt], w_in[:, o_hy:o_sc]], axis=1).astype(BF16)
    w_plain = jnp.concatenate([w_in[:, :o_xbc], w_in[:, o_gate:], w_in[:, o_sc:o_gate], w_dt], axis=1).astype(BF16)
    return w_conv, w_plain


def _hyena_fft_len(t):
    tf = 528
    nfb = -(-(2 * t - 1) // (2 * tf))
    return 2 * tf * nfb, tf


def _run_trunk(x, meta_tokens, norm_final, layers):
    b, seq, d = x.shape
    t = seq + N_META
    meta = jnp.broadcast_to(meta_tokens[None].astype(x.dtype), (b, N_META, d))
    h = jnp.concatenate([meta, x], axis=1)
    tr = _pick_tile(t, ROW_TILE)
    hy_a, hy_at = _hyena_tables(HYENA_BLOCK)
    hy_nb = -(-t // HYENA_BLOCK)
    tmf = tr
    tok = min(TOKEN_TILE, b * t)
    hv, hx1, hx2 = (CV_HY // BRANCH_W + k for k in range(3))
    for p in layers:
        w_conv, w_plain = p["w_in"]
        taps = jnp.concatenate([p["ssm_conv_w"], p["hyena_conv_w"]], axis=1)
        bias = jnp.concatenate([p["ssm_conv_b"], jnp.zeros((3 * BRANCH_W,), F32)]).reshape(1, N_CONV)
        projc = _conv_proj(h, p["norm_mix"], w_conv, taps, bias, CV_HY, tr)
        proj = _norm_matmul(h.reshape(b * t, d), p["norm_mix"], w_plain, min(512, tok), 768).reshape(b, t, N_PLAIN)
        y_fn = _fnet_branch(proj, tmf)
        y_ssm = _ssd_branch(projc, proj, p["ssm_dt_bias"], p["ssm_a_log"], p["ssm_d"], p["ssm_norm"])
        filt = _hyena_filters(t, hy_nb * HYENA_BLOCK, p["hyena_w1"], p["hyena_b1"], p["hyena_w2"], p["hyena_b2"],
                              p["hyena_w3"], p["hyena_freq"])
        spec = _hy_filter_spectrum(hy_a, filt, HYENA_BLOCK, hy_nb)
        z = _hy_long_conv(hy_a, hy_at, spec, 0, projc, hv, projc, hx1, p["hyena_bias"], HYENA_BLOCK)
        y_hy = _hy_long_conv(hy_a, hy_at, spec, 1, z, 0, projc, hx2, p["hyena_bias"], HYENA_BLOCK)
        h = _merge(h, [y_fn, y_ssm, y_hy], proj, p["sc_conv_w"], p["w_branch"], p["w_out"], tr)
        h = _ffn(h, p["norm_ffn"], p["w_up"], p["ffn_conv_w"], p["w_down"], tr)
    return _final_norm(h, norm_final, tr)


def kernel(x_prompt, x_sample, meta_tokens, norm_mix, w_in, ssm_conv_w, ssm_conv_b, ssm_dt_bias, ssm_a_log, ssm_d,
           ssm_norm, hyena_conv_w, hyena_w1, hyena_b1, hyena_w2, hyena_b2, hyena_w3, hyena_freq, hyena_bias,
           sc_conv_w, w_branch, w_out, norm_ffn, ffn_conv_w, w_up, w_down, norm_final):
    depth = w_in.shape[0]
    layers = []
    for l in range(depth):
        layers.append(dict(
            norm_mix=norm_mix[l], w_in=_prep_w_in(w_in[l]), ssm_conv_w=ssm_conv_w[l], ssm_conv_b=ssm_conv_b[l],
            ssm_dt_bias=ssm_dt_bias[l], ssm_a_log=ssm_a_log[l], ssm_d=ssm_d[l], ssm_norm=ssm_norm[l],
            hyena_conv_w=hyena_conv_w[l], hyena_w1=hyena_w1[l], hyena_b1=hyena_b1[l], hyena_w2=hyena_w2[l],
            hyena_b2=hyena_b2[l], hyena_w3=hyena_w3[l], hyena_freq=hyena_freq[l], hyena_bias=hyena_bias[l],
            sc_conv_w=sc_conv_w[l], w_branch=w_branch[l].astype(BF16), w_out=w_out[l].astype(BF16),
            norm_ffn=norm_ffn[l], ffn_conv_w=ffn_conv_w[l], w_up=w_up[l].astype(BF16), w_down=w_down[l].astype(BF16)))
    y_prompt = _run_trunk(x_prompt, meta_tokens, norm_final, layers)
    y_sample = _run_trunk(x_sample, meta_tokens, norm_final, layers)
    return (y_prompt, y_sample)
```

```python
import functools
import math

import jax
import jax.numpy as jnp
from jax import lax
from jax.experimental import pallas as pl
from jax.experimental.pallas import tpu as pltpu

F32 = jnp.float32
BF16 = jnp.bfloat16
HI = lax.Precision.HIGHEST

NORM_EPS = 1e-6
N_META = 16
BRANCH_W = 512
FNET_GW = 128
SSM_GROUPS = 2
SSM_HPG = 4
SSM_HEAD_DIM = 64
SSM_STATE = 128
SSM_CHUNK = 128
HYENA_ORDER = 2
HYENA_EMB = 33
HYENA_TARGET = 1e-2
HYENA_FAST = 0.3
HYENA_SLOW = 1.5

ROW_TILE = 700
TOKEN_TILE = 1024
HYENA_BLOCK = 704
HYENA_FREQ_CHUNK = 32
MXU_COLS = 256
SSD_UNROLL = 4
HALO = 16
VMEM_LIMIT = 56 * 1024 * 1024

CV_XBC = 0
CV_HY = 1024
N_CONV = 2560
COL_FN = 0
COL_Z = 512
COL_GATE = 1024
COL_SC = 5120
COL_DT = 6656
N_PLAIN = 6912


def _cp(*sem):
    return pltpu.CompilerParams(dimension_semantics=sem, vmem_limit_bytes=VMEM_LIMIT)


def _pick_tile(n, target, mult=16):
    k = max(1, -(-n // target))
    t = -(-n // k)
    return -(-t // mult) * mult


def _silu(x):
    h = 0.5 * x
    return h * (1.0 + jnp.tanh(h))


def _norm_matmul_kernel(x_ref, g_ref, w_ref, o_ref, *, tn):
    x = x_ref[...]
    ms = jnp.mean(x * x, axis=-1, keepdims=True)
    xn = (x * lax.rsqrt(ms + NORM_EPS) * g_ref[...]).astype(BF16)
    for s0 in range(0, w_ref.shape[1], tn):
        o_ref[:, s0:s0 + tn] = jnp.dot(xn, w_ref[:, s0:s0 + tn], preferred_element_type=F32).astype(o_ref.dtype)


def _norm_matmul(x, g, w, tm, tn):
    n, d = x.shape
    m = w.shape[1]
    assert m % tn == 0
    tm = min(tm, -(-n // HALO) * HALO)
    return pl.pallas_call(
        functools.partial(_norm_matmul_kernel, tn=tn),
        grid=(pl.cdiv(n, tm),),
        in_specs=[pl.BlockSpec((tm, d), lambda i: (i, 0)),
                  pl.BlockSpec((1, d), lambda i: (0, 0), pipeline_mode=pl.Buffered(1)),
                  pl.BlockSpec((d, m), lambda i: (0, 0), pipeline_mode=pl.Buffered(1))],
        out_specs=pl.BlockSpec((tm, m), lambda i: (i, 0)),
        out_shape=jax.ShapeDtypeStruct((n, m), BF16),
        compiler_params=_cp("arbitrary"),
        name="norm_matmul",
    )(x, g.reshape(1, d), w)


def _conv3(x, prev_row, next_row, w, row0, t_len):
    n = x.shape[0]
    li = lax.broadcasted_iota(jnp.int32, (n, 1), 0)
    gi = li + row0
    xm = pltpu.roll(x, 1, 0)
    xm = jnp.where(li == 0, prev_row, xm)
    xm = jnp.where(gi >= 1, xm, 0.0)
    xp = pltpu.roll(x, n - 1, 0)
    xp = jnp.where(li == n - 1, next_row, xp)
    xp = jnp.where(gi + 1 < t_len, xp, 0.0)
    return xm * w[0:1] + x * w[1:2] + xp * w[2:3]


def _last_row(halo_ref):
    return halo_ref[0].astype(F32)[HALO - 1:HALO]


def _first_row(halo_ref):
    return halo_ref[0].astype(F32)[0:1]


def _halo_specs(tr, cw, t_len, col_blk):
    nh = tr // HALO
    last = t_len // HALO - 1
    main = pl.BlockSpec((1, tr, cw), lambda b, i: (b, i, col_blk))
    prev = pl.BlockSpec((1, HALO, cw), lambda b, i: (b, jnp.maximum(i * nh - 1, 0), col_blk))
    nxt = pl.BlockSpec((1, HALO, cw), lambda b, i: (b, jnp.minimum((i + 1) * nh, last), col_blk))
    return main, prev, nxt


def _fnet_kernel(a_ref, u_ref, cs_ref, o_ref, *, tmf):
    p = jnp.dot(a_ref[0], u_ref[0], preferred_element_type=F32)
    pc = p[:tmf].astype(BF16)
    ps = p[tmf:].astype(BF16)
    y = jnp.dot(pc, cs_ref[0], preferred_element_type=F32) - jnp.dot(ps, cs_ref[1], preferred_element_type=F32)
    o_ref[0] = y.astype(o_ref.dtype)


def _fnet_tables(t, tmf):
    nm = -(-t // tmf)
    assert tmf % HALO == 0
    k = jnp.arange(t, dtype=jnp.int32)[None, :]
    ang = lambda j: ((j[:, None] * k) % t).astype(F32) * (2.0 * math.pi / t)
    ang_hi = ang(jnp.arange(nm * tmf // HALO, dtype=jnp.int32) * HALO)[:, None, :]
    ang_lo = ang(jnp.arange(HALO, dtype=jnp.int32))[None, :, :]
    c = (jnp.cos(ang_hi) * jnp.cos(ang_lo) - jnp.sin(ang_hi) * jnp.sin(ang_lo)).reshape(nm * tmf, t)
    s = (jnp.sin(ang_hi) * jnp.cos(ang_lo) + jnp.cos(ang_hi) * jnp.sin(ang_lo)).reshape(nm * tmf, t)
    valid = jnp.arange(nm * tmf, dtype=jnp.int32)[:, None] < t
    c = jnp.where(valid, c, 0.0).astype(BF16).reshape(nm, tmf, t)
    s = jnp.where(valid, s, 0.0).astype(BF16).reshape(nm, tmf, t)
    a = jnp.concatenate([c, s], axis=1)
    jj = jnp.arange(BRANCH_W, dtype=jnp.int32)[:, None]
    kk = jnp.arange(BRANCH_W, dtype=jnp.int32)[None, :]
    same = (jj // FNET_GW) == (kk // FNET_GW)
    ang2 = (((jj % FNET_GW) * (kk % FNET_GW)) % FNET_GW).astype(F32) * (2.0 * math.pi / FNET_GW)
    scale = 1.0 / math.sqrt(t * FNET_GW)
    cc = jnp.where(same, jnp.cos(ang2), 0.0) * scale
    sc = jnp.where(same, jnp.sin(ang2), 0.0) * scale
    return a, jnp.stack([cc, sc]).astype(BF16)


def _fnet_branch(proj, tmf):
    b, t, _ = proj.shape
    a, cs = _fnet_tables(t, tmf)
    nm = a.shape[0]
    return pl.pallas_call(
        functools.partial(_fnet_kernel, tmf=tmf),
        grid=(nm, b),
        in_specs=[pl.BlockSpec((1, 2 * tmf, t), lambda i, j: (i, 0, 0)),
                  pl.BlockSpec((1, t, BRANCH_W), lambda i, j: (j, 0, COL_FN // BRANCH_W)),
                  pl.BlockSpec((2, BRANCH_W, BRANCH_W), lambda i, j: (0, 0, 0))],
        out_specs=pl.BlockSpec((1, tmf, BRANCH_W), lambda i, j: (j, i, 0)),
        out_shape=jax.ShapeDtypeStruct((b, t, BRANCH_W), BF16),
        compiler_params=_cp("arbitrary", "arbitrary"),
        name="fnet",
    )(a, proj, cs)


def _hyena_filter_kernel(w1t_ref, w1c_ref, w1s_ref, b1_ref, w2_ref, b2_ref, fq_ref, w3f_ref, w3b_ref, dl_ref,
                         o_ref, hid_ref, *, t, half):
    row = lax.broadcasted_iota(jnp.int32, (t, 1), 0)
    pos_f = row.astype(F32)
    pos_b = (t - row).astype(F32)
    bands = (HYENA_EMB - 1) // 2

    @pl.when(pl.program_id(0) == 0)
    def _():
        lane = lax.broadcasted_iota(jnp.int32, (1, 2 * bands), 1)
        fr = 1e-4 + (lane % bands).astype(F32) * ((bands - 1 - 1e-4) / (bands - 1))
        pos = jnp.where(lane < bands, pos_f, pos_b)
        arg = (pos * (2.0 * math.pi / t)) * fr
        pre = ((pos_f * (1.0 / (t - 1))) * w1t_ref[0:1, :] + (pos_b * (1.0 / (t - 1))) * w1t_ref[1:2, :]
               + jnp.dot(jnp.cos(arg), w1c_ref[...], precision=HI, preferred_element_type=F32)
               - jnp.dot(jnp.sin(arg), w1s_ref[...], precision=HI, preferred_element_type=F32)
               + b1_ref[...])
        fq = fq_ref[...]
        h1 = jnp.sin(fq * pre)
        hid_ref[...] = jnp.sin(fq * (jnp.dot(h1, w2_ref[...], precision=HI, preferred_element_type=F32)
                                     + b2_ref[...]))

    def taps(w3_ref, pos):
        dec = jnp.exp(-(pos * (1.0 / (t - 1))) * dl_ref[...])
        return jnp.dot(hid_ref[...], w3_ref[...], precision=HI, preferred_element_type=F32) * dec

    hf = taps(w3f_ref, pos_f)
    hb = jnp.where(row >= 1, taps(w3b_ref, pos_b), 0.0)
    l1 = jnp.sum(jnp.abs(hf), axis=0, keepdims=True) + jnp.sum(jnp.abs(hb), axis=0, keepdims=True)
    inv = 1.0 / l1
    cw = o_ref.shape[2]
    o_ref[0, half - t:half, :] = (hb * inv).astype(o_ref.dtype)
    o_ref[0, half:half + t, :] = (hf * inv).astype(o_ref.dtype)
    if half > t:
        o_ref[0, 0:half - t, :] = jnp.zeros((half - t, cw), o_ref.dtype)
        o_ref[0, half + t:, :] = jnp.zeros((half - t, cw), o_ref.dtype)


def _hyena_filters(t, half, w1, b1, w2, b2, w3, freq):
    cw = 256
    nb = BRANCH_W // cw
    bands = (HYENA_EMB - 1) // 2
    nf = w2.shape[0]
    max_decay = math.log(HYENA_TARGET) / HYENA_FAST
    min_decay = math.log(HYENA_TARGET) / HYENA_SLOW
    deltas = jnp.abs(jnp.linspace(min_decay, max_decay, BRANCH_W, dtype=F32)).reshape(1, BRANCH_W)
    full = lambda shape: pl.BlockSpec(shape, lambda g: (0,) * len(shape))
    w1f = w1.astype(F32)
    zero = jnp.zeros_like
    bd = lambda m: jnp.concatenate([jnp.concatenate([m, zero(m)], axis=1), jnp.concatenate([zero(m), m], axis=1)])
    twice = lambda v: jnp.concatenate([v, v]).reshape(1, 2 * nf).astype(F32)
    w1t = bd(w1f[0:1])
    w3f32 = w3.astype(F32)
    w3_top = jnp.concatenate([w3f32, zero(w3f32)], axis=0)
    w3_bot = jnp.concatenate([zero(w3f32), w3f32], axis=0)
    return pl.pallas_call(
        functools.partial(_hyena_filter_kernel, t=t, half=half),
        grid=(HYENA_ORDER * nb,),
        in_specs=[full((2, 2 * nf)), full((2 * bands, 2 * nf)), full((2 * bands, 2 * nf)), full((1, 2 * nf)),
                  full((2 * nf, 2 * nf)), full((1, 2 * nf)), full((1, 2 * nf)),
                  pl.BlockSpec((2 * nf, cw), lambda g: (0, (g // nb) * 2 * nb + g % nb)),
                  pl.BlockSpec((2 * nf, cw), lambda g: (0, (g // nb) * 2 * nb + nb + g % nb)),
                  pl.BlockSpec((1, cw), lambda g: (0, g % nb))],
        out_specs=pl.BlockSpec((1, 2 * half, cw), lambda g: (g // nb, 0, g % nb)),
        out_shape=jax.ShapeDtypeStruct((HYENA_ORDER, 2 * half, BRANCH_W), BF16),
        scratch_shapes=[pltpu.VMEM((t, 2 * nf), F32)],
        compiler_params=_cp("arbitrary"),
        name="hyena_filter",
    )(w1t, bd(w1f[1:1 + bands]), bd(w1f[1 + bands:]), twice(b1), bd(w2.astype(F32)), twice(b2), twice(freq),
      w3_top, w3_bot, deltas)


def _hyena_tables(p_len):
    f = jnp.arange(p_len, dtype=jnp.int32)[:, None]
    n = jnp.arange(p_len, dtype=jnp.int32)[None, :]
    ang = (((2 * f + 1) * n) % (4 * p_len)).astype(F32) * (math.pi / (2 * p_len))
    a = jnp.concatenate([jnp.cos(ang), -jnp.sin(ang)], axis=0).astype(BF16)
    return a, a.T


def _hy_spec_kernel(a_ref, seg_ref, prev_ref, o_ref, *, p_len):
    g = jnp.dot(a_ref[...], seg_ref[0, 0], preferred_element_type=F32)
    gp = jnp.dot(a_ref[...], prev_ref[0, 0], preferred_element_type=F32)
    first = prev_ref[0, 0, 0:1, :].astype(F32)
    f = lax.broadcasted_iota(jnp.int32, (p_len, 1), 0)
    sign = jnp.where(f % 2 == 0, 1.0, -1.0)
    scale = 1.0 / p_len
    o_ref[0, 0, :p_len] = ((g[:p_len] - sign * gp[p_len:]) * scale).astype(o_ref.dtype)
    o_ref[0, 0, p_len:] = ((g[p_len:] + sign * (gp[:p_len] - first)) * scale).astype(o_ref.dtype)


def _hy_filter_spectrum(a, filt, p_len, nb):
    order, _, w = filt.shape
    lags = filt.reshape(order, 2 * nb, p_len, w)
    return pl.pallas_call(
        functools.partial(_hy_spec_kernel, p_len=p_len),
        grid=(order, 2 * nb - 1),
        in_specs=[pl.BlockSpec((2 * p_len, p_len), lambda o, d: (0, 0)),
                  pl.BlockSpec((1, 1, p_len, w), lambda o, d: (o, d + 1, 0, 0)),
                  pl.BlockSpec((1, 1, p_len, w), lambda o, d: (o, d, 0, 0))],
        out_specs=pl.BlockSpec((1, 1, 2 * p_len, w), lambda o, d: (o, d, 0, 0)),
        out_shape=jax.ShapeDtypeStruct((order, 2 * nb - 1, 2 * p_len, w), BF16),
        compiler_params=_cp("arbitrary", "arbitrary"),
        name="hyena_filter_spectrum",
    )(a, lags, lags)


def _hy_conv_kernel(a_ref, at_ref, k_ref, v_ref, g_ref, bias_ref, o_ref, u_scr, y_scr, *, t_len, p_len, nb, fc):
    cw = v_ref.shape[2]

    def block_rows(i):
        return i * p_len, min(p_len, t_len - i * p_len)

    for j in range(nb):
        r0, n = block_rows(j)
        u = v_ref[0, r0:r0 + n, :]
        if n < p_len:
            u = jnp.concatenate([u, jnp.zeros((p_len - n, cw), u.dtype)], axis=0)
        u_scr[j] = jnp.dot(a_ref[...], u, preferred_element_type=F32).astype(BF16)

    for i in range(nb):
        for r in range(0, p_len, fc):
            acc_r = acc_i = None
            for j in range(nb):
                d = i - j + nb - 1
                ur, ui = u_scr[j, r:r + fc, :], u_scr[j, p_len + r:p_len + r + fc, :]
                kr, ki = k_ref[0, d, r:r + fc, :], k_ref[0, d, p_len + r:p_len + r + fc, :]
                tr_, ti_ = ur * kr - ui * ki, ur * ki + ui * kr
                acc_r, acc_i = (tr_, ti_) if acc_r is None else (acc_r + tr_, acc_i + ti_)
            y_scr[i % 2, r:r + fc, :] = acc_r
            y_scr[i % 2, p_len + r:p_len + r + fc, :] = acc_i
        conv = jnp.dot(at_ref[...], y_scr[i % 2], preferred_element_type=F32)
        r0, n = block_rows(i)
        v = v_ref[0, r0:r0 + n, :].astype(F32)
        o_ref[0, r0:r0 + n, :] = (g_ref[0, r0:r0 + n, :].astype(F32) * (conv[:n] + v * bias_ref[0])).astype(o_ref.dtype)


def _hy_long_conv(a, at, spec, order, v, v_col, gate, gate_col, bias, p_len):
    b, t, _ = v.shape
    nb = -(-t // p_len)
    cw = BRANCH_W // 2
    nh = BRANCH_W // cw
    return pl.pallas_call(
        functools.partial(_hy_conv_kernel, t_len=t, p_len=p_len, nb=nb, fc=HYENA_FREQ_CHUNK),
        grid=(nh, b),
        in_specs=[pl.BlockSpec((2 * p_len, p_len), lambda c, i: (0, 0), pipeline_mode=pl.Buffered(1)),
                  pl.BlockSpec((p_len, 2 * p_len), lambda c, i: (0, 0), pipeline_mode=pl.Buffered(1)),
                  pl.BlockSpec((1, 2 * nb - 1, 2 * p_len, cw), lambda c, i: (order, 0, 0, c),
                               pipeline_mode=pl.Buffered(1)),
                  pl.BlockSpec((1, t, cw), lambda c, i: (i, 0, v_col * nh + c)),
                  pl.BlockSpec((1, t, cw), lambda c, i: (i, 0, gate_col * nh + c)),
                  pl.BlockSpec((1, 1, cw), lambda c, i: (order, 0, c))],
        out_specs=pl.BlockSpec((1, t, cw), lambda c, i: (i, 0, c)),
        out_shape=jax.ShapeDtypeStruct((b, t, BRANCH_W), BF16),
        scratch_shapes=[pltpu.VMEM((nb, 2 * p_len, cw), BF16), pltpu.VMEM((2, 2 * p_len, cw), BF16)],
        compiler_params=_cp("arbitrary", "arbitrary"),
        name="hyena_conv",
    )(a, at, spec, v, gate, bias.reshape(HYENA_ORDER, 1, BRANCH_W))


def _softplus(x):
    return jnp.maximum(x, 0.0) + jnp.log(1.0 + jnp.exp(-jnp.abs(x)))


def _ssd_kernel(xs_ref, b_ref, c_ref, z_ref, dt_ref, dtb_ref, an_ref, dsk_ref, nw_ref, tri_ref, exp_ref, o_ref,
                yf_scr, yb_scr, *, t_len, nc, unroll):
    q = SSM_CHUNK
    gw = SSM_HPG * SSM_HEAD_DIM
    front = q - N_META
    ri = lax.broadcasted_iota(jnp.int32, (q, q), 0)
    ci = lax.broadcasted_iota(jnp.int32, (q, q), 1)
    tri = (ri >= ci, ri <= ci)
    head_of_lane = lax.broadcasted_iota(jnp.int32, (1, gw), 1) // SSM_HEAD_DIM
    lane = lax.broadcasted_iota(jnp.int32, (1, q), 1)
    dt_lane_mask = (lane < 2 * SSM_HPG).astype(F32)
    low_half = lane < SSM_HEAD_DIM
    dtb = dtb_ref[...]
    an = an_ref[...]
    masked_out = -1e30

    def widen(cols):
        return jnp.concatenate([jnp.where(low_half, cols[0], cols[1]), jnp.where(low_half, cols[2], cols[3])], axis=1)

    def cumsum(d, x):
        hi = x.astype(BF16)
        r1 = x - hi.astype(F32)
        mid = r1.astype(BF16)
        lo = (r1 - mid.astype(F32)).astype(BF16)
        s = jnp.dot(tri_ref[d], jnp.concatenate([hi, mid, lo], axis=1), preferred_element_type=F32)
        return s[:, :q] + s[:, q:2 * q] + s[:, 2 * q:]

    def load(ref, r0, nrows):
        return ref[0, pl.ds(r0, nrows), :]

    def dt_of(raw):
        return _softplus(raw.astype(F32) + dtb) * dt_lane_mask

    def chunk_data(c):
        r0 = pl.multiple_of(c * q - front, HALO)
        return load(xs_ref, r0, q), load(b_ref, r0, q), load(c_ref, r0, q), dt_of(load(dt_ref, r0, q))

    def chunk0_data():
        def pad(x):
            return jnp.concatenate([jnp.zeros((front, x.shape[1]), x.dtype), x], axis=0)
        return (pad(load(xs_ref, 0, N_META)), pad(load(b_ref, 0, N_META)), pad(load(c_ref, 0, N_META)),
                pad(dt_of(load(dt_ref, 0, N_META))))

    def stage_local(job, cs_n):
        (xs, bm, cm, dt), d = job
        lanes = [d * SSM_HPG + r for r in range(SSM_HPG)]
        cs_cols = [jnp.broadcast_to(cs_n[:, ln:ln + 1], (q, q)) for ln in lanes]
        cs_w = widen(cs_cols)
        dt_w = jnp.dot(dt.astype(BF16), exp_ref[d], preferred_element_type=F32).astype(BF16)
        tot = cs_w[q - 1:q, :] if d == 0 else cs_w[0:1, :]
        xt = xs * dt_w
        xd = xt * jnp.exp2(tot - cs_w).astype(BF16)
        s_new = lax.dot_general(bm, xd, (((0,), (0,)), ((), ())), preferred_element_type=F32)
        cb = lax.dot_general(cm, bm, (((1,), (1,)), ((), ())), preferred_element_type=F32)
        return cs_cols, xt, cb, cm, s_new, jnp.exp2(cs_w), jnp.exp2(tot), cs_n.T

    def stage_carry(loc, h):
        _, _, _, cm, s_new, ecs, etot, _ = loc
        y_off = jnp.dot(cm, h.astype(BF16), preferred_element_type=F32) * ecs
        return y_off, h * etot + s_new

    def stage_diag(job, cs_n, loc, y_off):
        d = job[1]
        cs_cols, xt, cb = loc[:3]
        cs_t = loc[7]
        ms, xm = [], []
        for r in range(SSM_HPG):
            ln = d * SSM_HPG + r
            dec = jnp.exp2(jnp.where(tri[d], cs_cols[r] - cs_t[ln:ln + 1, :], masked_out))
            ms.append((cb * dec).astype(BF16))
            xm.append(jnp.where(head_of_lane == r, xt, jnp.zeros_like(xt)))
        return y_off + jnp.dot(jnp.concatenate(ms, axis=1), jnp.concatenate(xm, axis=0), preferred_element_type=F32)

    def process_all(fwd_data, bwd_data, hf, hb):
        jobs = [(x, 0) for x in fwd_data] + [(x, 1) for x in bwd_data]
        cs = [cumsum(d, data[3] * an) for data, d in jobs]
        ys, pending = [], None
        for k, (job, c) in enumerate(zip(jobs, cs)):
            loc = stage_local(job, c)
            if pending is not None:
                ys.append(stage_diag(*pending))
            y_off, h = stage_carry(loc, hf if job[1] == 0 else hb)
            hf, hb = (h, hb) if job[1] == 0 else (hf, h)
            pending = (job, c, loc, y_off)
        ys.append(stage_diag(*pending))
        return ys[:len(fwd_data)], ys[len(fwd_data):], hf, hb

    def finish(y, r_out, nrows):
        y = y + load(xs_ref, r_out, nrows).astype(F32) * dsk_ref[...]
        y = y * _silu(load(z_ref, r_out, nrows).astype(F32))
        y = y * lax.rsqrt(jnp.mean(y * y, axis=-1, keepdims=True) + NORM_EPS)
        o_ref[0, pl.ds(r_out, nrows), :] = (y * nw_ref[...]).astype(o_ref.dtype)

    def rows(c):
        return pl.ds(pl.multiple_of(c * q, q), q)

    h0 = jnp.zeros((SSM_STATE, gw), F32)
    (y,), _, hf, _ = process_all([chunk0_data()], [], h0, h0)
    yf_scr[0:q, :] = y

    def scan_body(i, carry):
        hf, hb = carry
        cf = [1 + unroll * i + k for k in range(unroll)]
        cb_ = [nc - c for c in cf]
        yf, yb, hf, hb = process_all([chunk_data(c) for c in cf], [chunk_data(c) for c in cb_], hf, hb)
        for c, y in zip(cf, yf):
            yf_scr[rows(c), :] = y
        for c, y in zip(cb_, yb):
            yb_scr[rows(c), :] = y
        return hf, hb

    _, hb = lax.fori_loop(0, (nc - 1) // unroll, scan_body, (hf, h0))
    _, (y,), _, _ = process_all([], [chunk0_data()], h0, hb)
    yb_scr[0:q, :] = y

    finish(yf_scr[front:q, :] + yb_scr[front:q, :], 0, N_META)

    def finish_body(c, carry):
        finish(yf_scr[rows(c), :] + yb_scr[rows(c), :], pl.multiple_of(c * q - front, HALO), q)
        return carry

    lax.fori_loop(1, nc, finish_body, 0)


def _ssd_branch(xbc, proj, dt_bias, a_log, d_skip, norm_w):
    b, t, _ = xbc.shape
    q = SSM_CHUNK
    gw = SSM_HPG * SSM_HEAD_DIM
    nc = (q - N_META + t) // q
    assert nc * q == q - N_META + t, "sequence length minus meta tokens must be a multiple of the SSD chunk"
    a = -jnp.exp(a_log.astype(F32)).reshape(2, SSM_GROUPS, SSM_HPG)
    dtb = dt_bias.astype(F32).reshape(2, SSM_GROUPS, SSM_HPG)
    pad = lambda v: jnp.pad(jnp.transpose(v, (1, 0, 2)).reshape(SSM_GROUPS, 1, 2 * SSM_HPG),
                            ((0, 0), (0, 0), (0, q - 2 * SSM_HPG)))
    a_n = pad(a * math.log2(math.e))
    dtb_n = pad(dtb)
    ri = jnp.arange(q, dtype=jnp.int32)[:, None]
    ci = jnp.arange(q, dtype=jnp.int32)[None, :]
    tri = jnp.stack([ri >= ci, ri <= ci]).astype(BF16)
    cj = jnp.arange(gw, dtype=jnp.int32)[None, :] // SSM_HEAD_DIM
    expand = jnp.stack([ri == cj, ri == cj + SSM_HPG]).astype(BF16)
    dsk = jnp.repeat(d_skip.astype(F32).reshape(SSM_GROUPS, 1, SSM_HPG), SSM_HEAD_DIM, axis=-1)
    nw = norm_w.astype(F32).reshape(SSM_GROUPS, 1, gw)
    par = lambda shape: pl.BlockSpec((None,) + shape, lambda i, g: (g, 0, 0))
    return pl.pallas_call(
        functools.partial(_ssd_kernel, t_len=t, nc=nc, unroll=math.gcd(nc - 1, SSD_UNROLL)),
        grid=(b, SSM_GROUPS),
        in_specs=[pl.BlockSpec((1, t, gw), lambda i, g: (i, 0, g)),
                  pl.BlockSpec((1, t, SSM_STATE), lambda i, g: (i, 0, 4 + g)),
                  pl.BlockSpec((1, t, SSM_STATE), lambda i, g: (i, 0, 6 + g)),
                  pl.BlockSpec((1, t, gw), lambda i, g: (i, 0, COL_Z // gw + g)),
                  pl.BlockSpec((1, t, q), lambda i, g: (i, 0, COL_DT // q + g)),
                  par((1, q)), par((1, q)), par((1, gw)), par((1, gw)),
                  pl.BlockSpec((2, q, q), lambda i, g: (0, 0, 0)),
                  pl.BlockSpec((2, q, gw), lambda i, g: (0, 0, 0))],
        out_specs=pl.BlockSpec((1, t, gw), lambda i, g: (i, 0, g)),
        out_shape=jax.ShapeDtypeStruct((b, t, BRANCH_W), BF16),
        scratch_shapes=[pltpu.VMEM((nc * q, gw), F32), pltpu.VMEM((nc * q, gw), F32)],
        compiler_params=_cp("arbitrary", "arbitrary"),
        name="ssd",
    )(xbc, xbc, xbc, proj, proj, dtb_n, a_n, dsk, nw, tri, expand)


def _merge_kernel(h_ref, y0_ref, y1_ref, y2_ref, g0_ref, g1_ref, g2_ref, g3_ref, bg_ref, cg_ref, xi_ref, cgp_ref,
                  xip_ref, cgn_ref, xin_ref, scw_ref, wb_ref, wo_ref, o_ref, *, tr, t_len):
    u = cg_ref[0].astype(F32) * xi_ref[0].astype(F32)
    prev = _last_row(cgp_ref) * _last_row(xip_ref)
    nxt = _first_row(cgn_ref) * _first_row(xin_ref)
    y_sc = (bg_ref[0].astype(F32) * _conv3(u, prev, nxt, scw_ref[...], pl.program_id(1) * tr, t_len)).astype(BF16)
    merged = None
    for k, (y, g_ref) in enumerate(zip((y0_ref[0], y1_ref[0], y2_ref[0], y_sc), (g0_ref, g1_ref, g2_ref, g3_ref))):
        term = (1.0 + jnp.tanh(g_ref[0].astype(F32))) * jnp.dot(y, wb_ref[k], preferred_element_type=F32)
        merged = term if merged is None else merged + term
    o_ref[0] = h_ref[0] + jnp.dot((0.5 * merged).astype(BF16), wo_ref[...], preferred_element_type=F32)


def _merge(h, ys, proj, sc_w, wb, wo, tr):
    b, t, d = h.shape
    nb = wb.shape[0]
    row = lambda w, c: pl.BlockSpec((1, tr, w), lambda b_, i: (b_, i, c))
    c0 = COL_SC // BRANCH_W
    bg, _, _ = _halo_specs(tr, BRANCH_W, t, c0)
    cg, cgp, cgn = _halo_specs(tr, BRANCH_W, t, c0 + 1)
    xi, xip, xin = _halo_specs(tr, BRANCH_W, t, c0 + 2)
    return pl.pallas_call(
        functools.partial(_merge_kernel, tr=tr, t_len=t),
        grid=(b, pl.cdiv(t, tr)),
        in_specs=[row(d, 0)] + [row(BRANCH_W, 0)] * len(ys) + [row(d, COL_GATE // d + k) for k in range(nb)]
                 + [bg, cg, xi, cgp, xip, cgn, xin, _resident((3, BRANCH_W)), _resident((nb, BRANCH_W, d)),
                    _resident((d, d))],
        out_specs=row(d, 0),
        out_shape=jax.ShapeDtypeStruct((b, t, d), F32),
        compiler_params=_cp("arbitrary", "arbitrary"),
        name="merge",
    )(h, *ys, *([proj] * (nb + 7)), sc_w, wb, wo)


def _rms_rows(x, g):
    return x * lax.rsqrt(jnp.mean(x * x, axis=-1, keepdims=True) + NORM_EPS) * g


def _stage_normed_rows(xs_ref, x_ref, xp_ref, xn_ref, g, row0, tr, t_len):
    li = lax.broadcasted_iota(jnp.int32, (tr, 1), 0)
    xs_ref[HALO:HALO + tr, :] = jnp.where(row0 + li < t_len, _rms_rows(x_ref[0], g), 0.0).astype(BF16)
    xs_ref[0:HALO, :] = jnp.where(row0 > 0, _rms_rows(xp_ref[0], g), 0.0).astype(BF16)
    xs_ref[HALO + tr:, :] = jnp.where(row0 + tr < t_len, _rms_rows(xn_ref[0], g), 0.0).astype(BF16)


def _conv3_rows(u, w):
    n = u.shape[0]
    return pltpu.roll(u, 1, 0) * w[0:1] + u * w[1:2] + pltpu.roll(u, n - 1, 0) * w[2:3]


def _row_halo_specs(tr, d, t_len):
    nh = tr // HALO
    last = t_len // HALO - 1
    return (pl.BlockSpec((1, tr, d), lambda b, i: (b, i, 0)),
            pl.BlockSpec((1, HALO, d), lambda b, i: (b, jnp.maximum(i * nh - 1, 0), 0)),
            pl.BlockSpec((1, HALO, d), lambda b, i: (b, jnp.minimum((i + 1) * nh, last), 0)))


def _resident(shape):
    return pl.BlockSpec(shape, lambda b, i: (0,) * len(shape), pipeline_mode=pl.Buffered(1))


def _col_chunks(n):
    return tuple((s0, min(MXU_COLS, n - s0)) for s0 in range(0, n, MXU_COLS))


def _conv_proj_kernel(x_ref, xp_ref, xn_ref, g_ref, w_ref, c_ref, b_ref, o_ref, xs_ref, *, tr, t_len, n_act):
    _stage_normed_rows(xs_ref, x_ref, xp_ref, xn_ref, g_ref[...], pl.program_id(1) * tr, tr, t_len)
    xs = xs_ref[...]
    for s0, w in _col_chunks(w_ref.shape[1]):
        u = jnp.dot(xs, w_ref[:, s0:s0 + w], preferred_element_type=F32)
        y = _conv3_rows(u, c_ref[:, s0:s0 + w])[HALO:HALO + tr] + b_ref[:, s0:s0 + w]
        o_ref[0, :, s0:s0 + w] = (_silu(y) if s0 < n_act else y).astype(o_ref.dtype)


def _conv_proj(h, g, w, taps, bias, n_act, tr):
    b, t, d = h.shape
    m = w.shape[1]
    main, prev, nxt = _row_halo_specs(tr, d, t)
    return pl.pallas_call(
        functools.partial(_conv_proj_kernel, tr=tr, t_len=t, n_act=n_act),
        grid=(b, pl.cdiv(t, tr)),
        in_specs=[main, prev, nxt, _resident((1, d)), _resident((d, m)), _resident((3, m)), _resident((1, m))],
        out_specs=pl.BlockSpec((1, tr, m), lambda b_, i: (b_, i, 0)),
        out_shape=jax.ShapeDtypeStruct((b, t, m), BF16),
        scratch_shapes=[pltpu.VMEM((tr + 2 * HALO, d), BF16)],
        compiler_params=_cp("arbitrary", "arbitrary"),
        name="conv_proj",
    )(h, h, h, g.reshape(1, d), w, taps, bias)


def _ffn_kernel(x_ref, xp_ref, xn_ref, g_ref, wu_ref, cw_ref, wd_ref, o_ref, xs_ref, gate_ref, *, tr, t_len):
    _stage_normed_rows(xs_ref, x_ref, xp_ref, xn_ref, g_ref[...], pl.program_id(1) * tr, tr, t_len)
    xs = xs_ref[...]
    dff = wd_ref.shape[0]
    for s0, w in _col_chunks(dff):
        a = _conv3_rows(jnp.dot(xs, wu_ref[:, s0:s0 + w], preferred_element_type=F32), cw_ref[:, s0:s0 + w])
        v = _conv3_rows(jnp.dot(xs, wu_ref[:, dff + s0:dff + s0 + w], preferred_element_type=F32),
                        cw_ref[:, dff + s0:dff + s0 + w])
        gate_ref[:, s0:s0 + w] = (_silu(a) * v)[HALO:HALO + tr].astype(BF16)
    o_ref[0] = x_ref[0] + jnp.dot(gate_ref[...], wd_ref[...], preferred_element_type=F32)


def _ffn(h, g, w_up, conv_w, w_down, tr):
    b, t, d = h.shape
    dff = w_down.shape[0]
    main, prev, nxt = _row_halo_specs(tr, d, t)
    return pl.pallas_call(
        functools.partial(_ffn_kernel, tr=tr, t_len=t),
        grid=(b, pl.cdiv(t, tr)),
        in_specs=[main, prev, nxt, _resident((1, d)), _resident((d, 2 * dff)), _resident((3, 2 * dff)),
                  _resident((dff, d))],
        out_specs=pl.BlockSpec((1, tr, d), lambda b_, i: (b_, i, 0)),
        out_shape=jax.ShapeDtypeStruct((b, t, d), F32),
        scratch_shapes=[pltpu.VMEM((tr + 2 * HALO, d), BF16), pltpu.VMEM((tr, dff), BF16)],
        compiler_params=_cp("arbitrary", "arbitrary"),
        name="ffn",
    )(h, h, h, g.reshape(1, d), w_up, conv_w, w_down)


def _final_norm_kernel(x_ref, xn_ref, g_ref, o_ref, *, tr):
    g = g_ref[...]
    o_ref[0, 0:tr - N_META, :] = _rms_rows(x_ref[0], g)[N_META:]
    o_ref[0, tr - N_META:, :] = _rms_rows(xn_ref[0], g)


def _final_norm(x, g, tr):
    b, t, d = x.shape
    assert N_META == HALO
    main, _, nxt = _row_halo_specs(tr, d, t)
    return pl.pallas_call(
        functools.partial(_final_norm_kernel, tr=tr),
        grid=(b, pl.cdiv(t - N_META, tr)),
        in_specs=[main, nxt, pl.BlockSpec((1, d), lambda b_, i: (0, 0))],
        out_specs=pl.BlockSpec((1, tr, d), lambda b_, i: (b_, i, 0)),
        out_shape=jax.ShapeDtypeStruct((b, t - N_META, d), F32),
        compiler_params=_cp("arbitrary", "arbitrary"),
        name="final_norm",
    )(x, x, g.reshape(1, d))


def _prep_w_in(w_in):
    d = w_in.shape[0]
    o_z, o_xbc, o_dt = BRANCH_W, 2 * BRANCH_W, 2 * BRANCH_W + 1024
    ndt = 2 * SSM_GROUPS * SSM_HPG
    o_hy = o_dt + ndt
    o_sc = o_hy + 3 * BRANCH_W
    o_gate = o_sc + 3 * BRANCH_W
    w_dt = w_in[:, o_dt:o_hy].reshape(d, 2, SSM_GROUPS, SSM_HPG)
    w_dt = jnp.transpose(w_dt, (0, 2, 1, 3)).reshape(d, SSM_GROUPS, 2 * SSM_HPG)
    w_dt = jnp.pad(w_dt, ((0, 0), (0, 0), (0, SSM_CHUNK - 2 * SSM_HPG))).reshape(d, SSM_GROUPS * SSM_CHUNK)
    w_conv = jnp.concatenate([w_in[:, o_xbc:o_dt], w_in[:, o_hy:o_sc]], axis=1).astype(BF16)
    w_plain = jnp.concatenate([w_in[:, :o_xbc], 0.5 * w_in[:, o_gate:], w_in[:, o_sc:o_gate], w_dt],
                              axis=1).astype(BF16)
    return w_conv, w_plain


def _hyena_fft_len(t):
    tf = 528
    nfb = -(-(2 * t - 1) // (2 * tf))
    return 2 * tf * nfb, tf


def _run_trunk(x, meta_tokens, norm_final, layers):
    b, seq, d = x.shape
    t = seq + N_META
    meta = jnp.broadcast_to(meta_tokens[None].astype(x.dtype), (b, N_META, d))
    h = jnp.concatenate([meta, x], axis=1)
    tr = _pick_tile(t, ROW_TILE)
    hy_a, hy_at = _hyena_tables(HYENA_BLOCK)
    hy_nb = -(-t // HYENA_BLOCK)
    tmf = tr
    tok = min(TOKEN_TILE, b * t)
    hv, hx1, hx2 = (CV_HY // BRANCH_W + k for k in range(3))
    for p in layers:
        w_conv, w_plain = p["w_in"]
        taps = jnp.concatenate([p["ssm_conv_w"], p["hyena_conv_w"]], axis=1)
        bias = jnp.concatenate([p["ssm_conv_b"], jnp.zeros((3 * BRANCH_W,), F32)]).reshape(1, N_CONV)
        projc = _conv_proj(h, p["norm_mix"], w_conv, taps, bias, CV_HY, tr)
        proj = _norm_matmul(h.reshape(b * t, d), p["norm_mix"], w_plain, min(512, tok), 768).reshape(b, t, N_PLAIN)
        y_fn = _fnet_branch(proj, tmf)
        y_ssm = _ssd_branch(projc, proj, p["ssm_dt_bias"], p["ssm_a_log"], p["ssm_d"], p["ssm_norm"])
        filt = _hyena_filters(t, hy_nb * HYENA_BLOCK, p["hyena_w1"], p["hyena_b1"], p["hyena_w2"], p["hyena_b2"],
                              p["hyena_w3"], p["hyena_freq"])
        spec = _hy_filter_spectrum(hy_a, filt, HYENA_BLOCK, hy_nb)
        z = _hy_long_conv(hy_a, hy_at, spec, 0, projc, hv, projc, hx1, p["hyena_bias"], HYENA_BLOCK)
        y_hy = _hy_long_conv(hy_a, hy_at, spec, 1, z, 0, projc, hx2, p["hyena_bias"], HYENA_BLOCK)
        h = _merge(h, [y_fn, y_ssm, y_hy], proj, p["sc_conv_w"], p["w_branch"], p["w_out"], tr)
        h = _ffn(h, p["norm_ffn"], p["w_up"], p["ffn_conv_w"], p["w_down"], tr)
    return _final_norm(h, norm_final, tr)


def kernel(x_prompt, x_sample, meta_tokens, norm_mix, w_in, ssm_conv_w, ssm_conv_b, ssm_dt_bias, ssm_a_log, ssm_d,
           ssm_norm, hyena_conv_w, hyena_w1, hyena_b1, hyena_w2, hyena_b2, hyena_w3, hyena_freq, hyena_bias,
           sc_conv_w, w_branch, w_out, norm_ffn, ffn_conv_w, w_up, w_down, norm_final):
    depth = w_in.shape[0]
    layers = []
    for l in range(depth):
        layers.append(dict(
            norm_mix=norm_mix[l], w_in=_prep_w_in(w_in[l]), ssm_conv_w=ssm_conv_w[l], ssm_conv_b=ssm_conv_b[l],
            ssm_dt_bias=ssm_dt_bias[l], ssm_a_log=ssm_a_log[l], ssm_d=ssm_d[l], ssm_norm=ssm_norm[l],
            hyena_conv_w=hyena_conv_w[l], hyena_w1=hyena_w1[l], hyena_b1=hyena_b1[l], hyena_w2=hyena_w2[l],
            hyena_b2=hyena_b2[l], hyena_w3=hyena_w3[l], hyena_freq=hyena_freq[l], hyena_bias=hyena_bias[l],
            sc_conv_w=sc_conv_w[l], w_branch=w_branch[l].astype(BF16), w_out=w_out[l].astype(BF16),
            norm_ffn=norm_ffn[l], ffn_conv_w=ffn_conv_w[l], w_up=w_up[l].astype(BF16), w_down=w_down[l].astype(BF16)))
    y_prompt = _run_trunk(x_prompt, meta_tokens, norm_final, layers)
    y_sample = _run_trunk(x_sample, meta_tokens, norm_final, layers)
    return (y_prompt, y_sample)
```

```python
import functools
import math

import jax
import jax.numpy as jnp
from jax import lax
from jax.experimental import pallas as pl
from jax.experimental.pallas import tpu as pltpu

F32 = jnp.float32
BF16 = jnp.bfloat16
HI = lax.Precision.HIGHEST

NORM_EPS = 1e-6
N_META = 16
BRANCH_W = 512
FNET_GW = 128
SSM_GROUPS = 2
SSM_HPG = 4
SSM_HEAD_DIM = 64
SSM_STATE = 128
SSM_CHUNK = 128
HYENA_ORDER = 2
HYENA_EMB = 33
HYENA_TARGET = 1e-2
HYENA_FAST = 0.3
HYENA_SLOW = 1.5

ROW_TILE = 700
TOKEN_TILE = 1024
HYENA_BLOCK = 704
HYENA_FREQ_CHUNK = 32
MXU_COLS = 256
PROJ_CHUNK = 2 * MXU_COLS
SSD_UNROLL = 4
HALO = 16
VMEM_LIMIT = 56 * 1024 * 1024

CV_XBC = 0
CV_HY = 1024
N_CONV = 2560
COL_FN = 0
COL_Z = 512
COL_GATE = 1024
COL_SC = 5120
COL_DT = 6656
N_PLAIN = 6912


def _cp(*sem):
    return pltpu.CompilerParams(dimension_semantics=sem, vmem_limit_bytes=VMEM_LIMIT)


def _pick_tile(n, target, mult=16):
    k = max(1, -(-n // target))
    t = -(-n // k)
    return -(-t // mult) * mult


def _silu(x):
    h = 0.5 * x
    return h * (1.0 + jnp.tanh(h))


def _norm_matmul_kernel(x_ref, g_ref, w_ref, o_ref, *, tn):
    x = x_ref[...]
    ms = jnp.mean(x * x, axis=-1, keepdims=True)
    xn = (x * lax.rsqrt(ms + NORM_EPS) * g_ref[...]).astype(BF16)
    for s0 in range(0, w_ref.shape[1], tn):
        o_ref[:, s0:s0 + tn] = jnp.dot(xn, w_ref[:, s0:s0 + tn], preferred_element_type=F32).astype(o_ref.dtype)


def _norm_matmul(x, g, w, tm, tn):
    n, d = x.shape
    m = w.shape[1]
    assert m % tn == 0
    tm = min(tm, -(-n // HALO) * HALO)
    return pl.pallas_call(
        functools.partial(_norm_matmul_kernel, tn=tn),
        grid=(pl.cdiv(n, tm),),
        in_specs=[pl.BlockSpec((tm, d), lambda i: (i, 0)),
                  pl.BlockSpec((1, d), lambda i: (0, 0), pipeline_mode=pl.Buffered(1)),
                  pl.BlockSpec((d, m), lambda i: (0, 0), pipeline_mode=pl.Buffered(1))],
        out_specs=pl.BlockSpec((tm, m), lambda i: (i, 0)),
        out_shape=jax.ShapeDtypeStruct((n, m), BF16),
        compiler_params=_cp("arbitrary"),
        name="norm_matmul",
    )(x, g.reshape(1, d), w)


def _conv3(x, prev_row, next_row, w, row0, t_len):
    n = x.shape[0]
    li = lax.broadcasted_iota(jnp.int32, (n, 1), 0)
    gi = li + row0
    xm = pltpu.roll(x, 1, 0)
    xm = jnp.where(li == 0, prev_row, xm)
    xm = jnp.where(gi >= 1, xm, 0.0)
    xp = pltpu.roll(x, n - 1, 0)
    xp = jnp.where(li == n - 1, next_row, xp)
    xp = jnp.where(gi + 1 < t_len, xp, 0.0)
    return xm * w[0:1] + x * w[1:2] + xp * w[2:3]


def _last_row(halo_ref):
    return halo_ref[0].astype(F32)[HALO - 1:HALO]


def _first_row(halo_ref):
    return halo_ref[0].astype(F32)[0:1]


def _halo_specs(tr, cw, t_len, col_blk):
    nh = tr // HALO
    last = t_len // HALO - 1
    main = pl.BlockSpec((1, tr, cw), lambda b, i: (b, i, col_blk))
    prev = pl.BlockSpec((1, HALO, cw), lambda b, i: (b, jnp.maximum(i * nh - 1, 0), col_blk))
    nxt = pl.BlockSpec((1, HALO, cw), lambda b, i: (b, jnp.minimum((i + 1) * nh, last), col_blk))
    return main, prev, nxt


def _fnet_kernel(a_ref, u_ref, cs_ref, o_ref, *, tmf):
    p = jnp.dot(a_ref[0], u_ref[0], preferred_element_type=F32)
    pc = p[:tmf].astype(BF16)
    ps = p[tmf:].astype(BF16)
    y = jnp.dot(pc, cs_ref[0], preferred_element_type=F32) - jnp.dot(ps, cs_ref[1], preferred_element_type=F32)
    o_ref[0] = y.astype(o_ref.dtype)


def _fnet_tables(t, tmf):
    nm = -(-t // tmf)
    assert tmf % HALO == 0
    k = jnp.arange(t, dtype=jnp.int32)[None, :]
    ang = lambda j: ((j[:, None] * k) % t).astype(F32) * (2.0 * math.pi / t)
    ang_hi = ang(jnp.arange(nm * tmf // HALO, dtype=jnp.int32) * HALO)[:, None, :]
    ang_lo = ang(jnp.arange(HALO, dtype=jnp.int32))[None, :, :]
    c = (jnp.cos(ang_hi) * jnp.cos(ang_lo) - jnp.sin(ang_hi) * jnp.sin(ang_lo)).reshape(nm * tmf, t)
    s = (jnp.sin(ang_hi) * jnp.cos(ang_lo) + jnp.cos(ang_hi) * jnp.sin(ang_lo)).reshape(nm * tmf, t)
    valid = jnp.arange(nm * tmf, dtype=jnp.int32)[:, None] < t
    c = jnp.where(valid, c, 0.0).astype(BF16).reshape(nm, tmf, t)
    s = jnp.where(valid, s, 0.0).astype(BF16).reshape(nm, tmf, t)
    a = jnp.concatenate([c, s], axis=1)
    jj = jnp.arange(BRANCH_W, dtype=jnp.int32)[:, None]
    kk = jnp.arange(BRANCH_W, dtype=jnp.int32)[None, :]
    same = (jj // FNET_GW) == (kk // FNET_GW)
    ang2 = (((jj % FNET_GW) * (kk % FNET_GW)) % FNET_GW).astype(F32) * (2.0 * math.pi / FNET_GW)
    scale = 1.0 / math.sqrt(t * FNET_GW)
    cc = jnp.where(same, jnp.cos(ang2), 0.0) * scale
    sc = jnp.where(same, jnp.sin(ang2), 0.0) * scale
    return a, jnp.stack([cc, sc]).astype(BF16)


def _fnet_branch(proj, tmf):
    b, t, _ = proj.shape
    a, cs = _fnet_tables(t, tmf)
    nm = a.shape[0]
    return pl.pallas_call(
        functools.partial(_fnet_kernel, tmf=tmf),
        grid=(nm, b),
        in_specs=[pl.BlockSpec((1, 2 * tmf, t), lambda i, j: (i, 0, 0)),
                  pl.BlockSpec((1, t, BRANCH_W), lambda i, j: (j, 0, COL_FN // BRANCH_W)),
                  pl.BlockSpec((2, BRANCH_W, BRANCH_W), lambda i, j: (0, 0, 0))],
        out_specs=pl.BlockSpec((1, tmf, BRANCH_W), lambda i, j: (j, i, 0)),
        out_shape=jax.ShapeDtypeStruct((b, t, BRANCH_W), BF16),
        compiler_params=_cp("arbitrary", "arbitrary"),
        name="fnet",
    )(a, proj, cs)


def _hyena_filter_kernel(w1t_ref, w1c_ref, w1s_ref, b1_ref, w2_ref, b2_ref, fq_ref, w3f_ref, w3b_ref, dl_ref,
                         o_ref, hid_ref, *, t, half):
    row = lax.broadcasted_iota(jnp.int32, (t, 1), 0)
    pos_f = row.astype(F32)
    pos_b = (t - row).astype(F32)
    bands = (HYENA_EMB - 1) // 2

    @pl.when(pl.program_id(0) == 0)
    def _():
        lane = lax.broadcasted_iota(jnp.int32, (1, 2 * bands), 1)
        fr = 1e-4 + (lane % bands).astype(F32) * ((bands - 1 - 1e-4) / (bands - 1))
        pos = jnp.where(lane < bands, pos_f, pos_b)
        arg = (pos * (2.0 * math.pi / t)) * fr
        pre = ((pos_f * (1.0 / (t - 1))) * w1t_ref[0:1, :] + (pos_b * (1.0 / (t - 1))) * w1t_ref[1:2, :]
               + jnp.dot(jnp.cos(arg), w1c_ref[...], precision=HI, preferred_element_type=F32)
               - jnp.dot(jnp.sin(arg), w1s_ref[...], precision=HI, preferred_element_type=F32)
               + b1_ref[...])
        fq = fq_ref[...]
        h1 = jnp.sin(fq * pre)
        hid_ref[...] = jnp.sin(fq * (jnp.dot(h1, w2_ref[...], precision=HI, preferred_element_type=F32)
                                     + b2_ref[...]))

    def taps(w3_ref, pos):
        dec = jnp.exp(-(pos * (1.0 / (t - 1))) * dl_ref[...])
        return jnp.dot(hid_ref[...], w3_ref[...], precision=HI, preferred_element_type=F32) * dec

    hf = taps(w3f_ref, pos_f)
    hb = jnp.where(row >= 1, taps(w3b_ref, pos_b), 0.0)
    l1 = jnp.sum(jnp.abs(hf), axis=0, keepdims=True) + jnp.sum(jnp.abs(hb), axis=0, keepdims=True)
    inv = 1.0 / l1
    cw = o_ref.shape[2]
    o_ref[0, half - t:half, :] = (hb * inv).astype(o_ref.dtype)
    o_ref[0, half:half + t, :] = (hf * inv).astype(o_ref.dtype)
    if half > t:
        o_ref[0, 0:half - t, :] = jnp.zeros((half - t, cw), o_ref.dtype)
        o_ref[0, half + t:, :] = jnp.zeros((half - t, cw), o_ref.dtype)


def _hyena_filters(t, half, w1, b1, w2, b2, w3, freq):
    cw = 256
    nb = BRANCH_W // cw
    bands = (HYENA_EMB - 1) // 2
    nf = w2.shape[0]
    max_decay = math.log(HYENA_TARGET) / HYENA_FAST
    min_decay = math.log(HYENA_TARGET) / HYENA_SLOW
    deltas = jnp.abs(jnp.linspace(min_decay, max_decay, BRANCH_W, dtype=F32)).reshape(1, BRANCH_W)
    full = lambda shape: pl.BlockSpec(shape, lambda g: (0,) * len(shape))
    w1f = w1.astype(F32)
    zero = jnp.zeros_like
    bd = lambda m: jnp.concatenate([jnp.concatenate([m, zero(m)], axis=1), jnp.concatenate([zero(m), m], axis=1)])
    twice = lambda v: jnp.concatenate([v, v]).reshape(1, 2 * nf).astype(F32)
    w1t = bd(w1f[0:1])
    w3f32 = w3.astype(F32)
    w3_top = jnp.concatenate([w3f32, zero(w3f32)], axis=0)
    w3_bot = jnp.concatenate([zero(w3f32), w3f32], axis=0)
    return pl.pallas_call(
        functools.partial(_hyena_filter_kernel, t=t, half=half),
        grid=(HYENA_ORDER * nb,),
        in_specs=[full((2, 2 * nf)), full((2 * bands, 2 * nf)), full((2 * bands, 2 * nf)), full((1, 2 * nf)),
                  full((2 * nf, 2 * nf)), full((1, 2 * nf)), full((1, 2 * nf)),
                  pl.BlockSpec((2 * nf, cw), lambda g: (0, (g // nb) * 2 * nb + g % nb)),
                  pl.BlockSpec((2 * nf, cw), lambda g: (0, (g // nb) * 2 * nb + nb + g % nb)),
                  pl.BlockSpec((1, cw), lambda g: (0, g % nb))],
        out_specs=pl.BlockSpec((1, 2 * half, cw), lambda g: (g // nb, 0, g % nb)),
        out_shape=jax.ShapeDtypeStruct((HYENA_ORDER, 2 * half, BRANCH_W), BF16),
        scratch_shapes=[pltpu.VMEM((t, 2 * nf), F32)],
        compiler_params=_cp("arbitrary"),
        name="hyena_filter",
    )(w1t, bd(w1f[1:1 + bands]), bd(w1f[1 + bands:]), twice(b1), bd(w2.astype(F32)), twice(b2), twice(freq),
      w3_top, w3_bot, deltas)


def _hyena_tables(p_len):
    f = jnp.arange(p_len, dtype=jnp.int32)[:, None]
    n = jnp.arange(p_len, dtype=jnp.int32)[None, :]
    ang = (((2 * f + 1) * n) % (4 * p_len)).astype(F32) * (math.pi / (2 * p_len))
    a = jnp.concatenate([jnp.cos(ang), -jnp.sin(ang)], axis=0).astype(BF16)
    return a, a.T


def _hy_spec_kernel(a_ref, seg_ref, prev_ref, o_ref, *, p_len):
    g = jnp.dot(a_ref[...], seg_ref[0, 0], preferred_element_type=F32)
    gp = jnp.dot(a_ref[...], prev_ref[0, 0], preferred_element_type=F32)
    first = prev_ref[0, 0, 0:1, :].astype(F32)
    f = lax.broadcasted_iota(jnp.int32, (p_len, 1), 0)
    sign = jnp.where(f % 2 == 0, 1.0, -1.0)
    scale = 1.0 / p_len
    o_ref[0, 0, :p_len] = ((g[:p_len] - sign * gp[p_len:]) * scale).astype(o_ref.dtype)
    o_ref[0, 0, p_len:] = ((g[p_len:] + sign * (gp[:p_len] - first)) * scale).astype(o_ref.dtype)


def _hy_filter_spectrum(a, filt, p_len, nb):
    order, _, w = filt.shape
    lags = filt.reshape(order, 2 * nb, p_len, w)
    return pl.pallas_call(
        functools.partial(_hy_spec_kernel, p_len=p_len),
        grid=(order, 2 * nb - 1),
        in_specs=[pl.BlockSpec((2 * p_len, p_len), lambda o, d: (0, 0)),
                  pl.BlockSpec((1, 1, p_len, w), lambda o, d: (o, d + 1, 0, 0)),
                  pl.BlockSpec((1, 1, p_len, w), lambda o, d: (o, d, 0, 0))],
        out_specs=pl.BlockSpec((1, 1, 2 * p_len, w), lambda o, d: (o, d, 0, 0)),
        out_shape=jax.ShapeDtypeStruct((order, 2 * nb - 1, 2 * p_len, w), BF16),
        compiler_params=_cp("arbitrary", "arbitrary"),
        name="hyena_filter_spectrum",
    )(a, lags, lags)


def _hy_conv_kernel(a_ref, at_ref, k_ref, v_ref, g_ref, bias_ref, o_ref, u_scr, y_scr, *, t_len, p_len, nb, fc):
    cw = v_ref.shape[2]

    def block_rows(i):
        return i * p_len, min(p_len, t_len - i * p_len)

    for j in range(nb):
        r0, n = block_rows(j)
        u = v_ref[0, r0:r0 + n, :]
        if n < p_len:
            u = jnp.concatenate([u, jnp.zeros((p_len - n, cw), u.dtype)], axis=0)
        u_scr[j] = jnp.dot(a_ref[...], u, preferred_element_type=F32).astype(BF16)

    for i in range(nb):
        for r in range(0, p_len, fc):
            acc_r = acc_i = None
            for j in range(nb):
                d = i - j + nb - 1
                ur, ui = u_scr[j, r:r + fc, :], u_scr[j, p_len + r:p_len + r + fc, :]
                kr, ki = k_ref[0, d, r:r + fc, :], k_ref[0, d, p_len + r:p_len + r + fc, :]
                tr_, ti_ = ur * kr - ui * ki, ur * ki + ui * kr
                acc_r, acc_i = (tr_, ti_) if acc_r is None else (acc_r + tr_, acc_i + ti_)
            y_scr[i % 2, r:r + fc, :] = acc_r
            y_scr[i % 2, p_len + r:p_len + r + fc, :] = acc_i
        conv = jnp.dot(at_ref[...], y_scr[i % 2], preferred_element_type=F32)
        r0, n = block_rows(i)
        v = v_ref[0, r0:r0 + n, :].astype(F32)
        o_ref[0, r0:r0 + n, :] = (g_ref[0, r0:r0 + n, :].astype(F32) * (conv[:n] + v * bias_ref[0])).astype(o_ref.dtype)


def _hy_long_conv(a, at, spec, order, v, v_col, gate, gate_col, bias, p_len):
    b, t, _ = v.shape
    nb = -(-t // p_len)
    cw = BRANCH_W // 2
    nh = BRANCH_W // cw
    return pl.pallas_call(
        functools.partial(_hy_conv_kernel, t_len=t, p_len=p_len, nb=nb, fc=HYENA_FREQ_CHUNK),
        grid=(nh, b),
        in_specs=[pl.BlockSpec((2 * p_len, p_len), lambda c, i: (0, 0), pipeline_mode=pl.Buffered(1)),
                  pl.BlockSpec((p_len, 2 * p_len), lambda c, i: (0, 0), pipeline_mode=pl.Buffered(1)),
                  pl.BlockSpec((1, 2 * nb - 1, 2 * p_len, cw), lambda c, i: (order, 0, 0, c),
                               pipeline_mode=pl.Buffered(1)),
                  pl.BlockSpec((1, t, cw), lambda c, i: (i, 0, v_col * nh + c)),
                  pl.BlockSpec((1, t, cw), lambda c, i: (i, 0, gate_col * nh + c)),
                  pl.BlockSpec((1, 1, cw), lambda c, i: (order, 0, c))],
        out_specs=pl.BlockSpec((1, t, cw), lambda c, i: (i, 0, c)),
        out_shape=jax.ShapeDtypeStruct((b, t, BRANCH_W), BF16),
        scratch_shapes=[pltpu.VMEM((nb, 2 * p_len, cw), BF16), pltpu.VMEM((2, 2 * p_len, cw), BF16)],
        compiler_params=_cp("arbitrary", "arbitrary"),
        name="hyena_conv",
    )(a, at, spec, v, gate, bias.reshape(HYENA_ORDER, 1, BRANCH_W))


def _softplus(x):
    return jnp.maximum(x, 0.0) + jnp.log(1.0 + jnp.exp(-jnp.abs(x)))


def _ssd_kernel(xs_ref, b_ref, c_ref, z_ref, dt_ref, dtb_ref, an_ref, dsk_ref, nw_ref, tri_ref, exp_ref, o_ref,
                yf_scr, yb_scr, *, t_len, nc, unroll):
    q = SSM_CHUNK
    gw = SSM_HPG * SSM_HEAD_DIM
    front = q - N_META
    ri = lax.broadcasted_iota(jnp.int32, (q, q), 0)
    ci = lax.broadcasted_iota(jnp.int32, (q, q), 1)
    tri = (ri >= ci, ri <= ci)
    head_of_lane = lax.broadcasted_iota(jnp.int32, (1, gw), 1) // SSM_HEAD_DIM
    lane = lax.broadcasted_iota(jnp.int32, (1, q), 1)
    dt_lane_mask = (lane < 2 * SSM_HPG).astype(F32)
    low_half = lane < SSM_HEAD_DIM
    dtb = dtb_ref[...]
    an = an_ref[...]
    masked_out = -1e30

    def widen(cols):
        return jnp.concatenate([jnp.where(low_half, cols[0], cols[1]), jnp.where(low_half, cols[2], cols[3])], axis=1)

    def cumsum(d, x):
        hi = x.astype(BF16)
        r1 = x - hi.astype(F32)
        mid = r1.astype(BF16)
        lo = (r1 - mid.astype(F32)).astype(BF16)
        s = jnp.dot(tri_ref[d], jnp.concatenate([hi, mid, lo], axis=1), preferred_element_type=F32)
        return s[:, :q] + s[:, q:2 * q] + s[:, 2 * q:]

    def load(ref, r0, nrows):
        return ref[0, pl.ds(r0, nrows), :]

    def dt_of(raw):
        return _softplus(raw.astype(F32) + dtb) * dt_lane_mask

    def chunk_data(c):
        r0 = c * q - front if isinstance(c, int) else pl.multiple_of(c * q - front, HALO)
        return load(xs_ref, r0, q), load(b_ref, r0, q), load(c_ref, r0, q), dt_of(load(dt_ref, r0, q))

    def chunk0_data():
        def pad(x):
            return jnp.concatenate([jnp.zeros((front, x.shape[1]), x.dtype), x], axis=0)
        return (pad(load(xs_ref, 0, N_META)), pad(load(b_ref, 0, N_META)), pad(load(c_ref, 0, N_META)),
                pad(dt_of(load(dt_ref, 0, N_META))))

    def stage_local(job, cs_n):
        (xs, bm, cm, dt), d = job
        lanes = [d * SSM_HPG + r for r in range(SSM_HPG)]
        cs_cols = [jnp.broadcast_to(cs_n[:, ln:ln + 1], (q, q)) for ln in lanes]
        cs_w = widen(cs_cols)
        dt_w = jnp.dot(dt.astype(BF16), exp_ref[d], preferred_element_type=F32).astype(BF16)
        tot = cs_w[q - 1:q, :] if d == 0 else cs_w[0:1, :]
        xt = xs * dt_w
        xd = xt * jnp.exp2(tot - cs_w).astype(BF16)
        s_new = lax.dot_general(bm, xd, (((0,), (0,)), ((), ())), preferred_element_type=F32)
        cb = lax.dot_general(cm, bm, (((1,), (1,)), ((), ())), preferred_element_type=F32)
        return cs_cols, xt, cb, cm, s_new, jnp.exp2(cs_w), jnp.exp2(tot), cs_n.T

    def stage_carry(loc, h):
        _, _, _, cm, s_new, ecs, etot, _ = loc
        y_off = jnp.dot(cm, h.astype(BF16), preferred_element_type=F32) * ecs
        return y_off, h * etot + s_new

    def stage_diag(job, cs_n, loc, y_off):
        d = job[1]
        cs_cols, xt, cb = loc[:3]
        cs_t = loc[7]
        ms, xm = [], []
        for r in range(SSM_HPG):
            ln = d * SSM_HPG + r
            dec = jnp.exp2(jnp.where(tri[d], cs_cols[r] - cs_t[ln:ln + 1, :], masked_out))
            ms.append((cb * dec).astype(BF16))
            xm.append(jnp.where(head_of_lane == r, xt, jnp.zeros_like(xt)))
        return y_off + jnp.dot(jnp.concatenate(ms, axis=1), jnp.concatenate(xm, axis=0), preferred_element_type=F32)

    def process_all(fwd_data, bwd_data, hf, hb):
        jobs = [(x, 0) for x in fwd_data] + [(x, 1) for x in bwd_data]
        cs = [cumsum(d, data[3] * an) for data, d in jobs]
        ys, pending = [], None
        for k, (job, c) in enumerate(zip(jobs, cs)):
            loc = stage_local(job, c)
            if pending is not None:
                ys.append(stage_diag(*pending))
            y_off, h = stage_carry(loc, hf if job[1] == 0 else hb)
            hf, hb = (h, hb) if job[1] == 0 else (hf, h)
            pending = (job, c, loc, y_off)
        ys.append(stage_diag(*pending))
        return ys[:len(fwd_data)], ys[len(fwd_data):], hf, hb

    def finish(y, r_out, nrows):
        y = y + load(xs_ref, r_out, nrows).astype(F32) * dsk_ref[...]
        y = y * _silu(load(z_ref, r_out, nrows).astype(F32))
        y = y * lax.rsqrt(jnp.mean(y * y, axis=-1, keepdims=True) + NORM_EPS)
        o_ref[0, pl.ds(r_out, nrows), :] = (y * nw_ref[...]).astype(o_ref.dtype)

    def rows(c):
        return pl.ds(c * q if isinstance(c, int) else pl.multiple_of(c * q, q), q)

    def scan_step(cf, carry):
        hf, hb = carry
        cb_ = [nc - 1 - c for c in cf]
        data = lambda c: chunk0_data() if isinstance(c, int) and c == 0 else chunk_data(c)
        yf, yb, hf, hb = process_all([data(c) for c in cf], [data(c) for c in cb_], hf, hb)
        for c, y in zip(cf, yf):
            yf_scr[rows(c), :] = y
        for c, y in zip(cb_, yb):
            yb_scr[rows(c), :] = y
        return hf, hb

    n_mid = (nc - unroll - 1) // unroll
    h0 = jnp.zeros((SSM_STATE, gw), F32)
    carry = scan_step(list(range(unroll)), (h0, h0))
    carry = lax.fori_loop(1, 1 + n_mid, lambda k, c: scan_step([k * unroll + j for j in range(unroll)], c), carry)
    scan_step(list(range((1 + n_mid) * unroll, nc)), carry)

    finish(yf_scr[front:q, :] + yb_scr[front:q, :], 0, N_META)

    def finish_body(c, carry):
        finish(yf_scr[rows(c), :] + yb_scr[rows(c), :], pl.multiple_of(c * q - front, HALO), q)
        return carry

    lax.fori_loop(1, nc, finish_body, 0)


def _ssd_branch(xbc, proj, dt_bias, a_log, d_skip, norm_w):
    b, t, _ = xbc.shape
    q = SSM_CHUNK
    gw = SSM_HPG * SSM_HEAD_DIM
    nc = (q - N_META + t) // q
    assert nc * q == q - N_META + t, "sequence length minus meta tokens must be a multiple of the SSD chunk"
    a = -jnp.exp(a_log.astype(F32)).reshape(2, SSM_GROUPS, SSM_HPG)
    dtb = dt_bias.astype(F32).reshape(2, SSM_GROUPS, SSM_HPG)
    pad = lambda v: jnp.pad(jnp.transpose(v, (1, 0, 2)).reshape(SSM_GROUPS, 1, 2 * SSM_HPG),
                            ((0, 0), (0, 0), (0, q - 2 * SSM_HPG)))
    a_n = pad(a * math.log2(math.e))
    dtb_n = pad(dtb)
    ri = jnp.arange(q, dtype=jnp.int32)[:, None]
    ci = jnp.arange(q, dtype=jnp.int32)[None, :]
    tri = jnp.stack([ri >= ci, ri <= ci]).astype(BF16)
    cj = jnp.arange(gw, dtype=jnp.int32)[None, :] // SSM_HEAD_DIM
    expand = jnp.stack([ri == cj, ri == cj + SSM_HPG]).astype(BF16)
    dsk = jnp.repeat(d_skip.astype(F32).reshape(SSM_GROUPS, 1, SSM_HPG), SSM_HEAD_DIM, axis=-1)
    nw = norm_w.astype(F32).reshape(SSM_GROUPS, 1, gw)
    par = lambda shape: pl.BlockSpec((None,) + shape, lambda i, g: (g, 0, 0))
    return pl.pallas_call(
        functools.partial(_ssd_kernel, t_len=t, nc=nc, unroll=min(SSD_UNROLL, nc - 1)),
        grid=(b, SSM_GROUPS),
        in_specs=[pl.BlockSpec((1, t, gw), lambda i, g: (i, 0, g)),
                  pl.BlockSpec((1, t, SSM_STATE), lambda i, g: (i, 0, 4 + g)),
                  pl.BlockSpec((1, t, SSM_STATE), lambda i, g: (i, 0, 6 + g)),
                  pl.BlockSpec((1, t, gw), lambda i, g: (i, 0, COL_Z // gw + g)),
                  pl.BlockSpec((1, t, q), lambda i, g: (i, 0, COL_DT // q + g)),
                  par((1, q)), par((1, q)), par((1, gw)), par((1, gw)),
                  pl.BlockSpec((2, q, q), lambda i, g: (0, 0, 0)),
                  pl.BlockSpec((2, q, gw), lambda i, g: (0, 0, 0))],
        out_specs=pl.BlockSpec((1, t, gw), lambda i, g: (i, 0, g)),
        out_shape=jax.ShapeDtypeStruct((b, t, BRANCH_W), BF16),
        scratch_shapes=[pltpu.VMEM((nc * q, gw), F32), pltpu.VMEM((nc * q, gw), F32)],
        compiler_params=_cp("arbitrary", "arbitrary"),
        name="ssd",
    )(xbc, xbc, xbc, proj, proj, dtb_n, a_n, dsk, nw, tri, expand)


def _merge_kernel(h_ref, y0_ref, y1_ref, y2_ref, g0_ref, g1_ref, g2_ref, g3_ref, bg_ref, cg_ref, xi_ref, cgp_ref,
                  xip_ref, cgn_ref, xin_ref, scw_ref, wb_ref, wo_ref, o_ref, *, tr, t_len):
    u = cg_ref[0].astype(F32) * xi_ref[0].astype(F32)
    prev = _last_row(cgp_ref) * _last_row(xip_ref)
    nxt = _first_row(cgn_ref) * _first_row(xin_ref)
    y_sc = (bg_ref[0].astype(F32) * _conv3(u, prev, nxt, scw_ref[...], pl.program_id(1) * tr, t_len)).astype(BF16)
    merged = None
    for k, (y, g_ref) in enumerate(zip((y0_ref[0], y1_ref[0], y2_ref[0], y_sc), (g0_ref, g1_ref, g2_ref, g3_ref))):
        term = (1.0 + jnp.tanh(g_ref[0].astype(F32))) * jnp.dot(y, wb_ref[k], preferred_element_type=F32)
        merged = term if merged is None else merged + term
    o_ref[0] = h_ref[0] + jnp.dot((0.5 * merged).astype(BF16), wo_ref[...], preferred_element_type=F32)


def _merge(h, ys, proj, sc_w, wb, wo, tr):
    b, t, d = h.shape
    nb = wb.shape[0]
    row = lambda w, c: pl.BlockSpec((1, tr, w), lambda b_, i: (b_, i, c))
    c0 = COL_SC // BRANCH_W
    bg, _, _ = _halo_specs(tr, BRANCH_W, t, c0)
    cg, cgp, cgn = _halo_specs(tr, BRANCH_W, t, c0 + 1)
    xi, xip, xin = _halo_specs(tr, BRANCH_W, t, c0 + 2)
    return pl.pallas_call(
        functools.partial(_merge_kernel, tr=tr, t_len=t),
        grid=(b, pl.cdiv(t, tr)),
        in_specs=[row(d, 0)] + [row(BRANCH_W, 0)] * len(ys) + [row(d, COL_GATE // d + k) for k in range(nb)]
                 + [bg, cg, xi, cgp, xip, cgn, xin, _resident((3, BRANCH_W)), _resident((nb, BRANCH_W, d)),
                    _resident((d, d))],
        out_specs=row(d, 0),
        out_shape=jax.ShapeDtypeStruct((b, t, d), F32),
        compiler_params=_cp("arbitrary", "arbitrary"),
        name="merge",
    )(h, *ys, *([proj] * (nb + 7)), sc_w, wb, wo)


def _rms_rows(x, g):
    return x * lax.rsqrt(jnp.mean(x * x, axis=-1, keepdims=True) + NORM_EPS) * g


def _stage_normed_rows(xs_ref, x_ref, xp_ref, xn_ref, g, row0, tr, t_len):
    li = lax.broadcasted_iota(jnp.int32, (tr, 1), 0)
    xs_ref[HALO:HALO + tr, :] = jnp.where(row0 + li < t_len, _rms_rows(x_ref[0], g), 0.0).astype(BF16)
    xs_ref[0:HALO, :] = jnp.where(row0 > 0, _rms_rows(xp_ref[0], g), 0.0).astype(BF16)
    xs_ref[HALO + tr:, :] = jnp.where(row0 + tr < t_len, _rms_rows(xn_ref[0], g), 0.0).astype(BF16)


def _conv3_rows(u, w):
    n = u.shape[0]
    return pltpu.roll(u, 1, 0) * w[0:1] + u * w[1:2] + pltpu.roll(u, n - 1, 0) * w[2:3]


def _row_halo_specs(tr, d, t_len):
    nh = tr // HALO
    last = t_len // HALO - 1
    return (pl.BlockSpec((1, tr, d), lambda b, i: (b, i, 0)),
            pl.BlockSpec((1, HALO, d), lambda b, i: (b, jnp.maximum(i * nh - 1, 0), 0)),
            pl.BlockSpec((1, HALO, d), lambda b, i: (b, jnp.minimum((i + 1) * nh, last), 0)))


def _resident(shape):
    return pl.BlockSpec(shape, lambda b, i: (0,) * len(shape), pipeline_mode=pl.Buffered(1))


def _col_chunks(n):
    return tuple((s0, min(PROJ_CHUNK, n - s0)) for s0 in range(0, n, PROJ_CHUNK))


def _conv_proj_kernel(x_ref, xp_ref, xn_ref, g_ref, w_ref, c_ref, b_ref, o_ref, xs_ref, *, tr, t_len, n_act):
    _stage_normed_rows(xs_ref, x_ref, xp_ref, xn_ref, g_ref[...], pl.program_id(1) * tr, tr, t_len)
    xs = xs_ref[...]
    for s0, w in _col_chunks(w_ref.shape[1]):
        u = jnp.dot(xs, w_ref[:, s0:s0 + w], preferred_element_type=F32)
        y = _conv3_rows(u, c_ref[:, s0:s0 + w])[HALO:HALO + tr] + b_ref[:, s0:s0 + w]
        o_ref[0, :, s0:s0 + w] = (_silu(y) if s0 < n_act else y).astype(o_ref.dtype)


def _conv_proj(h, g, w, taps, bias, n_act, tr):
    b, t, d = h.shape
    m = w.shape[1]
    main, prev, nxt = _row_halo_specs(tr, d, t)
    return pl.pallas_call(
        functools.partial(_conv_proj_kernel, tr=tr, t_len=t, n_act=n_act),
        grid=(b, pl.cdiv(t, tr)),
        in_specs=[main, prev, nxt, _resident((1, d)), _resident((d, m)), _resident((3, m)), _resident((1, m))],
        out_specs=pl.BlockSpec((1, tr, m), lambda b_, i: (b_, i, 0)),
        out_shape=jax.ShapeDtypeStruct((b, t, m), BF16),
        scratch_shapes=[pltpu.VMEM((tr + 2 * HALO, d), BF16)],
        compiler_params=_cp("arbitrary", "arbitrary"),
        name="conv_proj",
    )(h, h, h, g.reshape(1, d), w, taps, bias)


def _ffn_kernel(x_ref, xp_ref, xn_ref, g_ref, wu_ref, cw_ref, wd_ref, o_ref, xs_ref, gate_ref, *, tr, t_len):
    _stage_normed_rows(xs_ref, x_ref, xp_ref, xn_ref, g_ref[...], pl.program_id(1) * tr, tr, t_len)
    xs = xs_ref[...]
    dff = wd_ref.shape[0]
    for s0, w in _col_chunks(dff):
        a = _conv3_rows(jnp.dot(xs, wu_ref[:, s0:s0 + w], preferred_element_type=F32), cw_ref[:, s0:s0 + w])
        v = _conv3_rows(jnp.dot(xs, wu_ref[:, dff + s0:dff + s0 + w], preferred_element_type=F32),
                        cw_ref[:, dff + s0:dff + s0 + w])
        gate_ref[:, s0:s0 + w] = (_silu(a) * v)[HALO:HALO + tr].astype(BF16)
    o_ref[0] = x_ref[0] + jnp.dot(gate_ref[...], wd_ref[...], preferred_element_type=F32)


def _ffn(h, g, w_up, conv_w, w_down, tr):
    b, t, d = h.shape
    dff = w_down.shape[0]
    main, prev, nxt = _row_halo_specs(tr, d, t)
    return pl.pallas_call(
        functools.partial(_ffn_kernel, tr=tr, t_len=t),
        grid=(b, pl.cdiv(t, tr)),
        in_specs=[main, prev, nxt, _resident((1, d)), _resident((d, 2 * dff)), _resident((3, 2 * dff)),
                  _resident((dff, d))],
        out_specs=pl.BlockSpec((1, tr, d), lambda b_, i: (b_, i, 0)),
        out_shape=jax.ShapeDtypeStruct((b, t, d), F32),
        scratch_shapes=[pltpu.VMEM((tr + 2 * HALO, d), BF16), pltpu.VMEM((tr, dff), BF16)],
        compiler_params=_cp("arbitrary", "arbitrary"),
        name="ffn",
    )(h, h, h, g.reshape(1, d), w_up, conv_w, w_down)


def _final_norm_kernel(x_ref, xn_ref, g_ref, o_ref, *, tr):
    g = g_ref[...]
    o_ref[0, 0:tr - N_META, :] = _rms_rows(x_ref[0], g)[N_META:]
    o_ref[0, tr - N_META:, :] = _rms_rows(xn_ref[0], g)


def _final_norm(x, g, tr):
    b, t, d = x.shape
    assert N_META == HALO
    main, _, nxt = _row_halo_specs(tr, d, t)
    return pl.pallas_call(
        functools.partial(_final_norm_kernel, tr=tr),
        grid=(b, pl.cdiv(t - N_META, tr)),
        in_specs=[main, nxt, pl.BlockSpec((1, d), lambda b_, i: (0, 0))],
        out_specs=pl.BlockSpec((1, tr, d), lambda b_, i: (b_, i, 0)),
        out_shape=jax.ShapeDtypeStruct((b, t - N_META, d), F32),
        compiler_params=_cp("arbitrary", "arbitrary"),
        name="final_norm",
    )(x, x, g.reshape(1, d))


def _prep_w_in(w_in):
    d = w_in.shape[0]
    o_z, o_xbc, o_dt = BRANCH_W, 2 * BRANCH_W, 2 * BRANCH_W + 1024
    ndt = 2 * SSM_GROUPS * SSM_HPG
    o_hy = o_dt + ndt
    o_sc = o_hy + 3 * BRANCH_W
    o_gate = o_sc + 3 * BRANCH_W
    w_dt = w_in[:, o_dt:o_hy].reshape(d, 2, SSM_GROUPS, SSM_HPG)
    w_dt = jnp.transpose(w_dt, (0, 2, 1, 3)).reshape(d, SSM_GROUPS, 2 * SSM_HPG)
    w_dt = jnp.pad(w_dt, ((0, 0), (0, 0), (0, SSM_CHUNK - 2 * SSM_HPG))).reshape(d, SSM_GROUPS * SSM_CHUNK)
    w_conv = jnp.concatenate([w_in[:, o_xbc:o_dt], w_in[:, o_hy:o_sc]], axis=1).astype(BF16)
    w_plain = jnp.concatenate([w_in[:, :o_xbc], 0.5 * w_in[:, o_gate:], w_in[:, o_sc:o_gate], w_dt],
                              axis=1).astype(BF16)
    return w_conv, w_plain


def _hyena_fft_len(t):
    tf = 528
    nfb = -(-(2 * t - 1) // (2 * tf))
    return 2 * tf * nfb, tf


def _run_trunk(x, meta_tokens, norm_final, layers):
    b, seq, d = x.shape
    t = seq + N_META
    meta = jnp.broadcast_to(meta_tokens[None].astype(x.dtype), (b, N_META, d))
    h = jnp.concatenate([meta, x], axis=1)
    tr = _pick_tile(t, ROW_TILE)
    hy_a, hy_at = _hyena_tables(HYENA_BLOCK)
    hy_nb = -(-t // HYENA_BLOCK)
    tmf = tr
    tok = min(TOKEN_TILE, b * t)
    hv, hx1, hx2 = (CV_HY // BRANCH_W + k for k in range(3))
    for p in layers:
        w_conv, w_plain = p["w_in"]
        taps = jnp.concatenate([p["ssm_conv_w"], p["hyena_conv_w"]], axis=1)
        bias = jnp.concatenate([p["ssm_conv_b"], jnp.zeros((3 * BRANCH_W,), F32)]).reshape(1, N_CONV)
        projc = _conv_proj(h, p["norm_mix"], w_conv, taps, bias, CV_HY, tr)
        proj = _norm_matmul(h.reshape(b * t, d), p["norm_mix"], w_plain, min(512, tok), 768).reshape(b, t, N_PLAIN)
        y_fn = _fnet_branch(proj, tmf)
        y_ssm = _ssd_branch(projc, proj, p["ssm_dt_bias"], p["ssm_a_log"], p["ssm_d"], p["ssm_norm"])
        filt = _hyena_filters(t, hy_nb * HYENA_BLOCK, p["hyena_w1"], p["hyena_b1"], p["hyena_w2"], p["hyena_b2"],
                              p["hyena_w3"], p["hyena_freq"])
        spec = _hy_filter_spectrum(hy_a, filt, HYENA_BLOCK, hy_nb)
        z = _hy_long_conv(hy_a, hy_at, spec, 0, projc, hv, projc, hx1, p["hyena_bias"], HYENA_BLOCK)
        y_hy = _hy_long_conv(hy_a, hy_at, spec, 1, z, 0, projc, hx2, p["hyena_bias"], HYENA_BLOCK)
        h = _merge(h, [y_fn, y_ssm, y_hy], proj, p["sc_conv_w"], p["w_branch"], p["w_out"], tr)
        h = _ffn(h, p["norm_ffn"], p["w_up"], p["ffn_conv_w"], p["w_down"], tr)
    return _final_norm(h, norm_final, tr)


def kernel(x_prompt, x_sample, meta_tokens, norm_mix, w_in, ssm_conv_w, ssm_conv_b, ssm_dt_bias, ssm_a_log, ssm_d,
           ssm_norm, hyena_conv_w, hyena_w1, hyena_b1, hyena_w2, hyena_b2, hyena_w3, hyena_freq, hyena_bias,
           sc_conv_w, w_branch, w_out, norm_ffn, ffn_conv_w, w_up, w_down, norm_final):
    depth = w_in.shape[0]
    layers = []
    for l in range(depth):
        layers.append(dict(
            norm_mix=norm_mix[l], w_in=_prep_w_in(w_in[l]), ssm_conv_w=ssm_conv_w[l], ssm_conv_b=ssm_conv_b[l],
            ssm_dt_bias=ssm_dt_bias[l], ssm_a_log=ssm_a_log[l], ssm_d=ssm_d[l], ssm_norm=ssm_norm[l],
            hyena_conv_w=hyena_conv_w[l], hyena_w1=hyena_w1[l], hyena_b1=hyena_b1[l], hyena_w2=hyena_w2[l],
            hyena_b2=hyena_b2[l], hyena_w3=hyena_w3[l], hyena_freq=hyena_freq[l], hyena_bias=hyena_bias[l],
            sc_conv_w=sc_conv_w[l], w_branch=w_branch[l].astype(BF16), w_out=w_out[l].astype(BF16),
            norm_ffn=norm_ffn[l], ffn_conv_w=ffn_conv_w[l], w_up=w_up[l].astype(BF16), w_down=w_down[l].astype(BF16)))
    y_prompt = _run_trunk(x_prompt, meta_tokens, norm_final, layers)
    y_sample = _run_trunk(x_sample, meta_tokens, norm_final, layers)
    return (y_prompt, y_sample)
```

```python
import functools
import math

import jax
import jax.numpy as jnp
from jax import lax
from jax.experimental import pallas as pl
from jax.experimental.pallas import tpu as pltpu

F32 = jnp.float32
BF16 = jnp.bfloat16
HI = lax.Precision.HIGHEST

NORM_EPS = 1e-6
N_META = 16
BRANCH_W = 512
FNET_GW = 128
SSM_GROUPS = 2
SSM_HPG = 4
SSM_HEAD_DIM = 64
SSM_STATE = 128
SSM_CHUNK = 128
HYENA_ORDER = 2
HYENA_EMB = 33
HYENA_TARGET = 1e-2
HYENA_FAST = 0.3
HYENA_SLOW = 1.5

ROW_TILE = 700
TOKEN_TILE = 1024
HYENA_BLOCK = 704
HYENA_FREQ_CHUNK = 32
MXU_COLS = 256
PROJ_CHUNK = 2 * MXU_COLS
SSD_UNROLL = 4
HALO = 16
VMEM_LIMIT = 56 * 1024 * 1024

CV_XBC = 0
CV_HY = 1024
N_CONV = 2560
COL_FN = 0
COL_Z = 512
COL_GATE = 1024
COL_SC = 5120
COL_DT = 6656
N_PLAIN = 6912


def _cp(*sem):
    return pltpu.CompilerParams(dimension_semantics=sem, vmem_limit_bytes=VMEM_LIMIT)


def _pick_tile(n, target, mult=16):
    k = max(1, -(-n // target))
    t = -(-n // k)
    return -(-t // mult) * mult


def _silu(x):
    h = 0.5 * x
    return h * (1.0 + jnp.tanh(h))


def _norm_matmul_kernel(x_ref, g_ref, w_ref, o_ref, *, tn):
    x = x_ref[...]
    ms = jnp.mean(x * x, axis=-1, keepdims=True)
    xn = (x * lax.rsqrt(ms + NORM_EPS) * g_ref[...]).astype(BF16)
    for s0 in range(0, w_ref.shape[1], tn):
        o_ref[:, s0:s0 + tn] = jnp.dot(xn, w_ref[:, s0:s0 + tn], preferred_element_type=F32).astype(o_ref.dtype)


def _norm_matmul(x, g, w, tm, tn):
    n, d = x.shape
    m = w.shape[1]
    assert m % tn == 0
    tm = min(tm, -(-n // HALO) * HALO)
    return pl.pallas_call(
        functools.partial(_norm_matmul_kernel, tn=tn),
        grid=(pl.cdiv(n, tm),),
        in_specs=[pl.BlockSpec((tm, d), lambda i: (i, 0)),
                  pl.BlockSpec((1, d), lambda i: (0, 0), pipeline_mode=pl.Buffered(1)),
                  pl.BlockSpec((d, m), lambda i: (0, 0), pipeline_mode=pl.Buffered(1))],
        out_specs=pl.BlockSpec((tm, m), lambda i: (i, 0)),
        out_shape=jax.ShapeDtypeStruct((n, m), BF16),
        compiler_params=_cp("arbitrary"),
        name="norm_matmul",
    )(x, g.reshape(1, d), w)


def _conv3(x, prev_row, next_row, w, row0, t_len):
    n = x.shape[0]
    li = lax.broadcasted_iota(jnp.int32, (n, 1), 0)
    gi = li + row0
    xm = pltpu.roll(x, 1, 0)
    xm = jnp.where(li == 0, prev_row, xm)
    xm = jnp.where(gi >= 1, xm, 0.0)
    xp = pltpu.roll(x, n - 1, 0)
    xp = jnp.where(li == n - 1, next_row, xp)
    xp = jnp.where(gi + 1 < t_len, xp, 0.0)
    return xm * w[0:1] + x * w[1:2] + xp * w[2:3]


def _last_row(halo_ref):
    return halo_ref[0].astype(F32)[HALO - 1:HALO]


def _first_row(halo_ref):
    return halo_ref[0].astype(F32)[0:1]


def _halo_specs(tr, cw, t_len, col_blk):
    nh = tr // HALO
    last = t_len // HALO - 1
    main = pl.BlockSpec((1, tr, cw), lambda b, i: (b, i, col_blk))
    prev = pl.BlockSpec((1, HALO, cw), lambda b, i: (b, jnp.maximum(i * nh - 1, 0), col_blk))
    nxt = pl.BlockSpec((1, HALO, cw), lambda b, i: (b, jnp.minimum((i + 1) * nh, last), col_blk))
    return main, prev, nxt


def _fnet_kernel(a_ref, u_ref, cs_ref, o_ref, *, tmf):
    p = jnp.dot(a_ref[0], u_ref[0], preferred_element_type=F32)
    pc = p[:tmf].astype(BF16)
    ps = p[tmf:].astype(BF16)
    y = jnp.dot(pc, cs_ref[0], preferred_element_type=F32) - jnp.dot(ps, cs_ref[1], preferred_element_type=F32)
    o_ref[0] = y.astype(o_ref.dtype)


def _fnet_tables(t, tmf):
    nm = -(-t // tmf)
    assert tmf % HALO == 0
    k = jnp.arange(t, dtype=jnp.int32)[None, :]
    ang = lambda j: ((j[:, None] * k) % t).astype(F32) * (2.0 * math.pi / t)
    ang_hi = ang(jnp.arange(nm * tmf // HALO, dtype=jnp.int32) * HALO)[:, None, :]
    ang_lo = ang(jnp.arange(HALO, dtype=jnp.int32))[None, :, :]
    c = (jnp.cos(ang_hi) * jnp.cos(ang_lo) - jnp.sin(ang_hi) * jnp.sin(ang_lo)).reshape(nm * tmf, t)
    s = (jnp.sin(ang_hi) * jnp.cos(ang_lo) + jnp.cos(ang_hi) * jnp.sin(ang_lo)).reshape(nm * tmf, t)
    valid = jnp.arange(nm * tmf, dtype=jnp.int32)[:, None] < t
    c = jnp.where(valid, c, 0.0).astype(BF16).reshape(nm, tmf, t)
    s = jnp.where(valid, s, 0.0).astype(BF16).reshape(nm, tmf, t)
    a = jnp.concatenate([c, s], axis=1)
    jj = jnp.arange(BRANCH_W, dtype=jnp.int32)[:, None]
    kk = jnp.arange(BRANCH_W, dtype=jnp.int32)[None, :]
    same = (jj // FNET_GW) == (kk // FNET_GW)
    ang2 = (((jj % FNET_GW) * (kk % FNET_GW)) % FNET_GW).astype(F32) * (2.0 * math.pi / FNET_GW)
    scale = 1.0 / math.sqrt(t * FNET_GW)
    cc = jnp.where(same, jnp.cos(ang2), 0.0) * scale
    sc = jnp.where(same, jnp.sin(ang2), 0.0) * scale
    return a, jnp.stack([cc, sc]).astype(BF16)


def _fnet_branch(proj, tmf):
    b, t, _ = proj.shape
    a, cs = _fnet_tables(t, tmf)
    nm = a.shape[0]
    return pl.pallas_call(
        functools.partial(_fnet_kernel, tmf=tmf),
        grid=(nm, b),
        in_specs=[pl.BlockSpec((1, 2 * tmf, t), lambda i, j: (i, 0, 0)),
                  pl.BlockSpec((1, t, BRANCH_W), lambda i, j: (j, 0, COL_FN // BRANCH_W)),
                  pl.BlockSpec((2, BRANCH_W, BRANCH_W), lambda i, j: (0, 0, 0))],
        out_specs=pl.BlockSpec((1, tmf, BRANCH_W), lambda i, j: (j, i, 0)),
        out_shape=jax.ShapeDtypeStruct((b, t, BRANCH_W), BF16),
        compiler_params=_cp("arbitrary", "arbitrary"),
        name="fnet",
    )(a, proj, cs)


def _hyena_filter_kernel(w1t_ref, w1c_ref, w1s_ref, b1_ref, w2_ref, b2_ref, fq_ref, w3f_ref, w3b_ref, dl_ref,
                         o_ref, hid_ref, *, t, half):
    row = lax.broadcasted_iota(jnp.int32, (t, 1), 0)
    pos_f = row.astype(F32)
    pos_b = (t - row).astype(F32)
    bands = (HYENA_EMB - 1) // 2

    @pl.when(pl.program_id(0) == 0)
    def _():
        lane = lax.broadcasted_iota(jnp.int32, (1, 2 * bands), 1)
        fr = 1e-4 + (lane % bands).astype(F32) * ((bands - 1 - 1e-4) / (bands - 1))
        pos = jnp.where(lane < bands, pos_f, pos_b)
        arg = (pos * (2.0 * math.pi / t)) * fr
        pre = ((pos_f * (1.0 / (t - 1))) * w1t_ref[0:1, :] + (pos_b * (1.0 / (t - 1))) * w1t_ref[1:2, :]
               + jnp.dot(jnp.cos(arg), w1c_ref[...], precision=HI, preferred_element_type=F32)
               - jnp.dot(jnp.sin(arg), w1s_ref[...], precision=HI, preferred_element_type=F32)
               + b1_ref[...])
        fq = fq_ref[...]
        h1 = jnp.sin(fq * pre)
        hid_ref[...] = jnp.sin(fq * (jnp.dot(h1, w2_ref[...], precision=HI, preferred_element_type=F32)
                                     + b2_ref[...]))

    def taps(w3_ref, pos):
        dec = jnp.exp(-(pos * (1.0 / (t - 1))) * dl_ref[...])
        return jnp.dot(hid_ref[...], w3_ref[...], precision=HI, preferred_element_type=F32) * dec

    hf = taps(w3f_ref, pos_f)
    hb = jnp.where(row >= 1, taps(w3b_ref, pos_b), 0.0)
    l1 = jnp.sum(jnp.abs(hf), axis=0, keepdims=True) + jnp.sum(jnp.abs(hb), axis=0, keepdims=True)
    inv = 1.0 / l1
    cw = o_ref.shape[2]
    o_ref[0, half - t:half, :] = (hb * inv).astype(o_ref.dtype)
    o_ref[0, half:half + t, :] = (hf * inv).astype(o_ref.dtype)
    if half > t:
        o_ref[0, 0:half - t, :] = jnp.zeros((half - t, cw), o_ref.dtype)
        o_ref[0, half + t:, :] = jnp.zeros((half - t, cw), o_ref.dtype)


def _hyena_filters(t, half, w1, b1, w2, b2, w3, freq):
    cw = 256
    nb = BRANCH_W // cw
    bands = (HYENA_EMB - 1) // 2
    nf = w2.shape[0]
    max_decay = math.log(HYENA_TARGET) / HYENA_FAST
    min_decay = math.log(HYENA_TARGET) / HYENA_SLOW
    deltas = jnp.abs(jnp.linspace(min_decay, max_decay, BRANCH_W, dtype=F32)).reshape(1, BRANCH_W)
    full = lambda shape: pl.BlockSpec(shape, lambda g: (0,) * len(shape))
    w1f = w1.astype(F32)
    zero = jnp.zeros_like
    bd = lambda m: jnp.concatenate([jnp.concatenate([m, zero(m)], axis=1), jnp.concatenate([zero(m), m], axis=1)])
    twice = lambda v: jnp.concatenate([v, v]).reshape(1, 2 * nf).astype(F32)
    w1t = bd(w1f[0:1])
    w3f32 = w3.astype(F32)
    w3_top = jnp.concatenate([w3f32, zero(w3f32)], axis=0)
    w3_bot = jnp.concatenate([zero(w3f32), w3f32], axis=0)
    return pl.pallas_call(
        functools.partial(_hyena_filter_kernel, t=t, half=half),
        grid=(HYENA_ORDER * nb,),
        in_specs=[full((2, 2 * nf)), full((2 * bands, 2 * nf)), full((2 * bands, 2 * nf)), full((1, 2 * nf)),
                  full((2 * nf, 2 * nf)), full((1, 2 * nf)), full((1, 2 * nf)),
                  pl.BlockSpec((2 * nf, cw), lambda g: (0, (g // nb) * 2 * nb + g % nb)),
                  pl.BlockSpec((2 * nf, cw), lambda g: (0, (g // nb) * 2 * nb + nb + g % nb)),
                  pl.BlockSpec((1, cw), lambda g: (0, g % nb))],
        out_specs=pl.BlockSpec((1, 2 * half, cw), lambda g: (g // nb, 0, g % nb)),
        out_shape=jax.ShapeDtypeStruct((HYENA_ORDER, 2 * half, BRANCH_W), BF16),
        scratch_shapes=[pltpu.VMEM((t, 2 * nf), F32)],
        compiler_params=_cp("arbitrary"),
        name="hyena_filter",
    )(w1t, bd(w1f[1:1 + bands]), bd(w1f[1 + bands:]), twice(b1), bd(w2.astype(F32)), twice(b2), twice(freq),
      w3_top, w3_bot, deltas)


def _hyena_tables(p_len):
    f = jnp.arange(p_len, dtype=jnp.int32)[:, None]
    n = jnp.arange(p_len, dtype=jnp.int32)[None, :]
    ang = (((2 * f + 1) * n) % (4 * p_len)).astype(F32) * (math.pi / (2 * p_len))
    a = jnp.concatenate([jnp.cos(ang), -jnp.sin(ang)], axis=0).astype(BF16)
    return a, a.T


def _hy_spec_kernel(a_ref, seg_ref, prev_ref, o_ref, *, p_len):
    g = jnp.dot(a_ref[...], seg_ref[0, 0], preferred_element_type=F32)
    gp = jnp.dot(a_ref[...], prev_ref[0, 0], preferred_element_type=F32)
    first = prev_ref[0, 0, 0:1, :].astype(F32)
    f = lax.broadcasted_iota(jnp.int32, (p_len, 1), 0)
    sign = jnp.where(f % 2 == 0, 1.0, -1.0)
    scale = 1.0 / p_len
    o_ref[0, 0, :p_len] = ((g[:p_len] - sign * gp[p_len:]) * scale).astype(o_ref.dtype)
    o_ref[0, 0, p_len:] = ((g[p_len:] + sign * (gp[:p_len] - first)) * scale).astype(o_ref.dtype)


def _hy_filter_spectrum(a, filt, p_len, nb):
    order, _, w = filt.shape
    lags = filt.reshape(order, 2 * nb, p_len, w)
    return pl.pallas_call(
        functools.partial(_hy_spec_kernel, p_len=p_len),
        grid=(order, 2 * nb - 1),
        in_specs=[pl.BlockSpec((2 * p_len, p_len), lambda o, d: (0, 0)),
                  pl.BlockSpec((1, 1, p_len, w), lambda o, d: (o, d + 1, 0, 0)),
                  pl.BlockSpec((1, 1, p_len, w), lambda o, d: (o, d, 0, 0))],
        out_specs=pl.BlockSpec((1, 1, 2 * p_len, w), lambda o, d: (o, d, 0, 0)),
        out_shape=jax.ShapeDtypeStruct((order, 2 * nb - 1, 2 * p_len, w), BF16),
        compiler_params=_cp("arbitrary", "arbitrary"),
        name="hyena_filter_spectrum",
    )(a, lags, lags)


def _hy_conv_kernel(a_ref, at_ref, k_ref, v_ref, g_ref, bias_ref, o_ref, u_scr, y_scr, *, t_len, p_len, nb, fc):
    cw = v_ref.shape[2]

    def block_rows(i):
        return i * p_len, min(p_len, t_len - i * p_len)

    for j in range(nb):
        r0, n = block_rows(j)
        u = v_ref[0, r0:r0 + n, :]
        if n < p_len:
            u = jnp.concatenate([u, jnp.zeros((p_len - n, cw), u.dtype)], axis=0)
        u_scr[j, :p_len] = jnp.dot(a_ref[:p_len], u, preferred_element_type=F32).astype(BF16)
        u_scr[j, p_len:] = jnp.dot(a_ref[p_len:], u, preferred_element_type=F32).astype(BF16)

    for i in range(nb):
        for r in range(0, p_len, fc):
            acc_r = acc_i = None
            for j in range(nb):
                d = i - j + nb - 1
                ur, ui = u_scr[j, r:r + fc, :], u_scr[j, p_len + r:p_len + r + fc, :]
                kr, ki = k_ref[0, d, r:r + fc, :], k_ref[0, d, p_len + r:p_len + r + fc, :]
                tr_, ti_ = ur * kr - ui * ki, ur * ki + ui * kr
                acc_r, acc_i = (tr_, ti_) if acc_r is None else (acc_r + tr_, acc_i + ti_)
            y_scr[i % 2, r:r + fc, :] = acc_r
            y_scr[i % 2, p_len + r:p_len + r + fc, :] = acc_i
        conv = jnp.dot(at_ref[...], y_scr[i % 2], preferred_element_type=F32)
        r0, n = block_rows(i)
        v = v_ref[0, r0:r0 + n, :].astype(F32)
        o_ref[0, r0:r0 + n, :] = (g_ref[0, r0:r0 + n, :].astype(F32) * (conv[:n] + v * bias_ref[0])).astype(o_ref.dtype)


def _hy_long_conv(a, at, spec, order, v, v_col, gate, gate_col, bias, p_len):
    b, t, _ = v.shape
    nb = -(-t // p_len)
    cw = BRANCH_W // 2
    nh = BRANCH_W // cw
    return pl.pallas_call(
        functools.partial(_hy_conv_kernel, t_len=t, p_len=p_len, nb=nb, fc=HYENA_FREQ_CHUNK),
        grid=(nh, b),
        in_specs=[pl.BlockSpec((2 * p_len, p_len), lambda c, i: (0, 0), pipeline_mode=pl.Buffered(1)),
                  pl.BlockSpec((p_len, 2 * p_len), lambda c, i: (0, 0), pipeline_mode=pl.Buffered(1)),
                  pl.BlockSpec((1, 2 * nb - 1, 2 * p_len, cw), lambda c, i: (order, 0, 0, c),
                               pipeline_mode=pl.Buffered(1)),
                  pl.BlockSpec((1, t, cw), lambda c, i: (i, 0, v_col * nh + c)),
                  pl.BlockSpec((1, t, cw), lambda c, i: (i, 0, gate_col * nh + c)),
                  pl.BlockSpec((1, 1, cw), lambda c, i: (order, 0, c))],
        out_specs=pl.BlockSpec((1, t, cw), lambda c, i: (i, 0, c)),
        out_shape=jax.ShapeDtypeStruct((b, t, BRANCH_W), BF16),
        scratch_shapes=[pltpu.VMEM((nb, 2 * p_len, cw), BF16), pltpu.VMEM((2, 2 * p_len, cw), BF16)],
        compiler_params=_cp("arbitrary", "arbitrary"),
        name="hyena_conv",
    )(a, at, spec, v, gate, bias.reshape(HYENA_ORDER, 1, BRANCH_W))


def _softplus(x):
    return jnp.maximum(x, 0.0) + jnp.log(1.0 + jnp.exp(-jnp.abs(x)))


def _ssd_kernel(xs_ref, b_ref, c_ref, z_ref, dt_ref, dtb_ref, an_ref, dsk_ref, nw_ref, tri_ref, exp_ref, o_ref,
                yf_scr, yb_scr, *, t_len, nc, unroll):
    q = SSM_CHUNK
    gw = SSM_HPG * SSM_HEAD_DIM
    front = q - N_META
    ri = lax.broadcasted_iota(jnp.int32, (q, q), 0)
    ci = lax.broadcasted_iota(jnp.int32, (q, q), 1)
    tri = (ri >= ci, ri <= ci)
    head_of_lane = lax.broadcasted_iota(jnp.int32, (1, gw), 1) // SSM_HEAD_DIM
    lane = lax.broadcasted_iota(jnp.int32, (1, q), 1)
    dt_lane_mask = (lane < 2 * SSM_HPG).astype(F32)
    low_half = lane < SSM_HEAD_DIM
    dtb = dtb_ref[...]
    an = an_ref[...]
    masked_out = -1e30

    def widen(cols):
        return jnp.concatenate([jnp.where(low_half, cols[0], cols[1]), jnp.where(low_half, cols[2], cols[3])], axis=1)

    def cumsum(d, x):
        hi = x.astype(BF16)
        r1 = x - hi.astype(F32)
        mid = r1.astype(BF16)
        lo = (r1 - mid.astype(F32)).astype(BF16)
        s = jnp.dot(tri_ref[d], jnp.concatenate([hi, mid, lo], axis=1), preferred_element_type=F32)
        return s[:, :q] + s[:, q:2 * q] + s[:, 2 * q:]

    def load(ref, r0, nrows):
        return ref[0, pl.ds(r0, nrows), :]

    def dt_of(raw):
        return _softplus(raw.astype(F32) + dtb) * dt_lane_mask

    def chunk_data(c):
        r0 = c * q - front if isinstance(c, int) else pl.multiple_of(c * q - front, HALO)
        return load(xs_ref, r0, q), load(b_ref, r0, q), load(c_ref, r0, q), dt_of(load(dt_ref, r0, q))

    def chunk0_data():
        def pad(x):
            return jnp.concatenate([jnp.zeros((front, x.shape[1]), x.dtype), x], axis=0)
        return (pad(load(xs_ref, 0, N_META)), pad(load(b_ref, 0, N_META)), pad(load(c_ref, 0, N_META)),
                pad(dt_of(load(dt_ref, 0, N_META))))

    def stage_local(job, cs_n):
        (xs, bm, cm, dt), d = job
        lanes = [d * SSM_HPG + r for r in range(SSM_HPG)]
        cs_cols = [jnp.broadcast_to(cs_n[:, ln:ln + 1], (q, q)) for ln in lanes]
        cs_w = widen(cs_cols)
        dt_w = jnp.dot(dt.astype(BF16), exp_ref[d], preferred_element_type=F32).astype(BF16)
        tot = cs_w[q - 1:q, :] if d == 0 else cs_w[0:1, :]
        xt = xs * dt_w
        xd = xt * jnp.exp2(tot - cs_w).astype(BF16)
        s_new = lax.dot_general(bm, xd, (((0,), (0,)), ((), ())), preferred_element_type=F32)
        cb = lax.dot_general(cm, bm, (((1,), (1,)), ((), ())), preferred_element_type=F32)
        return cs_cols, xt, cb, cm, s_new, jnp.exp2(cs_w), jnp.exp2(tot), cs_n.T

    def stage_carry(loc, h):
        _, _, _, cm, s_new, ecs, etot, _ = loc
        y_off = jnp.dot(cm, h.astype(BF16), preferred_element_type=F32) * ecs
        return y_off, h * etot + s_new

    def stage_diag(job, cs_n, loc, y_off):
        d = job[1]
        cs_cols, xt, cb = loc[:3]
        cs_t = loc[7]
        ms, xm = [], []
        for r in range(SSM_HPG):
            ln = d * SSM_HPG + r
            dec = jnp.exp2(jnp.where(tri[d], cs_cols[r] - cs_t[ln:ln + 1, :], masked_out))
            ms.append((cb * dec).astype(BF16))
            xm.append(jnp.where(head_of_lane == r, xt, jnp.zeros_like(xt)))
        return y_off + jnp.dot(jnp.concatenate(ms, axis=1), jnp.concatenate(xm, axis=0), preferred_element_type=F32)

    def process_all(fwd_data, bwd_data, hf, hb):
        jobs = [(x, 0) for x in fwd_data] + [(x, 1) for x in bwd_data]
        cs = [cumsum(d, data[3] * an) for data, d in jobs]
        ys, pending = [], None
        for k, (job, c) in enumerate(zip(jobs, cs)):
            loc = stage_local(job, c)
            if pending is not None:
                ys.append(stage_diag(*pending))
            y_off, h = stage_carry(loc, hf if job[1] == 0 else hb)
            hf, hb = (h, hb) if job[1] == 0 else (hf, h)
            pending = (job, c, loc, y_off)
        ys.append(stage_diag(*pending))
        return ys[:len(fwd_data)], ys[len(fwd_data):], hf, hb

    def finish(y, r_out, nrows):
        y = y + load(xs_ref, r_out, nrows).astype(F32) * dsk_ref[...]
        y = y * _silu(load(z_ref, r_out, nrows).astype(F32))
        y = y * lax.rsqrt(jnp.mean(y * y, axis=-1, keepdims=True) + NORM_EPS)
        o_ref[0, pl.ds(r_out, nrows), :] = (y * nw_ref[...]).astype(o_ref.dtype)

    def rows(c):
        return pl.ds(c * q if isinstance(c, int) else pl.multiple_of(c * q, q), q)

    def scan_step(cf, carry):
        hf, hb = carry
        cb_ = [nc - 1 - c for c in cf]
        data = lambda c: chunk0_data() if isinstance(c, int) and c == 0 else chunk_data(c)
        yf, yb, hf, hb = process_all([data(c) for c in cf], [data(c) for c in cb_], hf, hb)
        for c, y in zip(cf, yf):
            yf_scr[rows(c), :] = y
        for c, y in zip(cb_, yb):
            yb_scr[rows(c), :] = y
        return hf, hb

    n_mid = (nc - unroll - 1) // unroll
    h0 = jnp.zeros((SSM_STATE, gw), F32)
    carry = scan_step(list(range(unroll)), (h0, h0))
    carry = lax.fori_loop(1, 1 + n_mid, lambda k, c: scan_step([k * unroll + j for j in range(unroll)], c), carry)
    scan_step(list(range((1 + n_mid) * unroll, nc)), carry)

    finish(yf_scr[front:q, :] + yb_scr[front:q, :], 0, N_META)

    def finish_body(c, carry):
        finish(yf_scr[rows(c), :] + yb_scr[rows(c), :], pl.multiple_of(c * q - front, HALO), q)
        return carry

    lax.fori_loop(1, nc, finish_body, 0)


def _ssd_branch(xbc, proj, dt_bias, a_log, d_skip, norm_w):
    b, t, _ = xbc.shape
    q = SSM_CHUNK
    gw = SSM_HPG * SSM_HEAD_DIM
    nc = (q - N_META + t) // q
    assert nc * q == q - N_META + t, "sequence length minus meta tokens must be a multiple of the SSD chunk"
    a = -jnp.exp(a_log.astype(F32)).reshape(2, SSM_GROUPS, SSM_HPG)
    dtb = dt_bias.astype(F32).reshape(2, SSM_GROUPS, SSM_HPG)
    pad = lambda v: jnp.pad(jnp.transpose(v, (1, 0, 2)).reshape(SSM_GROUPS, 1, 2 * SSM_HPG),
                            ((0, 0), (0, 0), (0, q - 2 * SSM_HPG)))
    a_n = pad(a * math.log2(math.e))
    dtb_n = pad(dtb)
    ri = jnp.arange(q, dtype=jnp.int32)[:, None]
    ci = jnp.arange(q, dtype=jnp.int32)[None, :]
    tri = jnp.stack([ri >= ci, ri <= ci]).astype(BF16)
    cj = jnp.arange(gw, dtype=jnp.int32)[None, :] // SSM_HEAD_DIM
    expand = jnp.stack([ri == cj, ri == cj + SSM_HPG]).astype(BF16)
    dsk = jnp.repeat(d_skip.astype(F32).reshape(SSM_GROUPS, 1, SSM_HPG), SSM_HEAD_DIM, axis=-1)
    nw = norm_w.astype(F32).reshape(SSM_GROUPS, 1, gw)
    par = lambda shape: pl.BlockSpec((None,) + shape, lambda i, g: (g, 0, 0))
    return pl.pallas_call(
        functools.partial(_ssd_kernel, t_len=t, nc=nc, unroll=min(SSD_UNROLL, nc - 1)),
        grid=(b, SSM_GROUPS),
        in_specs=[pl.BlockSpec((1, t, gw), lambda i, g: (i, 0, g)),
                  pl.BlockSpec((1, t, SSM_STATE), lambda i, g: (i, 0, 4 + g)),
                  pl.BlockSpec((1, t, SSM_STATE), lambda i, g: (i, 0, 6 + g)),
                  pl.BlockSpec((1, t, gw), lambda i, g: (i, 0, COL_Z // gw + g)),
                  pl.BlockSpec((1, t, q), lambda i, g: (i, 0, COL_DT // q + g)),
                  par((1, q)), par((1, q)), par((1, gw)), par((1, gw)),
                  pl.BlockSpec((2, q, q), lambda i, g: (0, 0, 0)),
                  pl.BlockSpec((2, q, gw), lambda i, g: (0, 0, 0))],
        out_specs=pl.BlockSpec((1, t, gw), lambda i, g: (i, 0, g)),
        out_shape=jax.ShapeDtypeStruct((b, t, BRANCH_W), BF16),
        scratch_shapes=[pltpu.VMEM((nc * q, gw), F32), pltpu.VMEM((nc * q, gw), F32)],
        compiler_params=_cp("arbitrary", "arbitrary"),
        name="ssd",
    )(xbc, xbc, xbc, proj, proj, dtb_n, a_n, dsk, nw, tri, expand)


def _merge_kernel(h_ref, y0_ref, y1_ref, y2_ref, g0_ref, g1_ref, g2_ref, g3_ref, bg_ref, cg_ref, xi_ref, cgp_ref,
                  xip_ref, cgn_ref, xin_ref, scw_ref, wb_ref, wo_ref, o_ref, m_scr, *, tr, t_len):
    u = cg_ref[0].astype(F32) * xi_ref[0].astype(F32)
    prev = _last_row(cgp_ref) * _last_row(xip_ref)
    nxt = _first_row(cgn_ref) * _first_row(xin_ref)
    y_sc = (bg_ref[0].astype(F32) * _conv3(u, prev, nxt, scw_ref[...], pl.program_id(1) * tr, t_len)).astype(BF16)
    ys = (y0_ref[0], y1_ref[0], y2_ref[0], y_sc)
    gs = (g0_ref, g1_ref, g2_ref, g3_ref)
    for s0, w in _col_chunks(h_ref.shape[2]):
        merged = None
        for k in range(len(ys)):
            term = ((1.0 + jnp.tanh(gs[k][0, :, s0:s0 + w].astype(F32)))
                    * jnp.dot(ys[k], wb_ref[k, :, s0:s0 + w], preferred_element_type=F32))
            merged = term if merged is None else merged + term
        m_scr[:, s0:s0 + w] = merged.astype(BF16)
    o_ref[0] = h_ref[0] + jnp.dot(m_scr[...], wo_ref[...], preferred_element_type=F32)


def _merge(h, ys, proj, sc_w, wb, wo, tr):
    b, t, d = h.shape
    nb = wb.shape[0]
    row = lambda w, c: pl.BlockSpec((1, tr, w), lambda b_, i: (b_, i, c))
    c0 = COL_SC // BRANCH_W
    bg, _, _ = _halo_specs(tr, BRANCH_W, t, c0)
    cg, cgp, cgn = _halo_specs(tr, BRANCH_W, t, c0 + 1)
    xi, xip, xin = _halo_specs(tr, BRANCH_W, t, c0 + 2)
    return pl.pallas_call(
        functools.partial(_merge_kernel, tr=tr, t_len=t),
        grid=(b, pl.cdiv(t, tr)),
        in_specs=[row(d, 0)] + [row(BRANCH_W, 0)] * len(ys) + [row(d, COL_GATE // d + k) for k in range(nb)]
                 + [bg, cg, xi, cgp, xip, cgn, xin, _resident((3, BRANCH_W)), _resident((nb, BRANCH_W, d)),
                    _resident((d, d))],
        out_specs=row(d, 0),
        out_shape=jax.ShapeDtypeStruct((b, t, d), F32),
        scratch_shapes=[pltpu.VMEM((tr, d), BF16)],
        compiler_params=_cp("arbitrary", "arbitrary"),
        name="merge",
    )(h, *ys, *([proj] * (nb + 7)), sc_w, wb, wo)


def _rms_rows(x, g):
    return x * lax.rsqrt(jnp.mean(x * x, axis=-1, keepdims=True) + NORM_EPS) * g


def _stage_normed_rows(xs_ref, x_ref, xp_ref, xn_ref, g, row0, tr, t_len):
    li = lax.broadcasted_iota(jnp.int32, (tr, 1), 0)
    xs_ref[HALO:HALO + tr, :] = jnp.where(row0 + li < t_len, _rms_rows(x_ref[0], g), 0.0).astype(BF16)
    xs_ref[0:HALO, :] = jnp.where(row0 > 0, _rms_rows(xp_ref[0], g), 0.0).astype(BF16)
    xs_ref[HALO + tr:, :] = jnp.where(row0 + tr < t_len, _rms_rows(xn_ref[0], g), 0.0).astype(BF16)


def _conv3_rows(u, w):
    n = u.shape[0]
    return pltpu.roll(u, 1, 0) * w[0:1] + u * w[1:2] + pltpu.roll(u, n - 1, 0) * w[2:3]


def _row_halo_specs(tr, d, t_len):
    nh = tr // HALO
    last = t_len // HALO - 1
    return (pl.BlockSpec((1, tr, d), lambda b, i: (b, i, 0)),
            pl.BlockSpec((1, HALO, d), lambda b, i: (b, jnp.maximum(i * nh - 1, 0), 0)),
            pl.BlockSpec((1, HALO, d), lambda b, i: (b, jnp.minimum((i + 1) * nh, last), 0)))


def _resident(shape):
    return pl.BlockSpec(shape, lambda b, i: (0,) * len(shape), pipeline_mode=pl.Buffered(1))


def _col_chunks(n):
    return tuple((s0, min(PROJ_CHUNK, n - s0)) for s0 in range(0, n, PROJ_CHUNK))


def _conv_proj_kernel(x_ref, xp_ref, xn_ref, g_ref, w_ref, c_ref, b_ref, o_ref, xs_ref, *, tr, t_len, n_act):
    _stage_normed_rows(xs_ref, x_ref, xp_ref, xn_ref, g_ref[...], pl.program_id(1) * tr, tr, t_len)
    xs = xs_ref[...]
    for s0, w in _col_chunks(w_ref.shape[1]):
        u = jnp.dot(xs, w_ref[:, s0:s0 + w], preferred_element_type=F32)
        y = _conv3_rows(u, c_ref[:, s0:s0 + w])[HALO:HALO + tr] + b_ref[:, s0:s0 + w]
        o_ref[0, :, s0:s0 + w] = (_silu(y) if s0 < n_act else y).astype(o_ref.dtype)


def _conv_proj(h, g, w, taps, bias, n_act, tr):
    b, t, d = h.shape
    m = w.shape[1]
    main, prev, nxt = _row_halo_specs(tr, d, t)
    return pl.pallas_call(
        functools.partial(_conv_proj_kernel, tr=tr, t_len=t, n_act=n_act),
        grid=(b, pl.cdiv(t, tr)),
        in_specs=[main, prev, nxt, _resident((1, d)), _resident((d, m)), _resident((3, m)), _resident((1, m))],
        out_specs=pl.BlockSpec((1, tr, m), lambda b_, i: (b_, i, 0)),
        out_shape=jax.ShapeDtypeStruct((b, t, m), BF16),
        scratch_shapes=[pltpu.VMEM((tr + 2 * HALO, d), BF16)],
        compiler_params=_cp("arbitrary", "arbitrary"),
        name="conv_proj",
    )(h, h, h, g.reshape(1, d), w, taps, bias)


def _ffn_kernel(x_ref, xp_ref, xn_ref, g_ref, wu_ref, cw_ref, wd_ref, o_ref, xs_ref, gate_ref, *, tr, t_len):
    _stage_normed_rows(xs_ref, x_ref, xp_ref, xn_ref, g_ref[...], pl.program_id(1) * tr, tr, t_len)
    xs = xs_ref[...]
    dff = wd_ref.shape[0]
    for s0, w in _col_chunks(dff):
        a = _conv3_rows(jnp.dot(xs, wu_ref[:, s0:s0 + w], preferred_element_type=F32), cw_ref[:, s0:s0 + w])
        v = _conv3_rows(jnp.dot(xs, wu_ref[:, dff + s0:dff + s0 + w], preferred_element_type=F32),
                        cw_ref[:, dff + s0:dff + s0 + w])
        gate_ref[:, s0:s0 + w] = (_silu(a) * v)[HALO:HALO + tr].astype(BF16)
    o_ref[0] = x_ref[0] + jnp.dot(gate_ref[...], wd_ref[...], preferred_element_type=F32)


def _ffn(h, g, w_up, conv_w, w_down, tr):
    b, t, d = h.shape
    dff = w_down.shape[0]
    main, prev, nxt = _row_halo_specs(tr, d, t)
    return pl.pallas_call(
        functools.partial(_ffn_kernel, tr=tr, t_len=t),
        grid=(b, pl.cdiv(t, tr)),
        in_specs=[main, prev, nxt, _resident((1, d)), _resident((d, 2 * dff)), _resident((3, 2 * dff)),
                  _resident((dff, d))],
        out_specs=pl.BlockSpec((1, tr, d), lambda b_, i: (b_, i, 0)),
        out_shape=jax.ShapeDtypeStruct((b, t, d), F32),
        scratch_shapes=[pltpu.VMEM((tr + 2 * HALO, d), BF16), pltpu.VMEM((tr, dff), BF16)],
        compiler_params=_cp("arbitrary", "arbitrary"),
        name="ffn",
    )(h, h, h, g.reshape(1, d), w_up, conv_w, w_down)


def _final_norm_kernel(x_ref, xn_ref, g_ref, o_ref, *, tr):
    g = g_ref[...]
    o_ref[0, 0:tr - N_META, :] = _rms_rows(x_ref[0], g)[N_META:]
    o_ref[0, tr - N_META:, :] = _rms_rows(xn_ref[0], g)


def _final_norm(x, g, tr):
    b, t, d = x.shape
    assert N_META == HALO
    main, _, nxt = _row_halo_specs(tr, d, t)
    return pl.pallas_call(
        functools.partial(_final_norm_kernel, tr=tr),
        grid=(b, pl.cdiv(t - N_META, tr)),
        in_specs=[main, nxt, pl.BlockSpec((1, d), lambda b_, i: (0, 0))],
        out_specs=pl.BlockSpec((1, tr, d), lambda b_, i: (b_, i, 0)),
        out_shape=jax.ShapeDtypeStruct((b, t - N_META, d), F32),
        compiler_params=_cp("arbitrary", "arbitrary"),
        name="final_norm",
    )(x, x, g.reshape(1, d))


def _prep_w_in(w_in):
    d = w_in.shape[0]
    o_z, o_xbc, o_dt = BRANCH_W, 2 * BRANCH_W, 2 * BRANCH_W + 1024
    ndt = 2 * SSM_GROUPS * SSM_HPG
    o_hy = o_dt + ndt
    o_sc = o_hy + 3 * BRANCH_W
    o_gate = o_sc + 3 * BRANCH_W
    w_dt = w_in[:, o_dt:o_hy].reshape(d, 2, SSM_GROUPS, SSM_HPG)
    w_dt = jnp.transpose(w_dt, (0, 2, 1, 3)).reshape(d, SSM_GROUPS, 2 * SSM_HPG)
    w_dt = jnp.pad(w_dt, ((0, 0), (0, 0), (0, SSM_CHUNK - 2 * SSM_HPG))).reshape(d, SSM_GROUPS * SSM_CHUNK)
    w_conv = jnp.concatenate([w_in[:, o_xbc:o_dt], w_in[:, o_hy:o_sc]], axis=1).astype(BF16)
    w_plain = jnp.concatenate([w_in[:, :o_xbc], 0.5 * w_in[:, o_gate:], w_in[:, o_sc:o_gate], w_dt],
                              axis=1).astype(BF16)
    return w_conv, w_plain


def _hyena_fft_len(t):
    tf = 528
    nfb = -(-(2 * t - 1) // (2 * tf))
    return 2 * tf * nfb, tf


def _run_trunk(x, meta_tokens, norm_final, layers):
    b, seq, d = x.shape
    t = seq + N_META
    meta = jnp.broadcast_to(meta_tokens[None].astype(x.dtype), (b, N_META, d))
    h = jnp.concatenate([meta, x], axis=1)
    tr = _pick_tile(t, ROW_TILE)
    hy_a, hy_at = _hyena_tables(HYENA_BLOCK)
    hy_nb = -(-t // HYENA_BLOCK)
    tmf = tr
    tok = min(TOKEN_TILE, b * t)
    hv, hx1, hx2 = (CV_HY // BRANCH_W + k for k in range(3))
    for p in layers:
        w_conv, w_plain = p["w_in"]
        taps = jnp.concatenate([p["ssm_conv_w"], p["hyena_conv_w"]], axis=1)
        bias = jnp.concatenate([p["ssm_conv_b"], jnp.zeros((3 * BRANCH_W,), F32)]).reshape(1, N_CONV)
        projc = _conv_proj(h, p["norm_mix"], w_conv, taps, bias, CV_HY, tr)
        proj = _norm_matmul(h.reshape(b * t, d), p["norm_mix"], w_plain, min(512, tok), 768).reshape(b, t, N_PLAIN)
        y_fn = _fnet_branch(proj, tmf)
        y_ssm = _ssd_branch(projc, proj, p["ssm_dt_bias"], p["ssm_a_log"], p["ssm_d"], p["ssm_norm"])
        filt = _hyena_filters(t, hy_nb * HYENA_BLOCK, p["hyena_w1"], p["hyena_b1"], p["hyena_w2"], p["hyena_b2"],
                              p["hyena_w3"], p["hyena_freq"])
        spec = _hy_filter_spectrum(hy_a, filt, HYENA_BLOCK, hy_nb)
        z = _hy_long_conv(hy_a, hy_at, spec, 0, projc, hv, projc, hx1, p["hyena_bias"], HYENA_BLOCK)
        y_hy = _hy_long_conv(hy_a, hy_at, spec, 1, z, 0, projc, hx2, p["hyena_bias"], HYENA_BLOCK)
        h = _merge(h, [y_fn, y_ssm, y_hy], proj, p["sc_conv_w"], p["w_branch"], p["w_out"], tr)
        h = _ffn(h, p["norm_ffn"], p["w_up"], p["ffn_conv_w"], p["w_down"], tr)
    return _final_norm(h, norm_final, tr)


def kernel(x_prompt, x_sample, meta_tokens, norm_mix, w_in, ssm_conv_w, ssm_conv_b, ssm_dt_bias, ssm_a_log, ssm_d,
           ssm_norm, hyena_conv_w, hyena_w1, hyena_b1, hyena_w2, hyena_b2, hyena_w3, hyena_freq, hyena_bias,
           sc_conv_w, w_branch, w_out, norm_ffn, ffn_conv_w, w_up, w_down, norm_final):
    depth = w_in.shape[0]
    layers = []
    for l in range(depth):
        layers.append(dict(
            norm_mix=norm_mix[l], w_in=_prep_w_in(w_in[l]), ssm_conv_w=ssm_conv_w[l], ssm_conv_b=ssm_conv_b[l],
            ssm_dt_bias=ssm_dt_bias[l], ssm_a_log=ssm_a_log[l], ssm_d=ssm_d[l], ssm_norm=ssm_norm[l],
            hyena_conv_w=hyena_conv_w[l], hyena_w1=hyena_w1[l], hyena_b1=hyena_b1[l], hyena_w2=hyena_w2[l],
            hyena_b2=hyena_b2[l], hyena_w3=hyena_w3[l], hyena_freq=hyena_freq[l], hyena_bias=hyena_bias[l],
            sc_conv_w=sc_conv_w[l], w_branch=w_branch[l].astype(BF16), w_out=(0.5 * w_out[l]).astype(BF16),
            norm_ffn=norm_ffn[l], ffn_conv_w=ffn_conv_w[l], w_up=w_up[l].astype(BF16), w_down=w_down[l].astype(BF16)))
    y_prompt = _run_trunk(x_prompt, meta_tokens, norm_final, layers)
    y_sample = _run_trunk(x_sample, meta_tokens, norm_final, layers)
    return (y_prompt, y_sample)
```

```python
import functools
import math

import jax
import jax.numpy as jnp
from jax import lax
from jax.experimental import pallas as pl
from jax.experimental.pallas import tpu as pltpu

F32 = jnp.float32
BF16 = jnp.bfloat16
HI = lax.Precision.HIGHEST

NORM_EPS = 1e-6
N_META = 16
BRANCH_W = 512
FNET_GW = 128
SSM_GROUPS = 2
SSM_HPG = 4
SSM_HEAD_DIM = 64
SSM_STATE = 128
SSM_CHUNK = 128
HYENA_ORDER = 2
HYENA_EMB = 33
HYENA_TARGET = 1e-2
HYENA_FAST = 0.3
HYENA_SLOW = 1.5

ROW_TILE = 700
TOKEN_TILE = 1024
HYENA_BLOCK = 704
HYENA_FREQ_CHUNK = 32
MXU_COLS = 256
PROJ_CHUNK = 2 * MXU_COLS
SSD_UNROLL = 4
HALO = 16
VMEM_LIMIT = 56 * 1024 * 1024

CV_XBC = 0
CV_HY = 1024
N_CONV = 2560
COL_FN = 0
COL_Z = 512
COL_GATE = 1024
COL_SC = 5120
COL_DT = 6656
N_PLAIN = 6912


def _cp(*sem):
    return pltpu.CompilerParams(dimension_semantics=sem, vmem_limit_bytes=VMEM_LIMIT)


def _pick_tile(n, target, mult=16):
    k = max(1, -(-n // target))
    t = -(-n // k)
    return -(-t // mult) * mult


def _silu(x):
    h = 0.5 * x
    return h * (1.0 + jnp.tanh(h))


def _norm_matmul_kernel(x_ref, g_ref, w_ref, o_ref, *, tn):
    x = x_ref[...]
    ms = jnp.mean(x * x, axis=-1, keepdims=True)
    xn = (x * lax.rsqrt(ms + NORM_EPS) * g_ref[...]).astype(BF16)
    for s0 in range(0, w_ref.shape[1], tn):
        o_ref[:, s0:s0 + tn] = jnp.dot(xn, w_ref[:, s0:s0 + tn], preferred_element_type=F32).astype(o_ref.dtype)


def _norm_matmul(x, g, w, tm, tn):
    n, d = x.shape
    m = w.shape[1]
    assert m % tn == 0
    tm = min(tm, -(-n // HALO) * HALO)
    return pl.pallas_call(
        functools.partial(_norm_matmul_kernel, tn=tn),
        grid=(pl.cdiv(n, tm),),
        in_specs=[pl.BlockSpec((tm, d), lambda i: (i, 0)),
                  pl.BlockSpec((1, d), lambda i: (0, 0), pipeline_mode=pl.Buffered(1)),
                  pl.BlockSpec((d, m), lambda i: (0, 0), pipeline_mode=pl.Buffered(1))],
        out_specs=pl.BlockSpec((tm, m), lambda i: (i, 0)),
        out_shape=jax.ShapeDtypeStruct((n, m), BF16),
        compiler_params=_cp("arbitrary"),
        name="norm_matmul",
    )(x, g.reshape(1, d), w)


def _conv3(x, prev_row, next_row, w, row0, t_len):
    n = x.shape[0]
    li = lax.broadcasted_iota(jnp.int32, (n, 1), 0)
    gi = li + row0
    xm = pltpu.roll(x, 1, 0)
    xm = jnp.where(li == 0, prev_row, xm)
    xm = jnp.where(gi >= 1, xm, 0.0)
    xp = pltpu.roll(x, n - 1, 0)
    xp = jnp.where(li == n - 1, next_row, xp)
    xp = jnp.where(gi + 1 < t_len, xp, 0.0)
    return xm * w[0:1] + x * w[1:2] + xp * w[2:3]


def _last_row(halo_ref):
    return halo_ref[0].astype(F32)[HALO - 1:HALO]


def _first_row(halo_ref):
    return halo_ref[0].astype(F32)[0:1]


def _halo_specs(tr, cw, t_len, col_blk):
    nh = tr // HALO
    last = t_len // HALO - 1
    main = pl.BlockSpec((1, tr, cw), lambda b, i: (b, i, col_blk))
    prev = pl.BlockSpec((1, HALO, cw), lambda b, i: (b, jnp.maximum(i * nh - 1, 0), col_blk))
    nxt = pl.BlockSpec((1, HALO, cw), lambda b, i: (b, jnp.minimum((i + 1) * nh, last), col_blk))
    return main, prev, nxt


def _fnet_kernel(a_ref, u_ref, cs_ref, o_ref, *, tmf):
    p = jnp.dot(a_ref[0], u_ref[0], preferred_element_type=F32)
    pc = p[:tmf].astype(BF16)
    ps = p[tmf:].astype(BF16)
    for c0 in range(0, BRANCH_W, MXU_COLS):
        blk = slice(c0, c0 + MXU_COLS)
        y = (jnp.dot(pc[:, blk], cs_ref[0, blk, blk], preferred_element_type=F32)
             - jnp.dot(ps[:, blk], cs_ref[1, blk, blk], preferred_element_type=F32))
        o_ref[0, :, blk] = y.astype(o_ref.dtype)


def _fnet_tables(t, tmf):
    nm = -(-t // tmf)
    assert tmf % HALO == 0
    k = jnp.arange(t, dtype=jnp.int32)[None, :]
    ang = lambda j: ((j[:, None] * k) % t).astype(F32) * (2.0 * math.pi / t)
    ang_hi = ang(jnp.arange(nm * tmf // HALO, dtype=jnp.int32) * HALO)[:, None, :]
    ang_lo = ang(jnp.arange(HALO, dtype=jnp.int32))[None, :, :]
    c = (jnp.cos(ang_hi) * jnp.cos(ang_lo) - jnp.sin(ang_hi) * jnp.sin(ang_lo)).reshape(nm * tmf, t)
    s = (jnp.sin(ang_hi) * jnp.cos(ang_lo) + jnp.cos(ang_hi) * jnp.sin(ang_lo)).reshape(nm * tmf, t)
    valid = jnp.arange(nm * tmf, dtype=jnp.int32)[:, None] < t
    c = jnp.where(valid, c, 0.0).astype(BF16).reshape(nm, tmf, t)
    s = jnp.where(valid, s, 0.0).astype(BF16).reshape(nm, tmf, t)
    a = jnp.concatenate([c, s], axis=1)
    jj = jnp.arange(BRANCH_W, dtype=jnp.int32)[:, None]
    kk = jnp.arange(BRANCH_W, dtype=jnp.int32)[None, :]
    same = (jj // FNET_GW) == (kk // FNET_GW)
    ang2 = (((jj % FNET_GW) * (kk % FNET_GW)) % FNET_GW).astype(F32) * (2.0 * math.pi / FNET_GW)
    scale = 1.0 / math.sqrt(t * FNET_GW)
    cc = jnp.where(same, jnp.cos(ang2), 0.0) * scale
    sc = jnp.where(same, jnp.sin(ang2), 0.0) * scale
    return a, jnp.stack([cc, sc]).astype(BF16)


def _fnet_branch(proj, tmf):
    b, t, _ = proj.shape
    a, cs = _fnet_tables(t, tmf)
    nm = a.shape[0]
    return pl.pallas_call(
        functools.partial(_fnet_kernel, tmf=tmf),
        grid=(nm, b),
        in_specs=[pl.BlockSpec((1, 2 * tmf, t), lambda i, j: (i, 0, 0)),
                  pl.BlockSpec((1, t, BRANCH_W), lambda i, j: (j, 0, COL_FN // BRANCH_W)),
                  pl.BlockSpec((2, BRANCH_W, BRANCH_W), lambda i, j: (0, 0, 0))],
        out_specs=pl.BlockSpec((1, tmf, BRANCH_W), lambda i, j: (j, i, 0)),
        out_shape=jax.ShapeDtypeStruct((b, t, BRANCH_W), BF16),
        compiler_params=_cp("arbitrary", "arbitrary"),
        name="fnet",
    )(a, proj, cs)


def _hyena_filter_kernel(w1t_ref, w1c_ref, w1s_ref, b1_ref, w2_ref, b2_ref, fq_ref, w3f_ref, w3b_ref, dl_ref,
                         o_ref, hid_ref, *, t, half):
    row = lax.broadcasted_iota(jnp.int32, (t, 1), 0)
    pos_f = row.astype(F32)
    pos_b = (t - row).astype(F32)
    bands = (HYENA_EMB - 1) // 2

    @pl.when(pl.program_id(0) == 0)
    def _():
        lane = lax.broadcasted_iota(jnp.int32, (1, 2 * bands), 1)
        fr = 1e-4 + (lane % bands).astype(F32) * ((bands - 1 - 1e-4) / (bands - 1))
        pos = jnp.where(lane < bands, pos_f, pos_b)
        arg = (pos * (2.0 * math.pi / t)) * fr
        pre = ((pos_f * (1.0 / (t - 1))) * w1t_ref[0:1, :] + (pos_b * (1.0 / (t - 1))) * w1t_ref[1:2, :]
               + jnp.dot(jnp.cos(arg), w1c_ref[...], precision=HI, preferred_element_type=F32)
               - jnp.dot(jnp.sin(arg), w1s_ref[...], precision=HI, preferred_element_type=F32)
               + b1_ref[...])
        fq = fq_ref[...]
        h1 = jnp.sin(fq * pre)
        hid_ref[...] = jnp.sin(fq * (jnp.dot(h1, w2_ref[...], precision=HI, preferred_element_type=F32)
                                     + b2_ref[...]))

    def taps(w3_ref, pos):
        dec = jnp.exp(-(pos * (1.0 / (t - 1))) * dl_ref[...])
        return jnp.dot(hid_ref[...], w3_ref[...], precision=HI, preferred_element_type=F32) * dec

    hf = taps(w3f_ref, pos_f)
    hb = jnp.where(row >= 1, taps(w3b_ref, pos_b), 0.0)
    l1 = jnp.sum(jnp.abs(hf), axis=0, keepdims=True) + jnp.sum(jnp.abs(hb), axis=0, keepdims=True)
    inv = 1.0 / l1
    cw = o_ref.shape[2]
    o_ref[0, half - t:half, :] = (hb * inv).astype(o_ref.dtype)
    o_ref[0, half:half + t, :] = (hf * inv).astype(o_ref.dtype)
    if half > t:
        o_ref[0, 0:half - t, :] = jnp.zeros((half - t, cw), o_ref.dtype)
        o_ref[0, half + t:, :] = jnp.zeros((half - t, cw), o_ref.dtype)


def _hyena_filters(t, half, w1, b1, w2, b2, w3, freq):
    cw = 256
    nb = BRANCH_W // cw
    bands = (HYENA_EMB - 1) // 2
    nf = w2.shape[0]
    max_decay = math.log(HYENA_TARGET) / HYENA_FAST
    min_decay = math.log(HYENA_TARGET) / HYENA_SLOW
    deltas = jnp.abs(jnp.linspace(min_decay, max_decay, BRANCH_W, dtype=F32)).reshape(1, BRANCH_W)
    full = lambda shape: pl.BlockSpec(shape, lambda g: (0,) * len(shape))
    w1f = w1.astype(F32)
    zero = jnp.zeros_like
    bd = lambda m: jnp.concatenate([jnp.concatenate([m, zero(m)], axis=1), jnp.concatenate([zero(m), m], axis=1)])
    twice = lambda v: jnp.concatenate([v, v]).reshape(1, 2 * nf).astype(F32)
    w1t = bd(w1f[0:1])
    w3f32 = w3.astype(F32)
    w3_top = jnp.concatenate([w3f32, zero(w3f32)], axis=0)
    w3_bot = jnp.concatenate([zero(w3f32), w3f32], axis=0)
    return pl.pallas_call(
        functools.partial(_hyena_filter_kernel, t=t, half=half),
        grid=(HYENA_ORDER * nb,),
        in_specs=[full((2, 2 * nf)), full((2 * bands, 2 * nf)), full((2 * bands, 2 * nf)), full((1, 2 * nf)),
                  full((2 * nf, 2 * nf)), full((1, 2 * nf)), full((1, 2 * nf)),
                  pl.BlockSpec((2 * nf, cw), lambda g: (0, (g // nb) * 2 * nb + g % nb)),
                  pl.BlockSpec((2 * nf, cw), lambda g: (0, (g // nb) * 2 * nb + nb + g % nb)),
                  pl.BlockSpec((1, cw), lambda g: (0, g % nb))],
        out_specs=pl.BlockSpec((1, 2 * half, cw), lambda g: (g // nb, 0, g % nb)),
        out_shape=jax.ShapeDtypeStruct((HYENA_ORDER, 2 * half, BRANCH_W), BF16),
        scratch_shapes=[pltpu.VMEM((t, 2 * nf), F32)],
        compiler_params=_cp("arbitrary"),
        name="hyena_filter",
    )(w1t, bd(w1f[1:1 + bands]), bd(w1f[1 + bands:]), twice(b1), bd(w2.astype(F32)), twice(b2), twice(freq),
      w3_top, w3_bot, deltas)


def _hyena_tables(p_len):
    f = jnp.arange(p_len, dtype=jnp.int32)[:, None]
    n = jnp.arange(p_len, dtype=jnp.int32)[None, :]
    ang = (((2 * f + 1) * n) % (4 * p_len)).astype(F32) * (math.pi / (2 * p_len))
    a = jnp.concatenate([jnp.cos(ang), -jnp.sin(ang)], axis=0).astype(BF16)
    return a, a.T


def _hy_spec_kernel(a_ref, seg_ref, prev_ref, o_ref, *, p_len):
    g = jnp.dot(a_ref[...], seg_ref[0, 0], preferred_element_type=F32)
    gp = jnp.dot(a_ref[...], prev_ref[0, 0], preferred_element_type=F32)
    first = prev_ref[0, 0, 0:1, :].astype(F32)
    f = lax.broadcasted_iota(jnp.int32, (p_len, 1), 0)
    sign = jnp.where(f % 2 == 0, 1.0, -1.0)
    scale = 1.0 / p_len
    o_ref[0, 0, :p_len] = ((g[:p_len] - sign * gp[p_len:]) * scale).astype(o_ref.dtype)
    o_ref[0, 0, p_len:] = ((g[p_len:] + sign * (gp[:p_len] - first)) * scale).astype(o_ref.dtype)


def _hy_filter_spectrum(a, filt, p_len, nb):
    order, _, w = filt.shape
    lags = filt.reshape(order, 2 * nb, p_len, w)
    return pl.pallas_call(
        functools.partial(_hy_spec_kernel, p_len=p_len),
        grid=(order, 2 * nb - 1),
        in_specs=[pl.BlockSpec((2 * p_len, p_len), lambda o, d: (0, 0)),
                  pl.BlockSpec((1, 1, p_len, w), lambda o, d: (o, d + 1, 0, 0)),
                  pl.BlockSpec((1, 1, p_len, w), lambda o, d: (o, d, 0, 0))],
        out_specs=pl.BlockSpec((1, 1, 2 * p_len, w), lambda o, d: (o, d, 0, 0)),
        out_shape=jax.ShapeDtypeStruct((order, 2 * nb - 1, 2 * p_len, w), BF16),
        compiler_params=_cp("arbitrary", "arbitrary"),
        name="hyena_filter_spectrum",
    )(a, lags, lags)


def _hy_conv_kernel(a_ref, at_ref, k_ref, v_ref, g_ref, bias_ref, o_ref, u_scr, y_scr, *, t_len, p_len, nb, fc):
    cw = v_ref.shape[2]

    def block_rows(i):
        return i * p_len, min(p_len, t_len - i * p_len)

    for j in range(nb):
        r0, n = block_rows(j)
        u = v_ref[0, r0:r0 + n, :]
        if n < p_len:
            u = jnp.concatenate([u, jnp.zeros((p_len - n, cw), u.dtype)], axis=0)
        u_scr[j, :p_len] = jnp.dot(a_ref[:p_len], u, preferred_element_type=F32).astype(BF16)
        u_scr[j, p_len:] = jnp.dot(a_ref[p_len:], u, preferred_element_type=F32).astype(BF16)

    for i in range(nb):
        for r in range(0, p_len, fc):
            acc_r = acc_i = None
            for j in range(nb):
                d = i - j + nb - 1
                ur, ui = u_scr[j, r:r + fc, :], u_scr[j, p_len + r:p_len + r + fc, :]
                kr, ki = k_ref[0, d, r:r + fc, :], k_ref[0, d, p_len + r:p_len + r + fc, :]
                tr_, ti_ = ur * kr - ui * ki, ur * ki + ui * kr
                acc_r, acc_i = (tr_, ti_) if acc_r is None else (acc_r + tr_, acc_i + ti_)
            y_scr[i % 2, r:r + fc, :] = acc_r
            y_scr[i % 2, p_len + r:p_len + r + fc, :] = acc_i
        conv = jnp.dot(at_ref[...], y_scr[i % 2], preferred_element_type=F32)
        r0, n = block_rows(i)
        v = v_ref[0, r0:r0 + n, :].astype(F32)
        o_ref[0, r0:r0 + n, :] = (g_ref[0, r0:r0 + n, :].astype(F32) * (conv[:n] + v * bias_ref[0])).astype(o_ref.dtype)


def _hy_long_conv(a, at, spec, order, v, v_col, gate, gate_col, bias, p_len):
    b, t, _ = v.shape
    nb = -(-t // p_len)
    cw = BRANCH_W // 2
    nh = BRANCH_W // cw
    return pl.pallas_call(
        functools.partial(_hy_conv_kernel, t_len=t, p_len=p_len, nb=nb, fc=HYENA_FREQ_CHUNK),
        grid=(nh, b),
        in_specs=[pl.BlockSpec((2 * p_len, p_len), lambda c, i: (0, 0), pipeline_mode=pl.Buffered(1)),
                  pl.BlockSpec((p_len, 2 * p_len), lambda c, i: (0, 0), pipeline_mode=pl.Buffered(1)),
                  pl.BlockSpec((1, 2 * nb - 1, 2 * p_len, cw), lambda c, i: (order, 0, 0, c),
                               pipeline_mode=pl.Buffered(1)),
                  pl.BlockSpec((1, t, cw), lambda c, i: (i, 0, v_col * nh + c)),
                  pl.BlockSpec((1, t, cw), lambda c, i: (i, 0, gate_col * nh + c)),
                  pl.BlockSpec((1, 1, cw), lambda c, i: (order, 0, c))],
        out_specs=pl.BlockSpec((1, t, cw), lambda c, i: (i, 0, c)),
        out_shape=jax.ShapeDtypeStruct((b, t, BRANCH_W), BF16),
        scratch_shapes=[pltpu.VMEM((nb, 2 * p_len, cw), BF16), pltpu.VMEM((2, 2 * p_len, cw), BF16)],
        compiler_params=_cp("arbitrary", "arbitrary"),
        name="hyena_conv",
    )(a, at, spec, v, gate, bias.reshape(HYENA_ORDER, 1, BRANCH_W))


def _softplus(x):
    return jnp.maximum(x, 0.0) + jnp.log(1.0 + jnp.exp(-jnp.abs(x)))


def _ssd_kernel(xs_ref, b_ref, c_ref, z_ref, dt_ref, dtb_ref, an_ref, dsk_ref, nw_ref, tri_ref, exp_ref, o_ref,
                yf_scr, yb_scr, *, t_len, nc, unroll):
    q = SSM_CHUNK
    gw = SSM_HPG * SSM_HEAD_DIM
    front = q - N_META
    ri = lax.broadcasted_iota(jnp.int32, (q, q), 0)
    ci = lax.broadcasted_iota(jnp.int32, (q, q), 1)
    tri = (ri >= ci, ri <= ci)
    head_of_lane = lax.broadcasted_iota(jnp.int32, (1, gw), 1) // SSM_HEAD_DIM
    lane = lax.broadcasted_iota(jnp.int32, (1, q), 1)
    dt_lane_mask = (lane < 2 * SSM_HPG).astype(F32)
    low_half = lane < SSM_HEAD_DIM
    dtb = dtb_ref[...]
    an = an_ref[...]
    masked_out = -1e30

    def widen(cols):
        return jnp.concatenate([jnp.where(low_half, cols[0], cols[1]), jnp.where(low_half, cols[2], cols[3])], axis=1)

    def cumsum(d, x):
        hi = x.astype(BF16)
        r1 = x - hi.astype(F32)
        mid = r1.astype(BF16)
        lo = (r1 - mid.astype(F32)).astype(BF16)
        s = jnp.dot(tri_ref[d], jnp.concatenate([hi, mid, lo], axis=1), preferred_element_type=F32)
        return s[:, :q] + s[:, q:2 * q] + s[:, 2 * q:]

    def load(ref, r0, nrows):
        return ref[0, pl.ds(r0, nrows), :]

    def dt_of(raw):
        return _softplus(raw.astype(F32) + dtb) * dt_lane_mask

    def chunk_data(c):
        r0 = c * q - front if isinstance(c, int) else pl.multiple_of(c * q - front, HALO)
        return load(xs_ref, r0, q), load(b_ref, r0, q), load(c_ref, r0, q), dt_of(load(dt_ref, r0, q))

    def chunk0_data():
        def pad(x):
            return jnp.concatenate([jnp.zeros((front, x.shape[1]), x.dtype), x], axis=0)
        return (pad(load(xs_ref, 0, N_META)), pad(load(b_ref, 0, N_META)), pad(load(c_ref, 0, N_META)),
                pad(dt_of(load(dt_ref, 0, N_META))))

    def stage_local(job, cs_n):
        (xs, bm, cm, dt), d = job
        lanes = [d * SSM_HPG + r for r in range(SSM_HPG)]
        cs_cols = [jnp.broadcast_to(cs_n[:, ln:ln + 1], (q, q)) for ln in lanes]
        cs_w = widen(cs_cols)
        dt_w = jnp.dot(dt.astype(BF16), exp_ref[d], preferred_element_type=F32).astype(BF16)
        tot = cs_w[q - 1:q, :] if d == 0 else cs_w[0:1, :]
        xt = xs * dt_w
        xd = xt * jnp.exp2(tot - cs_w).astype(BF16)
        s_new = lax.dot_general(bm, xd, (((0,), (0,)), ((), ())), preferred_element_type=F32)
        cb = lax.dot_general(cm, bm, (((1,), (1,)), ((), ())), preferred_element_type=F32)
        return cs_cols, xt, cb, cm, s_new, jnp.exp2(cs_w), jnp.exp2(tot), cs_n.T

    def stage_carry(loc, h):
        _, _, _, cm, s_new, ecs, etot, _ = loc
        y_off = jnp.dot(cm, h.astype(BF16), preferred_element_type=F32) * ecs
        return y_off, h * etot + s_new

    def stage_diag(job, cs_n, loc, y_off):
        d = job[1]
        cs_cols, xt, cb = loc[:3]
        cs_t = loc[7]
        ms, xm = [], []
        for r in range(SSM_HPG):
            ln = d * SSM_HPG + r
            dec = jnp.exp2(jnp.where(tri[d], cs_cols[r] - cs_t[ln:ln + 1, :], masked_out))
            ms.append((cb * dec).astype(BF16))
            xm.append(jnp.where(head_of_lane == r, xt, jnp.zeros_like(xt)))
        return y_off + jnp.dot(jnp.concatenate(ms, axis=1), jnp.concatenate(xm, axis=0), preferred_element_type=F32)

    def process_all(fwd_data, bwd_data, hf, hb):
        jobs = [(x, 0) for x in fwd_data] + [(x, 1) for x in bwd_data]
        cs = [cumsum(d, data[3] * an) for data, d in jobs]
        ys, pending = [], None
        for k, (job, c) in enumerate(zip(jobs, cs)):
            loc = stage_local(job, c)
            if pending is not None:
                ys.append(stage_diag(*pending))
            y_off, h = stage_carry(loc, hf if job[1] == 0 else hb)
            hf, hb = (h, hb) if job[1] == 0 else (hf, h)
            pending = (job, c, loc, y_off)
        ys.append(stage_diag(*pending))
        return ys[:len(fwd_data)], ys[len(fwd_data):], hf, hb

    def finish(y, r_out, nrows):
        y = y + load(xs_ref, r_out, nrows).astype(F32) * dsk_ref[...]
        y = y * _silu(load(z_ref, r_out, nrows).astype(F32))
        y = y * lax.rsqrt(jnp.mean(y * y, axis=-1, keepdims=True) + NORM_EPS)
        o_ref[0, pl.ds(r_out, nrows), :] = (y * nw_ref[...]).astype(o_ref.dtype)

    def rows(c):
        return pl.ds(c * q if isinstance(c, int) else pl.multiple_of(c * q, q), q)

    def scan_step(cf, carry):
        hf, hb = carry
        cb_ = [nc - 1 - c for c in cf]
        data = lambda c: chunk0_data() if isinstance(c, int) and c == 0 else chunk_data(c)
        yf, yb, hf, hb = process_all([data(c) for c in cf], [data(c) for c in cb_], hf, hb)
        for c, y in zip(cf, yf):
            yf_scr[rows(c), :] = y
        for c, y in zip(cb_, yb):
            yb_scr[rows(c), :] = y
        return hf, hb

    n_mid = (nc - unroll - 1) // unroll
    h0 = jnp.zeros((SSM_STATE, gw), F32)
    carry = scan_step(list(range(unroll)), (h0, h0))
    carry = lax.fori_loop(1, 1 + n_mid, lambda k, c: scan_step([k * unroll + j for j in range(unroll)], c), carry)
    scan_step(list(range((1 + n_mid) * unroll, nc)), carry)

    finish(yf_scr[front:q, :] + yb_scr[front:q, :], 0, N_META)

    def finish_body(c, carry):
        finish(yf_scr[rows(c), :] + yb_scr[rows(c), :], pl.multiple_of(c * q - front, HALO), q)
        return carry

    lax.fori_loop(1, nc, finish_body, 0, unroll=unroll)


def _ssd_branch(xbc, proj, dt_bias, a_log, d_skip, norm_w):
    b, t, _ = xbc.shape
    q = SSM_CHUNK
    gw = SSM_HPG * SSM_HEAD_DIM
    nc = (q - N_META + t) // q
    assert nc * q == q - N_META + t, "sequence length minus meta tokens must be a multiple of the SSD chunk"
    a = -jnp.exp(a_log.astype(F32)).reshape(2, SSM_GROUPS, SSM_HPG)
    dtb = dt_bias.astype(F32).reshape(2, SSM_GROUPS, SSM_HPG)
    pad = lambda v: jnp.pad(jnp.transpose(v, (1, 0, 2)).reshape(SSM_GROUPS, 1, 2 * SSM_HPG),
                            ((0, 0), (0, 0), (0, q - 2 * SSM_HPG)))
    a_n = pad(a * math.log2(math.e))
    dtb_n = pad(dtb)
    ri = jnp.arange(q, dtype=jnp.int32)[:, None]
    ci = jnp.arange(q, dtype=jnp.int32)[None, :]
    tri = jnp.stack([ri >= ci, ri <= ci]).astype(BF16)
    cj = jnp.arange(gw, dtype=jnp.int32)[None, :] // SSM_HEAD_DIM
    expand = jnp.stack([ri == cj, ri == cj + SSM_HPG]).astype(BF16)
    dsk = jnp.repeat(d_skip.astype(F32).reshape(SSM_GROUPS, 1, SSM_HPG), SSM_HEAD_DIM, axis=-1)
    nw = norm_w.astype(F32).reshape(SSM_GROUPS, 1, gw)
    par = lambda shape: pl.BlockSpec((None,) + shape, lambda i, g: (g, 0, 0))
    return pl.pallas_call(
        functools.partial(_ssd_kernel, t_len=t, nc=nc, unroll=min(SSD_UNROLL, nc - 1)),
        grid=(b, SSM_GROUPS),
        in_specs=[pl.BlockSpec((1, t, gw), lambda i, g: (i, 0, g)),
                  pl.BlockSpec((1, t, SSM_STATE), lambda i, g: (i, 0, 4 + g)),
                  pl.BlockSpec((1, t, SSM_STATE), lambda i, g: (i, 0, 6 + g)),
                  pl.BlockSpec((1, t, gw), lambda i, g: (i, 0, COL_Z // gw + g)),
                  pl.BlockSpec((1, t, q), lambda i, g: (i, 0, COL_DT // q + g)),
                  par((1, q)), par((1, q)), par((1, gw)), par((1, gw)),
                  pl.BlockSpec((2, q, q), lambda i, g: (0, 0, 0)),
                  pl.BlockSpec((2, q, gw), lambda i, g: (0, 0, 0))],
        out_specs=pl.BlockSpec((1, t, gw), lambda i, g: (i, 0, g)),
        out_shape=jax.ShapeDtypeStruct((b, t, BRANCH_W), BF16),
        scratch_shapes=[pltpu.VMEM((nc * q, gw), F32), pltpu.VMEM((nc * q, gw), F32)],
        compiler_params=_cp("arbitrary", "arbitrary"),
        name="ssd",
    )(xbc, xbc, xbc, proj, proj, dtb_n, a_n, dsk, nw, tri, expand)


def _merge_kernel(h_ref, y0_ref, y1_ref, y2_ref, g0_ref, g1_ref, g2_ref, g3_ref, bg_ref, cg_ref, xi_ref, cgp_ref,
                  xip_ref, cgn_ref, xin_ref, scw_ref, wb_ref, wo_ref, o_ref, m_scr, *, tr, t_len):
    u = cg_ref[0].astype(F32) * xi_ref[0].astype(F32)
    prev = _last_row(cgp_ref) * _last_row(xip_ref)
    nxt = _first_row(cgn_ref) * _first_row(xin_ref)
    y_sc = (bg_ref[0].astype(F32) * _conv3(u, prev, nxt, scw_ref[...], pl.program_id(1) * tr, t_len)).astype(BF16)
    ys = (y0_ref[0], y1_ref[0], y2_ref[0], y_sc)
    gs = (g0_ref, g1_ref, g2_ref, g3_ref)
    for s0, w in _col_chunks(h_ref.shape[2]):
        merged = None
        for k in range(len(ys)):
            term = ((1.0 + jnp.tanh(gs[k][0, :, s0:s0 + w].astype(F32)))
                    * jnp.dot(ys[k], wb_ref[k, :, s0:s0 + w], preferred_element_type=F32))
            merged = term if merged is None else merged + term
        m_scr[:, s0:s0 + w] = merged.astype(BF16)
    o_ref[0] = h_ref[0] + jnp.dot(m_scr[...], wo_ref[...], preferred_element_type=F32)


def _merge(h, ys, proj, sc_w, wb, wo, tr):
    b, t, d = h.shape
    nb = wb.shape[0]
    row = lambda w, c: pl.BlockSpec((1, tr, w), lambda b_, i: (b_, i, c))
    c0 = COL_SC // BRANCH_W
    bg, _, _ = _halo_specs(tr, BRANCH_W, t, c0)
    cg, cgp, cgn = _halo_specs(tr, BRANCH_W, t, c0 + 1)
    xi, xip, xin = _halo_specs(tr, BRANCH_W, t, c0 + 2)
    return pl.pallas_call(
        functools.partial(_merge_kernel, tr=tr, t_len=t),
        grid=(b, pl.cdiv(t, tr)),
        in_specs=[row(d, 0)] + [row(BRANCH_W, 0)] * len(ys) + [row(d, COL_GATE // d + k) for k in range(nb)]
                 + [bg, cg, xi, cgp, xip, cgn, xin, _resident((3, BRANCH_W)), _resident((nb, BRANCH_W, d)),
                    _resident((d, d))],
        out_specs=row(d, 0),
        out_shape=jax.ShapeDtypeStruct((b, t, d), F32),
        scratch_shapes=[pltpu.VMEM((tr, d), BF16)],
        compiler_params=_cp("arbitrary", "arbitrary"),
        name="merge",
    )(h, *ys, *([proj] * (nb + 7)), sc_w, wb, wo)


def _rms_rows(x, g):
    return x * lax.rsqrt(jnp.mean(x * x, axis=-1, keepdims=True) + NORM_EPS) * g


def _stage_normed_rows(xs_ref, x_ref, xp_ref, xn_ref, g, row0, tr, t_len):
    li = lax.broadcasted_iota(jnp.int32, (tr, 1), 0)
    xs_ref[HALO:HALO + tr, :] = jnp.where(row0 + li < t_len, _rms_rows(x_ref[0], g), 0.0).astype(BF16)
    xs_ref[0:HALO, :] = jnp.where(row0 > 0, _rms_rows(xp_ref[0], g), 0.0).astype(BF16)
    xs_ref[HALO + tr:, :] = jnp.where(row0 + tr < t_len, _rms_rows(xn_ref[0], g), 0.0).astype(BF16)


def _conv3_rows(u, w):
    n = u.shape[0]
    return pltpu.roll(u, 1, 0) * w[0:1] + u * w[1:2] + pltpu.roll(u, n - 1, 0) * w[2:3]


def _row_halo_specs(tr, d, t_len):
    nh = tr // HALO
    last = t_len // HALO - 1
    return (pl.BlockSpec((1, tr, d), lambda b, i: (b, i, 0)),
            pl.BlockSpec((1, HALO, d), lambda b, i: (b, jnp.maximum(i * nh - 1, 0), 0)),
            pl.BlockSpec((1, HALO, d), lambda b, i: (b, jnp.minimum((i + 1) * nh, last), 0)))


def _resident(shape):
    return pl.BlockSpec(shape, lambda b, i: (0,) * len(shape), pipeline_mode=pl.Buffered(1))


def _col_chunks(n):
    return tuple((s0, min(PROJ_CHUNK, n - s0)) for s0 in range(0, n, PROJ_CHUNK))


def _conv_proj_kernel(x_ref, xp_ref, xn_ref, g_ref, w_ref, c_ref, b_ref, o_ref, xs_ref, *, tr, t_len, n_act):
    _stage_normed_rows(xs_ref, x_ref, xp_ref, xn_ref, g_ref[...], pl.program_id(1) * tr, tr, t_len)
    xs = xs_ref[...]
    for s0, w in _col_chunks(w_ref.shape[1]):
        u = jnp.dot(xs, w_ref[:, s0:s0 + w], preferred_element_type=F32)
        y = _conv3_rows(u, c_ref[:, s0:s0 + w])[HALO:HALO + tr] + b_ref[:, s0:s0 + w]
        o_ref[0, :, s0:s0 + w] = (_silu(y) if s0 < n_act else y).astype(o_ref.dtype)


def _conv_proj(h, g, w, taps, bias, n_act, tr):
    b, t, d = h.shape
    m = w.shape[1]
    main, prev, nxt = _row_halo_specs(tr, d, t)
    return pl.pallas_call(
        functools.partial(_conv_proj_kernel, tr=tr, t_len=t, n_act=n_act),
        grid=(b, pl.cdiv(t, tr)),
        in_specs=[main, prev, nxt, _resident((1, d)), _resident((d, m)), _resident((3, m)), _resident((1, m))],
        out_specs=pl.BlockSpec((1, tr, m), lambda b_, i: (b_, i, 0)),
        out_shape=jax.ShapeDtypeStruct((b, t, m), BF16),
        scratch_shapes=[pltpu.VMEM((tr + 2 * HALO, d), BF16)],
        compiler_params=_cp("arbitrary", "arbitrary"),
        name="conv_proj",
    )(h, h, h, g.reshape(1, d), w, taps, bias)


def _ffn_kernel(x_ref, xp_ref, xn_ref, g_ref, wu_ref, cw_ref, wd_ref, o_ref, xs_ref, gate_ref, *, tr, t_len):
    _stage_normed_rows(xs_ref, x_ref, xp_ref, xn_ref, g_ref[...], pl.program_id(1) * tr, tr, t_len)
    xs = xs_ref[...]
    dff = wd_ref.shape[0]
    for s0, w in _col_chunks(dff):
        a = _conv3_rows(jnp.dot(xs, wu_ref[:, s0:s0 + w], preferred_element_type=F32), cw_ref[:, s0:s0 + w])
        v = _conv3_rows(jnp.dot(xs, wu_ref[:, dff + s0:dff + s0 + w], preferred_element_type=F32),
                        cw_ref[:, dff + s0:dff + s0 + w])
        gate_ref[:, s0:s0 + w] = (_silu(a) * v)[HALO:HALO + tr].astype(BF16)
    o_ref[0] = x_ref[0] + jnp.dot(gate_ref[...], wd_ref[...], preferred_element_type=F32)


def _ffn(h, g, w_up, conv_w, w_down, tr):
    b, t, d = h.shape
    dff = w_down.shape[0]
    main, prev, nxt = _row_halo_specs(tr, d, t)
    return pl.pallas_call(
        functools.partial(_ffn_kernel, tr=tr, t_len=t),
        grid=(b, pl.cdiv(t, tr)),
        in_specs=[main, prev, nxt, _resident((1, d)), _resident((d, 2 * dff)), _resident((3, 2 * dff)),
                  _resident((dff, d))],
        out_specs=pl.BlockSpec((1, tr, d), lambda b_, i: (b_, i, 0)),
        out_shape=jax.ShapeDtypeStruct((b, t, d), F32),
        scratch_shapes=[pltpu.VMEM((tr + 2 * HALO, d), BF16), pltpu.VMEM((tr, dff), BF16)],
        compiler_params=_cp("arbitrary", "arbitrary"),
        name="ffn",
    )(h, h, h, g.reshape(1, d), w_up, conv_w, w_down)


def _final_norm_kernel(x_ref, xn_ref, g_ref, o_ref, *, tr):
    g = g_ref[...]
    o_ref[0, 0:tr - N_META, :] = _rms_rows(x_ref[0], g)[N_META:]
    o_ref[0, tr - N_META:, :] = _rms_rows(xn_ref[0], g)


def _final_norm(x, g, tr):
    b, t, d = x.shape
    assert N_META == HALO
    main, _, nxt = _row_halo_specs(tr, d, t)
    return pl.pallas_call(
        functools.partial(_final_norm_kernel, tr=tr),
        grid=(b, pl.cdiv(t - N_META, tr)),
        in_specs=[main, nxt, pl.BlockSpec((1, d), lambda b_, i: (0, 0))],
        out_specs=pl.BlockSpec((1, tr, d), lambda b_, i: (b_, i, 0)),
        out_shape=jax.ShapeDtypeStruct((b, t - N_META, d), F32),
        compiler_params=_cp("arbitrary", "arbitrary"),
        name="final_norm",
    )(x, x, g.reshape(1, d))


def _prep_w_in(w_in):
    d = w_in.shape[0]
    o_z, o_xbc, o_dt = BRANCH_W, 2 * BRANCH_W, 2 * BRANCH_W + 1024
    ndt = 2 * SSM_GROUPS * SSM_HPG
    o_hy = o_dt + ndt
    o_sc = o_hy + 3 * BRANCH_W
    o_gate = o_sc + 3 * BRANCH_W
    w_dt = w_in[:, o_dt:o_hy].reshape(d, 2, SSM_GROUPS, SSM_HPG)
    w_dt = jnp.transpose(w_dt, (0, 2, 1, 3)).reshape(d, SSM_GROUPS, 2 * SSM_HPG)
    w_dt = jnp.pad(w_dt, ((0, 0), (0, 0), (0, SSM_CHUNK - 2 * SSM_HPG))).reshape(d, SSM_GROUPS * SSM_CHUNK)
    w_conv = jnp.concatenate([w_in[:, o_xbc:o_dt], w_in[:, o_hy:o_sc]], axis=1).astype(BF16)
    w_plain = jnp.concatenate([w_in[:, :o_xbc], 0.5 * w_in[:, o_gate:], w_in[:, o_sc:o_gate], w_dt],
                              axis=1).astype(BF16)
    return w_conv, w_plain


def _hyena_fft_len(t):
    tf = 528
    nfb = -(-(2 * t - 1) // (2 * tf))
    return 2 * tf * nfb, tf


def _run_trunk(x, meta_tokens, norm_final, layers):
    b, seq, d = x.shape
    t = seq + N_META
    meta = jnp.broadcast_to(meta_tokens[None].astype(x.dtype), (b, N_META, d))
    h = jnp.concatenate([meta, x], axis=1)
    tr = _pick_tile(t, ROW_TILE)
    hy_a, hy_at = _hyena_tables(HYENA_BLOCK)
    hy_nb = -(-t // HYENA_BLOCK)
    tmf = tr
    tok = min(TOKEN_TILE, b * t)
    hv, hx1, hx2 = (CV_HY // BRANCH_W + k for k in range(3))
    for p in layers:
        w_conv, w_plain = p["w_in"]
        taps = jnp.concatenate([p["ssm_conv_w"], p["hyena_conv_w"]], axis=1)
        bias = jnp.concatenate([p["ssm_conv_b"], jnp.zeros((3 * BRANCH_W,), F32)]).reshape(1, N_CONV)
        projc = _conv_proj(h, p["norm_mix"], w_conv, taps, bias, CV_HY, tr)
        proj = _norm_matmul(h.reshape(b * t, d), p["norm_mix"], w_plain, min(512, tok), 768).reshape(b, t, N_PLAIN)
        y_fn = _fnet_branch(proj, tmf)
        y_ssm = _ssd_branch(projc, proj, p["ssm_dt_bias"], p["ssm_a_log"], p["ssm_d"], p["ssm_norm"])
        filt = _hyena_filters(t, hy_nb * HYENA_BLOCK, p["hyena_w1"], p["hyena_b1"], p["hyena_w2"], p["hyena_b2"],
                              p["hyena_w3"], p["hyena_freq"])
        spec = _hy_filter_spectrum(hy_a, filt, HYENA_BLOCK, hy_nb)
        z = _hy_long_conv(hy_a, hy_at, spec, 0, projc, hv, projc, hx1, p["hyena_bias"], HYENA_BLOCK)
        y_hy = _hy_long_conv(hy_a, hy_at, spec, 1, z, 0, projc, hx2, p["hyena_bias"], HYENA_BLOCK)
        h = _merge(h, [y_fn, y_ssm, y_hy], proj, p["sc_conv_w"], p["w_branch"], p["w_out"], tr)
        h = _ffn(h, p["norm_ffn"], p["w_up"], p["ffn_conv_w"], p["w_down"], tr)
    return _final_norm(h, norm_final, tr)


def kernel(x_prompt, x_sample, meta_tokens, norm_mix, w_in, ssm_conv_w, ssm_conv_b, ssm_dt_bias, ssm_a_log, ssm_d,
           ssm_norm, hyena_conv_w, hyena_w1, hyena_b1, hyena_w2, hyena_b2, hyena_w3, hyena_freq, hyena_bias,
           sc_conv_w, w_branch, w_out, norm_ffn, ffn_conv_w, w_up, w_down, norm_final):
    depth = w_in.shape[0]
    layers = []
    for l in range(depth):
        layers.append(dict(
            norm_mix=norm_mix[l], w_in=_prep_w_in(w_in[l]), ssm_conv_w=ssm_conv_w[l], ssm_conv_b=ssm_conv_b[l],
            ssm_dt_bias=ssm_dt_bias[l], ssm_a_log=ssm_a_log[l], ssm_d=ssm_d[l], ssm_norm=ssm_norm[l],
            hyena_conv_w=hyena_conv_w[l], hyena_w1=hyena_w1[l], hyena_b1=hyena_b1[l], hyena_w2=hyena_w2[l],
            hyena_b2=hyena_b2[l], hyena_w3=hyena_w3[l], hyena_freq=hyena_freq[l], hyena_bias=hyena_bias[l],
            sc_conv_w=sc_conv_w[l], w_branch=w_branch[l].astype(BF16), w_out=(0.5 * w_out[l]).astype(BF16),
            norm_ffn=norm_ffn[l], ffn_conv_w=ffn_conv_w[l], w_up=w_up[l].astype(BF16), w_down=w_down[l].astype(BF16)))
    y_prompt = _run_trunk(x_prompt, meta_tokens, norm_final, layers)
    y_sample = _run_trunk(x_sample, meta_tokens, norm_final, layers)
    return (y_prompt, y_sample)
```

```python
import functools
import math

import jax
import jax.numpy as jnp
from jax import lax
from jax.experimental import pallas as pl
from jax.experimental.pallas import tpu as pltpu

F32 = jnp.float32
BF16 = jnp.bfloat16
HI = lax.Precision.HIGHEST

NORM_EPS = 1e-6
N_META = 16
BRANCH_W = 512
FNET_GW = 128
SSM_GROUPS = 2
SSM_HPG = 4
SSM_HEAD_DIM = 64
SSM_STATE = 128
SSM_CHUNK = 128
HYENA_ORDER = 2
HYENA_EMB = 33
HYENA_TARGET = 1e-2
HYENA_FAST = 0.3
HYENA_SLOW = 1.5

ROW_TILE = 700
TOKEN_TILE = 512
HYENA_BLOCK = 704
HYENA_FREQ_CHUNK = 32
MXU_COLS = 256
PROJ_CHUNK = 2 * MXU_COLS
SSD_UNROLL = 4
HALO = 16
VMEM_LIMIT = 56 * 1024 * 1024

CV_XBC = 0
CV_HY = 1024
N_CONV = 2560
COL_FN = 0
COL_Z = 512
COL_GATE = 1024
COL_SC = 5120
COL_DT = 6656
N_PLAIN = 6912


def _cp(*sem):
    return pltpu.CompilerParams(dimension_semantics=sem, vmem_limit_bytes=VMEM_LIMIT)


def _pick_tile(n, target, mult=16):
    k = max(1, -(-n // target))
    t = -(-n // k)
    return -(-t // mult) * mult


def _silu(x):
    h = 0.5 * x
    return h * (1.0 + jnp.tanh(h))


def _norm_matmul_kernel(x_ref, g_ref, w_ref, o_ref, *, tn):
    x = x_ref[...]
    ms = jnp.mean(x * x, axis=-1, keepdims=True)
    xn = (x * lax.rsqrt(ms + NORM_EPS) * g_ref[...]).astype(BF16)
    for s0 in range(0, w_ref.shape[1], tn):
        o_ref[:, s0:s0 + tn] = jnp.dot(xn, w_ref[:, s0:s0 + tn], preferred_element_type=F32).astype(o_ref.dtype)


def _norm_matmul(x, g, w, tm, tn):
    n, d = x.shape
    m = w.shape[1]
    assert m % tn == 0
    tm = min(tm, -(-n // HALO) * HALO)
    return pl.pallas_call(
        functools.partial(_norm_matmul_kernel, tn=tn),
        grid=(pl.cdiv(n, tm),),
        in_specs=[pl.BlockSpec((tm, d), lambda i: (i, 0)),
                  pl.BlockSpec((1, d), lambda i: (0, 0), pipeline_mode=pl.Buffered(1)),
                  pl.BlockSpec((d, m), lambda i: (0, 0), pipeline_mode=pl.Buffered(1))],
        out_specs=pl.BlockSpec((tm, m), lambda i: (i, 0)),
        out_shape=jax.ShapeDtypeStruct((n, m), BF16),
        compiler_params=_cp("arbitrary"),
        name="norm_matmul",
    )(x, g.reshape(1, d), w)


def _conv3(x, prev_row, next_row, w, row0, t_len):
    n = x.shape[0]
    li = lax.broadcasted_iota(jnp.int32, (n, 1), 0)
    gi = li + row0
    xm = pltpu.roll(x, 1, 0)
    xm = jnp.where(li == 0, prev_row, xm)
    xm = jnp.where(gi >= 1, xm, 0.0)
    xp = pltpu.roll(x, n - 1, 0)
    xp = jnp.where(li == n - 1, next_row, xp)
    xp = jnp.where(gi + 1 < t_len, xp, 0.0)
    return xm * w[0:1] + x * w[1:2] + xp * w[2:3]


def _last_row(halo_ref):
    return halo_ref[0].astype(F32)[HALO - 1:HALO]


def _first_row(halo_ref):
    return halo_ref[0].astype(F32)[0:1]


def _halo_specs(tr, cw, t_len, col_blk):
    nh = tr // HALO
    last = t_len // HALO - 1
    main = pl.BlockSpec((1, tr, cw), lambda b, i: (b, i, col_blk))
    prev = pl.BlockSpec((1, HALO, cw), lambda b, i: (b, jnp.maximum(i * nh - 1, 0), col_blk))
    nxt = pl.BlockSpec((1, HALO, cw), lambda b, i: (b, jnp.minimum((i + 1) * nh, last), col_blk))
    return main, prev, nxt


def _fnet_kernel(a_ref, u_ref, cs_ref, o_ref, *, tmf):
    p = jnp.dot(a_ref[0], u_ref[0], preferred_element_type=F32)
    pc = p[:tmf].astype(BF16)
    ps = p[tmf:].astype(BF16)
    for c0 in range(0, BRANCH_W, MXU_COLS):
        blk = slice(c0, c0 + MXU_COLS)
        y = (jnp.dot(pc[:, blk], cs_ref[0, blk, blk], preferred_element_type=F32)
             - jnp.dot(ps[:, blk], cs_ref[1, blk, blk], preferred_element_type=F32))
        o_ref[0, :, blk] = y.astype(o_ref.dtype)


def _fnet_tables(t, tmf):
    nm = -(-t // tmf)
    assert tmf % HALO == 0
    k = jnp.arange(t, dtype=jnp.int32)[None, :]
    ang = lambda j: ((j[:, None] * k) % t).astype(F32) * (2.0 * math.pi / t)
    ang_hi = ang(jnp.arange(nm * tmf // HALO, dtype=jnp.int32) * HALO)[:, None, :]
    ang_lo = ang(jnp.arange(HALO, dtype=jnp.int32))[None, :, :]
    c = (jnp.cos(ang_hi) * jnp.cos(ang_lo) - jnp.sin(ang_hi) * jnp.sin(ang_lo)).reshape(nm * tmf, t)
    s = (jnp.sin(ang_hi) * jnp.cos(ang_lo) + jnp.cos(ang_hi) * jnp.sin(ang_lo)).reshape(nm * tmf, t)
    valid = jnp.arange(nm * tmf, dtype=jnp.int32)[:, None] < t
    c = jnp.where(valid, c, 0.0).astype(BF16).reshape(nm, tmf, t)
    s = jnp.where(valid, s, 0.0).astype(BF16).reshape(nm, tmf, t)
    a = jnp.concatenate([c, s], axis=1)
    jj = jnp.arange(BRANCH_W, dtype=jnp.int32)[:, None]
    kk = jnp.arange(BRANCH_W, dtype=jnp.int32)[None, :]
    same = (jj // FNET_GW) == (kk // FNET_GW)
    ang2 = (((jj % FNET_GW) * (kk % FNET_GW)) % FNET_GW).astype(F32) * (2.0 * math.pi / FNET_GW)
    scale = 1.0 / math.sqrt(t * FNET_GW)
    cc = jnp.where(same, jnp.cos(ang2), 0.0) * scale
    sc = jnp.where(same, jnp.sin(ang2), 0.0) * scale
    return a, jnp.stack([cc, sc]).astype(BF16)


def _fnet_branch(proj, tmf):
    b, t, _ = proj.shape
    a, cs = _fnet_tables(t, tmf)
    nm = a.shape[0]
    return pl.pallas_call(
        functools.partial(_fnet_kernel, tmf=tmf),
        grid=(nm, b),
        in_specs=[pl.BlockSpec((1, 2 * tmf, t), lambda i, j: (i, 0, 0)),
                  pl.BlockSpec((1, t, BRANCH_W), lambda i, j: (j, 0, COL_FN // BRANCH_W)),
                  pl.BlockSpec((2, BRANCH_W, BRANCH_W), lambda i, j: (0, 0, 0))],
        out_specs=pl.BlockSpec((1, tmf, BRANCH_W), lambda i, j: (j, i, 0)),
        out_shape=jax.ShapeDtypeStruct((b, t, BRANCH_W), BF16),
        compiler_params=_cp("arbitrary", "arbitrary"),
        name="fnet",
    )(a, proj, cs)


def _hyena_filter_kernel(w1t_ref, w1c_ref, w1s_ref, b1_ref, w2_ref, b2_ref, fq_ref, w3f_ref, w3b_ref, dl_ref,
                         o_ref, hid_ref, *, t, half):
    row = lax.broadcasted_iota(jnp.int32, (t, 1), 0)
    pos_f = row.astype(F32)
    pos_b = (t - row).astype(F32)
    bands = (HYENA_EMB - 1) // 2

    @pl.when(pl.program_id(0) == 0)
    def _():
        lane = lax.broadcasted_iota(jnp.int32, (1, 2 * bands), 1)
        fr = 1e-4 + (lane % bands).astype(F32) * ((bands - 1 - 1e-4) / (bands - 1))
        pos = jnp.where(lane < bands, pos_f, pos_b)
        arg = (pos * (2.0 * math.pi / t)) * fr
        pre = ((pos_f * (1.0 / (t - 1))) * w1t_ref[0:1, :] + (pos_b * (1.0 / (t - 1))) * w1t_ref[1:2, :]
               + jnp.dot(jnp.cos(arg), w1c_ref[...], precision=HI, preferred_element_type=F32)
               - jnp.dot(jnp.sin(arg), w1s_ref[...], precision=HI, preferred_element_type=F32)
               + b1_ref[...])
        fq = fq_ref[...]
        h1 = jnp.sin(fq * pre)
        hid_ref[...] = jnp.sin(fq * (jnp.dot(h1, w2_ref[...], precision=HI, preferred_element_type=F32)
                                     + b2_ref[...]))

    def taps(w3_ref, pos):
        dec = jnp.exp(-(pos * (1.0 / (t - 1))) * dl_ref[...])
        return jnp.dot(hid_ref[...], w3_ref[...], precision=HI, preferred_element_type=F32) * dec

    hf = taps(w3f_ref, pos_f)
    hb = jnp.where(row >= 1, taps(w3b_ref, pos_b), 0.0)
    l1 = jnp.sum(jnp.abs(hf), axis=0, keepdims=True) + jnp.sum(jnp.abs(hb), axis=0, keepdims=True)
    inv = 1.0 / l1
    cw = o_ref.shape[2]
    o_ref[0, half - t:half, :] = (hb * inv).astype(o_ref.dtype)
    o_ref[0, half:half + t, :] = (hf * inv).astype(o_ref.dtype)
    if half > t:
        o_ref[0, 0:half - t, :] = jnp.zeros((half - t, cw), o_ref.dtype)
        o_ref[0, half + t:, :] = jnp.zeros((half - t, cw), o_ref.dtype)


def _hyena_filters(t, half, w1, b1, w2, b2, w3, freq):
    cw = 256
    nb = BRANCH_W // cw
    bands = (HYENA_EMB - 1) // 2
    nf = w2.shape[0]
    max_decay = math.log(HYENA_TARGET) / HYENA_FAST
    min_decay = math.log(HYENA_TARGET) / HYENA_SLOW
    deltas = jnp.abs(jnp.linspace(min_decay, max_decay, BRANCH_W, dtype=F32)).reshape(1, BRANCH_W)
    full = lambda shape: pl.BlockSpec(shape, lambda g: (0,) * len(shape))
    w1f = w1.astype(F32)
    zero = jnp.zeros_like
    bd = lambda m: jnp.concatenate([jnp.concatenate([m, zero(m)], axis=1), jnp.concatenate([zero(m), m], axis=1)])
    twice = lambda v: jnp.concatenate([v, v]).reshape(1, 2 * nf).astype(F32)
    w1t = bd(w1f[0:1])
    w3f32 = w3.astype(F32)
    w3_top = jnp.concatenate([w3f32, zero(w3f32)], axis=0)
    w3_bot = jnp.concatenate([zero(w3f32), w3f32], axis=0)
    return pl.pallas_call(
        functools.partial(_hyena_filter_kernel, t=t, half=half),
        grid=(HYENA_ORDER * nb,),
        in_specs=[full((2, 2 * nf)), full((2 * bands, 2 * nf)), full((2 * bands, 2 * nf)), full((1, 2 * nf)),
                  full((2 * nf, 2 * nf)), full((1, 2 * nf)), full((1, 2 * nf)),
                  pl.BlockSpec((2 * nf, cw), lambda g: (0, (g // nb) * 2 * nb + g % nb)),
                  pl.BlockSpec((2 * nf, cw), lambda g: (0, (g // nb) * 2 * nb + nb + g % nb)),
                  pl.BlockSpec((1, cw), lambda g: (0, g % nb))],
        out_specs=pl.BlockSpec((1, 2 * half, cw), lambda g: (g // nb, 0, g % nb)),
        out_shape=jax.ShapeDtypeStruct((HYENA_ORDER, 2 * half, BRANCH_W), BF16),
        scratch_shapes=[pltpu.VMEM((t, 2 * nf), F32)],
        compiler_params=_cp("arbitrary"),
        name="hyena_filter",
    )(w1t, bd(w1f[1:1 + bands]), bd(w1f[1 + bands:]), twice(b1), bd(w2.astype(F32)), twice(b2), twice(freq),
      w3_top, w3_bot, deltas)


def _hyena_tables(p_len):
    f = jnp.arange(p_len, dtype=jnp.int32)[:, None]
    n = jnp.arange(p_len, dtype=jnp.int32)[None, :]
    ang = (((2 * f + 1) * n) % (4 * p_len)).astype(F32) * (math.pi / (2 * p_len))
    a = jnp.concatenate([jnp.cos(ang), -jnp.sin(ang)], axis=0).astype(BF16)
    return a, a.T


def _hy_spec_kernel(a_ref, seg_ref, prev_ref, o_ref, *, p_len):
    g = jnp.dot(a_ref[...], seg_ref[0, 0], preferred_element_type=F32)
    gp = jnp.dot(a_ref[...], prev_ref[0, 0], preferred_element_type=F32)
    first = prev_ref[0, 0, 0:1, :].astype(F32)
    f = lax.broadcasted_iota(jnp.int32, (p_len, 1), 0)
    sign = jnp.where(f % 2 == 0, 1.0, -1.0)
    scale = 1.0 / p_len
    o_ref[0, 0, :p_len] = ((g[:p_len] - sign * gp[p_len:]) * scale).astype(o_ref.dtype)
    o_ref[0, 0, p_len:] = ((g[p_len:] + sign * (gp[:p_len] - first)) * scale).astype(o_ref.dtype)


def _hy_filter_spectrum(a, filt, p_len, nb):
    order, _, w = filt.shape
    lags = filt.reshape(order, 2 * nb, p_len, w)
    return pl.pallas_call(
        functools.partial(_hy_spec_kernel, p_len=p_len),
        grid=(order, 2 * nb - 1),
        in_specs=[pl.BlockSpec((2 * p_len, p_len), lambda o, d: (0, 0)),
                  pl.BlockSpec((1, 1, p_len, w), lambda o, d: (o, d + 1, 0, 0)),
                  pl.BlockSpec((1, 1, p_len, w), lambda o, d: (o, d, 0, 0))],
        out_specs=pl.BlockSpec((1, 1, 2 * p_len, w), lambda o, d: (o, d, 0, 0)),
        out_shape=jax.ShapeDtypeStruct((order, 2 * nb - 1, 2 * p_len, w), BF16),
        compiler_params=_cp("arbitrary", "arbitrary"),
        name="hyena_filter_spectrum",
    )(a, lags, lags)


def _hy_conv_kernel(a_ref, at_ref, k_ref, v_ref, g_ref, bias_ref, o_ref, u_scr, y_scr, *, t_len, p_len, nb, fc):
    cw = v_ref.shape[2]

    def block_rows(i):
        return i * p_len, min(p_len, t_len - i * p_len)

    for j in range(nb):
        r0, n = block_rows(j)
        u = v_ref[0, r0:r0 + n, :]
        if n < p_len:
            u = jnp.concatenate([u, jnp.zeros((p_len - n, cw), u.dtype)], axis=0)
        u_scr[j, :p_len] = jnp.dot(a_ref[:p_len], u, preferred_element_type=F32).astype(BF16)
        u_scr[j, p_len:] = jnp.dot(a_ref[p_len:], u, preferred_element_type=F32).astype(BF16)

    for i in range(nb):
        for r in range(0, p_len, fc):
            acc_r = acc_i = None
            for j in range(nb):
                d = i - j + nb - 1
                ur, ui = u_scr[j, r:r + fc, :], u_scr[j, p_len + r:p_len + r + fc, :]
                kr, ki = k_ref[0, d, r:r + fc, :], k_ref[0, d, p_len + r:p_len + r + fc, :]
                tr_, ti_ = ur * kr - ui * ki, ur * ki + ui * kr
                acc_r, acc_i = (tr_, ti_) if acc_r is None else (acc_r + tr_, acc_i + ti_)
            y_scr[i % 2, r:r + fc, :] = acc_r
            y_scr[i % 2, p_len + r:p_len + r + fc, :] = acc_i
        conv = jnp.dot(at_ref[...], y_scr[i % 2], preferred_element_type=F32)
        r0, n = block_rows(i)
        v = v_ref[0, r0:r0 + n, :].astype(F32)
        o_ref[0, r0:r0 + n, :] = (g_ref[0, r0:r0 + n, :].astype(F32) * (conv[:n] + v * bias_ref[0])).astype(o_ref.dtype)


def _hy_long_conv(a, at, spec, order, v, v_col, gate, gate_col, bias, p_len):
    b, t, _ = v.shape
    nb = -(-t // p_len)
    cw = BRANCH_W // 2
    nh = BRANCH_W // cw
    return pl.pallas_call(
        functools.partial(_hy_conv_kernel, t_len=t, p_len=p_len, nb=nb, fc=HYENA_FREQ_CHUNK),
        grid=(nh, b),
        in_specs=[pl.BlockSpec((2 * p_len, p_len), lambda c, i: (0, 0), pipeline_mode=pl.Buffered(1)),
                  pl.BlockSpec((p_len, 2 * p_len), lambda c, i: (0, 0), pipeline_mode=pl.Buffered(1)),
                  pl.BlockSpec((1, 2 * nb - 1, 2 * p_len, cw), lambda c, i: (order, 0, 0, c),
                               pipeline_mode=pl.Buffered(1)),
                  pl.BlockSpec((1, t, cw), lambda c, i: (i, 0, v_col * nh + c)),
                  pl.BlockSpec((1, t, cw), lambda c, i: (i, 0, gate_col * nh + c)),
                  pl.BlockSpec((1, 1, cw), lambda c, i: (order, 0, c))],
        out_specs=pl.BlockSpec((1, t, cw), lambda c, i: (i, 0, c)),
        out_shape=jax.ShapeDtypeStruct((b, t, BRANCH_W), BF16),
        scratch_shapes=[pltpu.VMEM((nb, 2 * p_len, cw), BF16), pltpu.VMEM((2, 2 * p_len, cw), BF16)],
        compiler_params=_cp("arbitrary", "arbitrary"),
        name="hyena_conv",
    )(a, at, spec, v, gate, bias.reshape(HYENA_ORDER, 1, BRANCH_W))


def _softplus(x):
    return jnp.maximum(x, 0.0) + jnp.log(1.0 + jnp.exp(-jnp.abs(x)))


def _ssd_kernel(xs_ref, b_ref, c_ref, z_ref, dt_ref, dtb_ref, an_ref, dsk_ref, nw_ref, tri_ref, exp_ref, o_ref,
                yf_scr, yb_scr, *, t_len, nc, unroll):
    q = SSM_CHUNK
    gw = SSM_HPG * SSM_HEAD_DIM
    front = q - N_META
    ri = lax.broadcasted_iota(jnp.int32, (q, q), 0)
    ci = lax.broadcasted_iota(jnp.int32, (q, q), 1)
    tri = (ri >= ci, ri <= ci)
    head_of_lane = lax.broadcasted_iota(jnp.int32, (1, gw), 1) // SSM_HEAD_DIM
    lane = lax.broadcasted_iota(jnp.int32, (1, q), 1)
    dt_lane_mask = (lane < 2 * SSM_HPG).astype(F32)
    low_half = lane < SSM_HEAD_DIM
    dtb = dtb_ref[...]
    an = an_ref[...]
    masked_out = -1e30

    def widen(cols):
        return jnp.concatenate([jnp.where(low_half, cols[0], cols[1]), jnp.where(low_half, cols[2], cols[3])], axis=1)

    def cumsum(d, x):
        hi = x.astype(BF16)
        r1 = x - hi.astype(F32)
        mid = r1.astype(BF16)
        lo = (r1 - mid.astype(F32)).astype(BF16)
        s = jnp.dot(tri_ref[d], jnp.concatenate([hi, mid, lo], axis=1), preferred_element_type=F32)
        return s[:, :q] + s[:, q:2 * q] + s[:, 2 * q:]

    def load(ref, r0, nrows):
        return ref[0, pl.ds(r0, nrows), :]

    def dt_of(raw):
        return _softplus(raw.astype(F32) + dtb) * dt_lane_mask

    def chunk_data(c):
        r0 = c * q - front if isinstance(c, int) else pl.multiple_of(c * q - front, HALO)
        return load(xs_ref, r0, q), load(b_ref, r0, q), load(c_ref, r0, q), dt_of(load(dt_ref, r0, q))

    def chunk0_data():
        def pad(x):
            return jnp.concatenate([jnp.zeros((front, x.shape[1]), x.dtype), x], axis=0)
        return (pad(load(xs_ref, 0, N_META)), pad(load(b_ref, 0, N_META)), pad(load(c_ref, 0, N_META)),
                pad(dt_of(load(dt_ref, 0, N_META))))

    def stage_local(job, cs_n):
        (xs, bm, cm, dt), d = job
        lanes = [d * SSM_HPG + r for r in range(SSM_HPG)]
        cs_cols = [jnp.broadcast_to(cs_n[:, ln:ln + 1], (q, q)) for ln in lanes]
        cs_w = widen(cs_cols)
        dt_w = jnp.dot(dt.astype(BF16), exp_ref[d], preferred_element_type=F32).astype(BF16)
        tot = cs_w[q - 1:q, :] if d == 0 else cs_w[0:1, :]
        xt = xs * dt_w
        xd = xt * jnp.exp2(tot - cs_w).astype(BF16)
        s_new = lax.dot_general(bm, xd, (((0,), (0,)), ((), ())), preferred_element_type=F32)
        cb = lax.dot_general(cm, bm, (((1,), (1,)), ((), ())), preferred_element_type=F32)
        return cs_cols, xt, cb, cm, s_new, jnp.exp2(cs_w), jnp.exp2(tot), cs_n.T

    def stage_carry(loc, h):
        _, _, _, cm, s_new, ecs, etot, _ = loc
        y_off = jnp.dot(cm, h.astype(BF16), preferred_element_type=F32) * ecs
        return y_off, h * etot + s_new

    def stage_diag(job, cs_n, loc, y_off):
        d = job[1]
        cs_cols, xt, cb = loc[:3]
        cs_t = loc[7]
        ms, xm = [], []
        for r in range(SSM_HPG):
            ln = d * SSM_HPG + r
            dec = jnp.exp2(jnp.where(tri[d], cs_cols[r] - cs_t[ln:ln + 1, :], masked_out))
            ms.append((cb * dec).astype(BF16))
            xm.append(jnp.where(head_of_lane == r, xt, jnp.zeros_like(xt)))
        return y_off + jnp.dot(jnp.concatenate(ms, axis=1), jnp.concatenate(xm, axis=0), preferred_element_type=F32)

    def process_all(fwd_data, bwd_data, hf, hb):
        jobs = [(x, 0) for x in fwd_data] + [(x, 1) for x in bwd_data]
        cs = [cumsum(d, data[3] * an) for data, d in jobs]
        ys, pending = [], None
        for k, (job, c) in enumerate(zip(jobs, cs)):
            loc = stage_local(job, c)
            if pending is not None:
                ys.append(stage_diag(*pending))
            y_off, h = stage_carry(loc, hf if job[1] == 0 else hb)
            hf, hb = (h, hb) if job[1] == 0 else (hf, h)
            pending = (job, c, loc, y_off)
        ys.append(stage_diag(*pending))
        return ys[:len(fwd_data)], ys[len(fwd_data):], hf, hb

    def finish(y, r_out, nrows):
        y = y + load(xs_ref, r_out, nrows).astype(F32) * dsk_ref[...]
        y = y * _silu(load(z_ref, r_out, nrows).astype(F32))
        y = y * lax.rsqrt(jnp.mean(y * y, axis=-1, keepdims=True) + NORM_EPS)
        o_ref[0, pl.ds(r_out, nrows), :] = (y * nw_ref[...]).astype(o_ref.dtype)

    def rows(c):
        return pl.ds(c * q if isinstance(c, int) else pl.multiple_of(c * q, q), q)

    def scan_step(cf, carry):
        hf, hb = carry
        cb_ = [nc - 1 - c for c in cf]
        data = lambda c: chunk0_data() if isinstance(c, int) and c == 0 else chunk_data(c)
        yf, yb, hf, hb = process_all([data(c) for c in cf], [data(c) for c in cb_], hf, hb)
        for c, y in zip(cf, yf):
            yf_scr[rows(c), :] = y
        for c, y in zip(cb_, yb):
            yb_scr[rows(c), :] = y
        return hf, hb

    n_mid = (nc - unroll - 1) // unroll
    h0 = jnp.zeros((SSM_STATE, gw), F32)
    carry = scan_step(list(range(unroll)), (h0, h0))
    carry = lax.fori_loop(1, 1 + n_mid, lambda k, c: scan_step([k * unroll + j for j in range(unroll)], c), carry)
    scan_step(list(range((1 + n_mid) * unroll, nc)), carry)

    finish(yf_scr[front:q, :] + yb_scr[front:q, :], 0, N_META)

    def finish_body(c, carry):
        finish(yf_scr[rows(c), :] + yb_scr[rows(c), :], pl.multiple_of(c * q - front, HALO), q)
        return carry

    lax.fori_loop(1, nc, finish_body, 0, unroll=unroll)


def _ssd_branch(xbc, proj, dt_bias, a_log, d_skip, norm_w):
    b, t, _ = xbc.shape
    q = SSM_CHUNK
    gw = SSM_HPG * SSM_HEAD_DIM
    nc = (q - N_META + t) // q
    assert nc * q == q - N_META + t, "sequence length minus meta tokens must be a multiple of the SSD chunk"
    a = -jnp.exp(a_log.astype(F32)).reshape(2, SSM_GROUPS, SSM_HPG)
    dtb = dt_bias.astype(F32).reshape(2, SSM_GROUPS, SSM_HPG)
    pad = lambda v: jnp.pad(jnp.transpose(v, (1, 0, 2)).reshape(SSM_GROUPS, 1, 2 * SSM_HPG),
                            ((0, 0), (0, 0), (0, q - 2 * SSM_HPG)))
    a_n = pad(a * math.log2(math.e))
    dtb_n = pad(dtb)
    ri = jnp.arange(q, dtype=jnp.int32)[:, None]
    ci = jnp.arange(q, dtype=jnp.int32)[None, :]
    tri = jnp.stack([ri >= ci, ri <= ci]).astype(BF16)
    cj = jnp.arange(gw, dtype=jnp.int32)[None, :] // SSM_HEAD_DIM
    expand = jnp.stack([ri == cj, ri == cj + SSM_HPG]).astype(BF16)
    dsk = jnp.repeat(d_skip.astype(F32).reshape(SSM_GROUPS, 1, SSM_HPG), SSM_HEAD_DIM, axis=-1)
    nw = norm_w.astype(F32).reshape(SSM_GROUPS, 1, gw)
    par = lambda shape: pl.BlockSpec((None,) + shape, lambda i, g: (g, 0, 0))
    return pl.pallas_call(
        functools.partial(_ssd_kernel, t_len=t, nc=nc, unroll=min(SSD_UNROLL, nc - 1)),
        grid=(b, SSM_GROUPS),
        in_specs=[pl.BlockSpec((1, t, gw), lambda i, g: (i, 0, g)),
                  pl.BlockSpec((1, t, SSM_STATE), lambda i, g: (i, 0, 4 + g)),
                  pl.BlockSpec((1, t, SSM_STATE), lambda i, g: (i, 0, 6 + g)),
                  pl.BlockSpec((1, t, gw), lambda i, g: (i, 0, COL_Z // gw + g)),
                  pl.BlockSpec((1, t, q), lambda i, g: (i, 0, COL_DT // q + g)),
                  par((1, q)), par((1, q)), par((1, gw)), par((1, gw)),
                  pl.BlockSpec((2, q, q), lambda i, g: (0, 0, 0)),
                  pl.BlockSpec((2, q, gw), lambda i, g: (0, 0, 0))],
        out_specs=pl.BlockSpec((1, t, gw), lambda i, g: (i, 0, g)),
        out_shape=jax.ShapeDtypeStruct((b, t, BRANCH_W), BF16),
        scratch_shapes=[pltpu.VMEM((nc * q, gw), F32), pltpu.VMEM((nc * q, gw), F32)],
        compiler_params=_cp("arbitrary", "arbitrary"),
        name="ssd",
    )(xbc, xbc, xbc, proj, proj, dtb_n, a_n, dsk, nw, tri, expand)


def _merge_kernel(h_ref, y0_ref, y1_ref, y2_ref, g0_ref, g1_ref, g2_ref, g3_ref, bg_ref, cg_ref, xi_ref, cgp_ref,
                  xip_ref, cgn_ref, xin_ref, scw_ref, wb_ref, wo_ref, o_ref, m_scr, *, tr, t_len):
    u = cg_ref[0].astype(F32) * xi_ref[0].astype(F32)
    prev = _last_row(cgp_ref) * _last_row(xip_ref)
    nxt = _first_row(cgn_ref) * _first_row(xin_ref)
    y_sc = (bg_ref[0].astype(F32) * _conv3(u, prev, nxt, scw_ref[...], pl.program_id(1) * tr, t_len)).astype(BF16)
    ys = (y0_ref[0], y1_ref[0], y2_ref[0], y_sc)
    gs = (g0_ref, g1_ref, g2_ref, g3_ref)
    for s0, w in _col_chunks(h_ref.shape[2]):
        merged = None
        for k in range(len(ys)):
            term = ((1.0 + jnp.tanh(gs[k][0, :, s0:s0 + w].astype(F32)))
                    * jnp.dot(ys[k], wb_ref[k, :, s0:s0 + w], preferred_element_type=F32))
            merged = term if merged is None else merged + term
        m_scr[:, s0:s0 + w] = merged.astype(BF16)
    o_ref[0] = h_ref[0] + jnp.dot(m_scr[...], wo_ref[...], preferred_element_type=F32)


def _merge(h, ys, proj, sc_w, wb, wo, tr):
    b, t, d = h.shape
    nb = wb.shape[0]
    row = lambda w, c: pl.BlockSpec((1, tr, w), lambda b_, i: (b_, i, c))
    c0 = COL_SC // BRANCH_W
    bg, _, _ = _halo_specs(tr, BRANCH_W, t, c0)
    cg, cgp, cgn = _halo_specs(tr, BRANCH_W, t, c0 + 1)
    xi, xip, xin = _halo_specs(tr, BRANCH_W, t, c0 + 2)
    return pl.pallas_call(
        functools.partial(_merge_kernel, tr=tr, t_len=t),
        grid=(b, pl.cdiv(t, tr)),
        in_specs=[row(d, 0)] + [row(BRANCH_W, 0)] * len(ys) + [row(d, COL_GATE // d + k) for k in range(nb)]
                 + [bg, cg, xi, cgp, xip, cgn, xin, _resident((3, BRANCH_W)), _resident((nb, BRANCH_W, d)),
                    _resident((d, d))],
        out_specs=row(d, 0),
        out_shape=jax.ShapeDtypeStruct((b, t, d), F32),
        scratch_shapes=[pltpu.VMEM((tr, d), BF16)],
        compiler_params=_cp("arbitrary", "arbitrary"),
        name="merge",
    )(h, *ys, *([proj] * (nb + 7)), sc_w, wb, wo)


def _rms_rows(x, g):
    return x * lax.rsqrt(jnp.mean(x * x, axis=-1, keepdims=True) + NORM_EPS) * g


def _stage_normed_rows(xs_ref, x_ref, xp_ref, xn_ref, g, row0, tr, t_len):
    li = lax.broadcasted_iota(jnp.int32, (tr, 1), 0)
    xs_ref[HALO:HALO + tr, :] = jnp.where(row0 + li < t_len, _rms_rows(x_ref[0], g), 0.0).astype(BF16)
    xs_ref[0:HALO, :] = jnp.where(row0 > 0, _rms_rows(xp_ref[0], g), 0.0).astype(BF16)
    xs_ref[HALO + tr:, :] = jnp.where(row0 + tr < t_len, _rms_rows(xn_ref[0], g), 0.0).astype(BF16)


def _conv3_rows(u, w):
    n = u.shape[0]
    return pltpu.roll(u, 1, 0) * w[0:1] + u * w[1:2] + pltpu.roll(u, n - 1, 0) * w[2:3]


def _row_halo_specs(tr, d, t_len):
    nh = tr // HALO
    last = t_len // HALO - 1
    return (pl.BlockSpec((1, tr, d), lambda b, i: (b, i, 0)),
            pl.BlockSpec((1, HALO, d), lambda b, i: (b, jnp.maximum(i * nh - 1, 0), 0)),
            pl.BlockSpec((1, HALO, d), lambda b, i: (b, jnp.minimum((i + 1) * nh, last), 0)))


def _resident(shape):
    return pl.BlockSpec(shape, lambda b, i: (0,) * len(shape), pipeline_mode=pl.Buffered(1))


def _col_chunks(n):
    return tuple((s0, min(PROJ_CHUNK, n - s0)) for s0 in range(0, n, PROJ_CHUNK))


def _conv_proj_kernel(x_ref, xp_ref, xn_ref, g_ref, w_ref, c_ref, b_ref, o_ref, xs_ref, *, tr, t_len, n_act):
    _stage_normed_rows(xs_ref, x_ref, xp_ref, xn_ref, g_ref[...], pl.program_id(1) * tr, tr, t_len)
    xs = xs_ref[...]
    for s0, w in _col_chunks(w_ref.shape[1]):
        u = jnp.dot(xs, w_ref[:, s0:s0 + w], preferred_element_type=F32)
        y = _conv3_rows(u, c_ref[:, s0:s0 + w])[HALO:HALO + tr] + b_ref[:, s0:s0 + w]
        o_ref[0, :, s0:s0 + w] = (_silu(y) if s0 < n_act else y).astype(o_ref.dtype)


def _conv_proj(h, g, w, taps, bias, n_act, tr):
    b, t, d = h.shape
    m = w.shape[1]
    main, prev, nxt = _row_halo_specs(tr, d, t)
    return pl.pallas_call(
        functools.partial(_conv_proj_kernel, tr=tr, t_len=t, n_act=n_act),
        grid=(b, pl.cdiv(t, tr)),
        in_specs=[main, prev, nxt, _resident((1, d)), _resident((d, m)), _resident((3, m)), _resident((1, m))],
        out_specs=pl.BlockSpec((1, tr, m), lambda b_, i: (b_, i, 0)),
        out_shape=jax.ShapeDtypeStruct((b, t, m), BF16),
        scratch_shapes=[pltpu.VMEM((tr + 2 * HALO, d), BF16)],
        compiler_params=_cp("arbitrary", "arbitrary"),
        name="conv_proj",
    )(h, h, h, g.reshape(1, d), w, taps, bias)


def _ffn_kernel(x_ref, xp_ref, xn_ref, g_ref, wu_ref, cw_ref, wd_ref, o_ref, xs_ref, gate_ref, *, tr, t_len):
    _stage_normed_rows(xs_ref, x_ref, xp_ref, xn_ref, g_ref[...], pl.program_id(1) * tr, tr, t_len)
    xs = xs_ref[...]
    dff = wd_ref.shape[0]
    for s0, w in _col_chunks(dff):
        a = _conv3_rows(jnp.dot(xs, wu_ref[:, s0:s0 + w], preferred_element_type=F32), cw_ref[:, s0:s0 + w])
        v = _conv3_rows(jnp.dot(xs, wu_ref[:, dff + s0:dff + s0 + w], preferred_element_type=F32),
                        cw_ref[:, dff + s0:dff + s0 + w])
        gate_ref[:, s0:s0 + w] = (_silu(a) * v)[HALO:HALO + tr].astype(BF16)
    o_ref[0] = x_ref[0] + jnp.dot(gate_ref[...], wd_ref[...], preferred_element_type=F32)


def _ffn(h, g, w_up, conv_w, w_down, tr):
    b, t, d = h.shape
    dff = w_down.shape[0]
    main, prev, nxt = _row_halo_specs(tr, d, t)
    return pl.pallas_call(
        functools.partial(_ffn_kernel, tr=tr, t_len=t),
        grid=(b, pl.cdiv(t, tr)),
        in_specs=[main, prev, nxt, _resident((1, d)), _resident((d, 2 * dff)), _resident((3, 2 * dff)),
                  _resident((dff, d))],
        out_specs=pl.BlockSpec((1, tr, d), lambda b_, i: (b_, i, 0)),
        out_shape=jax.ShapeDtypeStruct((b, t, d), F32),
        scratch_shapes=[pltpu.VMEM((tr + 2 * HALO, d), BF16), pltpu.VMEM((tr, dff), BF16)],
        compiler_params=_cp("arbitrary", "arbitrary"),
        name="ffn",
    )(h, h, h, g.reshape(1, d), w_up, conv_w, w_down)


def _final_norm_kernel(x_ref, xn_ref, g_ref, o_ref, *, tr):
    g = g_ref[...]
    o_ref[0, 0:tr - N_META, :] = _rms_rows(x_ref[0], g)[N_META:]
    o_ref[0, tr - N_META:, :] = _rms_rows(xn_ref[0], g)


def _final_norm(x, g, tr):
    b, t, d = x.shape
    assert N_META == HALO
    main, _, nxt = _row_halo_specs(tr, d, t)
    return pl.pallas_call(
        functools.partial(_final_norm_kernel, tr=tr),
        grid=(b, pl.cdiv(t - N_META, tr)),
        in_specs=[main, nxt, pl.BlockSpec((1, d), lambda b_, i: (0, 0))],
        out_specs=pl.BlockSpec((1, tr, d), lambda b_, i: (b_, i, 0)),
        out_shape=jax.ShapeDtypeStruct((b, t - N_META, d), F32),
        compiler_params=_cp("arbitrary", "arbitrary"),
        name="final_norm",
    )(x, x, g.reshape(1, d))


def _prep_w_in(w_in):
    d = w_in.shape[0]
    o_z, o_xbc, o_dt = BRANCH_W, 2 * BRANCH_W, 2 * BRANCH_W + 1024
    ndt = 2 * SSM_GROUPS * SSM_HPG
    o_hy = o_dt + ndt
    o_sc = o_hy + 3 * BRANCH_W
    o_gate = o_sc + 3 * BRANCH_W
    w_dt = w_in[:, o_dt:o_hy].reshape(d, 2, SSM_GROUPS, SSM_HPG)
    w_dt = jnp.transpose(w_dt, (0, 2, 1, 3)).reshape(d, SSM_GROUPS, 2 * SSM_HPG)
    w_dt = jnp.pad(w_dt, ((0, 0), (0, 0), (0, SSM_CHUNK - 2 * SSM_HPG))).reshape(d, SSM_GROUPS * SSM_CHUNK)
    w_conv = jnp.concatenate([w_in[:, o_xbc:o_dt], w_in[:, o_hy:o_sc]], axis=1).astype(BF16)
    w_plain = jnp.concatenate([w_in[:, :o_xbc], 0.5 * w_in[:, o_gate:], w_in[:, o_sc:o_gate], w_dt],
                              axis=1).astype(BF16)
    return w_conv, w_plain


def _run_trunk(x, meta_tokens, norm_final, layers):
    b, seq, d = x.shape
    t = seq + N_META
    meta = jnp.broadcast_to(meta_tokens[None].astype(x.dtype), (b, N_META, d))
    h = jnp.concatenate([meta, x], axis=1)
    tr = _pick_tile(t, ROW_TILE)
    hy_a, hy_at = _hyena_tables(HYENA_BLOCK)
    hy_nb = -(-t // HYENA_BLOCK)
    tmf = tr
    tok = min(TOKEN_TILE, b * t)
    hv, hx1, hx2 = (CV_HY // BRANCH_W + k for k in range(3))
    for p in layers:
        w_conv, w_plain = p["w_in"]
        taps = jnp.concatenate([p["ssm_conv_w"], p["hyena_conv_w"]], axis=1)
        bias = jnp.concatenate([p["ssm_conv_b"], jnp.zeros((3 * BRANCH_W,), F32)]).reshape(1, N_CONV)
        projc = _conv_proj(h, p["norm_mix"], w_conv, taps, bias, CV_HY, tr)
        proj = _norm_matmul(h.reshape(b * t, d), p["norm_mix"], w_plain, tok, 768).reshape(b, t, N_PLAIN)
        y_fn = _fnet_branch(proj, tmf)
        y_ssm = _ssd_branch(projc, proj, p["ssm_dt_bias"], p["ssm_a_log"], p["ssm_d"], p["ssm_norm"])
        filt = _hyena_filters(t, hy_nb * HYENA_BLOCK, p["hyena_w1"], p["hyena_b1"], p["hyena_w2"], p["hyena_b2"],
                              p["hyena_w3"], p["hyena_freq"])
        spec = _hy_filter_spectrum(hy_a, filt, HYENA_BLOCK, hy_nb)
        z = _hy_long_conv(hy_a, hy_at, spec, 0, projc, hv, projc, hx1, p["hyena_bias"], HYENA_BLOCK)
        y_hy = _hy_long_conv(hy_a, hy_at, spec, 1, z, 0, projc, hx2, p["hyena_bias"], HYENA_BLOCK)
        h = _merge(h, [y_fn, y_ssm, y_hy], proj, p["sc_conv_w"], p["w_branch"], p["w_out"], tr)
        h = _ffn(h, p["norm_ffn"], p["w_up"], p["ffn_conv_w"], p["w_down"], tr)
    return _final_norm(h, norm_final, tr)


def kernel(x_prompt, x_sample, meta_tokens, norm_mix, w_in, ssm_conv_w, ssm_conv_b, ssm_dt_bias, ssm_a_log, ssm_d,
           ssm_norm, hyena_conv_w, hyena_w1, hyena_b1, hyena_w2, hyena_b2, hyena_w3, hyena_freq, hyena_bias,
           sc_conv_w, w_branch, w_out, norm_ffn, ffn_conv_w, w_up, w_down, norm_final):
    depth = w_in.shape[0]
    layers = []
    for l in range(depth):
        layers.append(dict(
            norm_mix=norm_mix[l], w_in=_prep_w_in(w_in[l]), ssm_conv_w=ssm_conv_w[l], ssm_conv_b=ssm_conv_b[l],
            ssm_dt_bias=ssm_dt_bias[l], ssm_a_log=ssm_a_log[l], ssm_d=ssm_d[l], ssm_norm=ssm_norm[l],
            hyena_conv_w=hyena_conv_w[l], hyena_w1=hyena_w1[l], hyena_b1=hyena_b1[l], hyena_w2=hyena_w2[l],
            hyena_b2=hyena_b2[l], hyena_w3=hyena_w3[l], hyena_freq=hyena_freq[l], hyena_bias=hyena_bias[l],
            sc_conv_w=sc_conv_w[l], w_branch=w_branch[l].astype(BF16), w_out=(0.5 * w_out[l]).astype(BF16),
            norm_ffn=norm_ffn[l], ffn_conv_w=ffn_conv_w[l], w_up=w_up[l].astype(BF16), w_down=w_down[l].astype(BF16)))
    y_prompt = _run_trunk(x_prompt, meta_tokens, norm_final, layers)
    y_sample = _run_trunk(x_sample, meta_tokens, norm_final, layers)
    return (y_prompt, y_sample)
```

```python
import functools
import math

import jax
import jax.numpy as jnp
from jax import lax
from jax.experimental import pallas as pl
from jax.experimental.pallas import tpu as pltpu

F32 = jnp.float32
BF16 = jnp.bfloat16
HI = lax.Precision.HIGHEST

NORM_EPS = 1e-6
N_META = 16
BRANCH_W = 512
FNET_GW = 128
SSM_GROUPS = 2
SSM_HPG = 4
SSM_HEAD_DIM = 64
SSM_STATE = 128
SSM_CHUNK = 128
HYENA_ORDER = 2
HYENA_EMB = 33
HYENA_TARGET = 1e-2
HYENA_FAST = 0.3
HYENA_SLOW = 1.5

ROW_TILE = 700
TOKEN_TILE = 512
HYENA_BLOCK = 704
HYENA_VMEM_BUDGET = 32 * 1024 * 1024
HYENA_FREQ_CHUNK = 32
MXU_COLS = 256
PROJ_CHUNK = 2 * MXU_COLS
SSD_UNROLL = 4
HALO = 16
VMEM_LIMIT = 56 * 1024 * 1024

CV_XBC = 0
CV_HY = 1024
N_CONV = 2560
COL_FN = 0
COL_Z = 512
COL_GATE = 1024
COL_SC = 5120
COL_DT = 6656
N_PLAIN = 6912


def _cp(*sem):
    return pltpu.CompilerParams(dimension_semantics=sem, vmem_limit_bytes=VMEM_LIMIT)


def _pick_tile(n, target, mult=16):
    k = max(1, -(-n // target))
    t = -(-n // k)
    return -(-t // mult) * mult


def _silu(x):
    h = 0.5 * x
    return h * (1.0 + jnp.tanh(h))


def _norm_matmul_kernel(x_ref, g_ref, w_ref, o_ref, *, tn):
    x = x_ref[...]
    ms = jnp.mean(x * x, axis=-1, keepdims=True)
    xn = (x * lax.rsqrt(ms + NORM_EPS) * g_ref[...]).astype(BF16)
    for s0 in range(0, w_ref.shape[1], tn):
        o_ref[:, s0:s0 + tn] = jnp.dot(xn, w_ref[:, s0:s0 + tn], preferred_element_type=F32).astype(o_ref.dtype)


def _norm_matmul(x, g, w, tm, tn):
    n, d = x.shape
    m = w.shape[1]
    assert m % tn == 0
    tm = min(tm, -(-n // HALO) * HALO)
    return pl.pallas_call(
        functools.partial(_norm_matmul_kernel, tn=tn),
        grid=(pl.cdiv(n, tm),),
        in_specs=[pl.BlockSpec((tm, d), lambda i: (i, 0)),
                  pl.BlockSpec((1, d), lambda i: (0, 0), pipeline_mode=pl.Buffered(1)),
                  pl.BlockSpec((d, m), lambda i: (0, 0), pipeline_mode=pl.Buffered(1))],
        out_specs=pl.BlockSpec((tm, m), lambda i: (i, 0)),
        out_shape=jax.ShapeDtypeStruct((n, m), BF16),
        compiler_params=_cp("arbitrary"),
        name="norm_matmul",
    )(x, g.reshape(1, d), w)


def _conv3(x, prev_row, next_row, w, row0, t_len):
    n = x.shape[0]
    li = lax.broadcasted_iota(jnp.int32, (n, 1), 0)
    gi = li + row0
    xm = pltpu.roll(x, 1, 0)
    xm = jnp.where(li == 0, prev_row, xm)
    xm = jnp.where(gi >= 1, xm, 0.0)
    xp = pltpu.roll(x, n - 1, 0)
    xp = jnp.where(li == n - 1, next_row, xp)
    xp = jnp.where(gi + 1 < t_len, xp, 0.0)
    return xm * w[0:1] + x * w[1:2] + xp * w[2:3]


def _last_row(halo_ref):
    return halo_ref[0].astype(F32)[HALO - 1:HALO]


def _first_row(halo_ref):
    return halo_ref[0].astype(F32)[0:1]


def _halo_specs(tr, cw, t_len, col_blk):
    nh = tr // HALO
    last = t_len // HALO - 1
    main = pl.BlockSpec((1, tr, cw), lambda b, i: (b, i, col_blk))
    prev = pl.BlockSpec((1, HALO, cw), lambda b, i: (b, jnp.maximum(i * nh - 1, 0), col_blk))
    nxt = pl.BlockSpec((1, HALO, cw), lambda b, i: (b, jnp.minimum((i + 1) * nh, last), col_blk))
    return main, prev, nxt


def _fnet_kernel(a_ref, u_ref, cs_ref, o_ref, *, tmf):
    p = jnp.dot(a_ref[0], u_ref[0], preferred_element_type=F32)
    pc = p[:tmf].astype(BF16)
    ps = p[tmf:].astype(BF16)
    for c0 in range(0, BRANCH_W, MXU_COLS):
        blk = slice(c0, c0 + MXU_COLS)
        y = (jnp.dot(pc[:, blk], cs_ref[0, blk, blk], preferred_element_type=F32)
             - jnp.dot(ps[:, blk], cs_ref[1, blk, blk], preferred_element_type=F32))
        o_ref[0, :, blk] = y.astype(o_ref.dtype)


def _fnet_tables(t, tmf):
    nm = -(-t // tmf)
    assert tmf % HALO == 0
    k = jnp.arange(t, dtype=jnp.int32)[None, :]
    ang = lambda j: ((j[:, None] * k) % t).astype(F32) * (2.0 * math.pi / t)
    ang_hi = ang(jnp.arange(nm * tmf // HALO, dtype=jnp.int32) * HALO)[:, None, :]
    ang_lo = ang(jnp.arange(HALO, dtype=jnp.int32))[None, :, :]
    c = (jnp.cos(ang_hi) * jnp.cos(ang_lo) - jnp.sin(ang_hi) * jnp.sin(ang_lo)).reshape(nm * tmf, t)
    s = (jnp.sin(ang_hi) * jnp.cos(ang_lo) + jnp.cos(ang_hi) * jnp.sin(ang_lo)).reshape(nm * tmf, t)
    valid = jnp.arange(nm * tmf, dtype=jnp.int32)[:, None] < t
    c = jnp.where(valid, c, 0.0).astype(BF16).reshape(nm, tmf, t)
    s = jnp.where(valid, s, 0.0).astype(BF16).reshape(nm, tmf, t)
    a = jnp.concatenate([c, s], axis=1)
    jj = jnp.arange(BRANCH_W, dtype=jnp.int32)[:, None]
    kk = jnp.arange(BRANCH_W, dtype=jnp.int32)[None, :]
    same = (jj // FNET_GW) == (kk // FNET_GW)
    ang2 = (((jj % FNET_GW) * (kk % FNET_GW)) % FNET_GW).astype(F32) * (2.0 * math.pi / FNET_GW)
    scale = 1.0 / math.sqrt(t * FNET_GW)
    cc = jnp.where(same, jnp.cos(ang2), 0.0) * scale
    sc = jnp.where(same, jnp.sin(ang2), 0.0) * scale
    return a, jnp.stack([cc, sc]).astype(BF16)


def _fnet_branch(proj, tmf):
    b, t, _ = proj.shape
    a, cs = _fnet_tables(t, tmf)
    nm = a.shape[0]
    return pl.pallas_call(
        functools.partial(_fnet_kernel, tmf=tmf),
        grid=(nm, b),
        in_specs=[pl.BlockSpec((1, 2 * tmf, t), lambda i, j: (i, 0, 0)),
                  pl.BlockSpec((1, t, BRANCH_W), lambda i, j: (j, 0, COL_FN // BRANCH_W)),
                  pl.BlockSpec((2, BRANCH_W, BRANCH_W), lambda i, j: (0, 0, 0))],
        out_specs=pl.BlockSpec((1, tmf, BRANCH_W), lambda i, j: (j, i, 0)),
        out_shape=jax.ShapeDtypeStruct((b, t, BRANCH_W), BF16),
        compiler_params=_cp("arbitrary", "arbitrary"),
        name="fnet",
    )(a, proj, cs)


def _hyena_filter_kernel(w1t_ref, w1c_ref, w1s_ref, b1_ref, w2_ref, b2_ref, fq_ref, w3f_ref, w3b_ref, dl_ref,
                         o_ref, hid_ref, *, t, half):
    row = lax.broadcasted_iota(jnp.int32, (t, 1), 0)
    pos_f = row.astype(F32)
    pos_b = (t - row).astype(F32)
    bands = (HYENA_EMB - 1) // 2

    @pl.when(pl.program_id(0) == 0)
    def _():
        lane = lax.broadcasted_iota(jnp.int32, (1, 2 * bands), 1)
        fr = 1e-4 + (lane % bands).astype(F32) * ((bands - 1 - 1e-4) / (bands - 1))
        pos = jnp.where(lane < bands, pos_f, pos_b)
        arg = (pos * (2.0 * math.pi / t)) * fr
        pre = ((pos_f * (1.0 / (t - 1))) * w1t_ref[0:1, :] + (pos_b * (1.0 / (t - 1))) * w1t_ref[1:2, :]
               + jnp.dot(jnp.cos(arg), w1c_ref[...], precision=HI, preferred_element_type=F32)
               - jnp.dot(jnp.sin(arg), w1s_ref[...], precision=HI, preferred_element_type=F32)
               + b1_ref[...])
        fq = fq_ref[...]
        h1 = jnp.sin(fq * pre)
        hid_ref[...] = jnp.sin(fq * (jnp.dot(h1, w2_ref[...], precision=HI, preferred_element_type=F32)
                                     + b2_ref[...]))

    def taps(w3_ref, pos):
        dec = jnp.exp(-(pos * (1.0 / (t - 1))) * dl_ref[...])
        return jnp.dot(hid_ref[...], w3_ref[...], precision=HI, preferred_element_type=F32) * dec

    hf = taps(w3f_ref, pos_f)
    hb = jnp.where(row >= 1, taps(w3b_ref, pos_b), 0.0)
    l1 = jnp.sum(jnp.abs(hf), axis=0, keepdims=True) + jnp.sum(jnp.abs(hb), axis=0, keepdims=True)
    inv = 1.0 / l1
    cw = o_ref.shape[2]
    o_ref[0, half - t:half, :] = (hb * inv).astype(o_ref.dtype)
    o_ref[0, half:half + t, :] = (hf * inv).astype(o_ref.dtype)
    if half > t:
        o_ref[0, 0:half - t, :] = jnp.zeros((half - t, cw), o_ref.dtype)
        o_ref[0, half + t:, :] = jnp.zeros((half - t, cw), o_ref.dtype)


def _hyena_filters(t, half, w1, b1, w2, b2, w3, freq):
    cw = 256
    nb = BRANCH_W // cw
    bands = (HYENA_EMB - 1) // 2
    nf = w2.shape[0]
    max_decay = math.log(HYENA_TARGET) / HYENA_FAST
    min_decay = math.log(HYENA_TARGET) / HYENA_SLOW
    deltas = jnp.abs(jnp.linspace(min_decay, max_decay, BRANCH_W, dtype=F32)).reshape(1, BRANCH_W)
    full = lambda shape: pl.BlockSpec(shape, lambda g: (0,) * len(shape))
    w1f = w1.astype(F32)
    zero = jnp.zeros_like
    bd = lambda m: jnp.concatenate([jnp.concatenate([m, zero(m)], axis=1), jnp.concatenate([zero(m), m], axis=1)])
    twice = lambda v: jnp.concatenate([v, v]).reshape(1, 2 * nf).astype(F32)
    w1t = bd(w1f[0:1])
    w3f32 = w3.astype(F32)
    w3_top = jnp.concatenate([w3f32, zero(w3f32)], axis=0)
    w3_bot = jnp.concatenate([zero(w3f32), w3f32], axis=0)
    return pl.pallas_call(
        functools.partial(_hyena_filter_kernel, t=t, half=half),
        grid=(HYENA_ORDER * nb,),
        in_specs=[full((2, 2 * nf)), full((2 * bands, 2 * nf)), full((2 * bands, 2 * nf)), full((1, 2 * nf)),
                  full((2 * nf, 2 * nf)), full((1, 2 * nf)), full((1, 2 * nf)),
                  pl.BlockSpec((2 * nf, cw), lambda g: (0, (g // nb) * 2 * nb + g % nb)),
                  pl.BlockSpec((2 * nf, cw), lambda g: (0, (g // nb) * 2 * nb + nb + g % nb)),
                  pl.BlockSpec((1, cw), lambda g: (0, g % nb))],
        out_specs=pl.BlockSpec((1, 2 * half, cw), lambda g: (g // nb, 0, g % nb)),
        out_shape=jax.ShapeDtypeStruct((HYENA_ORDER, 2 * half, BRANCH_W), BF16),
        scratch_shapes=[pltpu.VMEM((t, 2 * nf), F32)],
        compiler_params=_cp("arbitrary"),
        name="hyena_filter",
    )(w1t, bd(w1f[1:1 + bands]), bd(w1f[1 + bands:]), twice(b1), bd(w2.astype(F32)), twice(b2), twice(freq),
      w3_top, w3_bot, deltas)


def _hyena_tables(p_len):
    f = jnp.arange(p_len, dtype=jnp.int32)[:, None]
    n = jnp.arange(p_len, dtype=jnp.int32)[None, :]
    ang = (((2 * f + 1) * n) % (4 * p_len)).astype(F32) * (math.pi / (2 * p_len))
    a = jnp.concatenate([jnp.cos(ang), -jnp.sin(ang)], axis=0).astype(BF16)
    return a, a.T


def _hy_spec_kernel(a_ref, seg_ref, prev_ref, o_ref, *, p_len):
    g = jnp.dot(a_ref[...], seg_ref[0, 0], preferred_element_type=F32)
    gp = jnp.dot(a_ref[...], prev_ref[0, 0], preferred_element_type=F32)
    first = prev_ref[0, 0, 0:1, :].astype(F32)
    f = lax.broadcasted_iota(jnp.int32, (p_len, 1), 0)
    sign = jnp.where(f % 2 == 0, 1.0, -1.0)
    scale = 1.0 / p_len
    o_ref[0, 0, :p_len] = ((g[:p_len] - sign * gp[p_len:]) * scale).astype(o_ref.dtype)
    o_ref[0, 0, p_len:] = ((g[p_len:] + sign * (gp[:p_len] - first)) * scale).astype(o_ref.dtype)


def _hy_filter_spectrum(a, filt, p_len, nb):
    order, _, w = filt.shape
    lags = filt.reshape(order, 2 * nb, p_len, w)
    return pl.pallas_call(
        functools.partial(_hy_spec_kernel, p_len=p_len),
        grid=(order, 2 * nb - 1),
        in_specs=[pl.BlockSpec((2 * p_len, p_len), lambda o, d: (0, 0)),
                  pl.BlockSpec((1, 1, p_len, w), lambda o, d: (o, d + 1, 0, 0)),
                  pl.BlockSpec((1, 1, p_len, w), lambda o, d: (o, d, 0, 0))],
        out_specs=pl.BlockSpec((1, 1, 2 * p_len, w), lambda o, d: (o, d, 0, 0)),
        out_shape=jax.ShapeDtypeStruct((order, 2 * nb - 1, 2 * p_len, w), BF16),
        compiler_params=_cp("arbitrary", "arbitrary"),
        name="hyena_filter_spectrum",
    )(a, lags, lags)


def _hy_conv_kernel(a_ref, at_ref, k_ref, v_ref, g_ref, bias_ref, o_ref, u_scr, y_scr, *, t_len, p_len, nb, fc):
    cw = v_ref.shape[2]

    def block_rows(i):
        return i * p_len, min(p_len, t_len - i * p_len)

    for j in range(nb):
        r0, n = block_rows(j)
        u = v_ref[0, r0:r0 + n, :]
        if n < p_len:
            u = jnp.concatenate([u, jnp.zeros((p_len - n, cw), u.dtype)], axis=0)
        u_scr[j, :p_len] = jnp.dot(a_ref[:p_len], u, preferred_element_type=F32).astype(BF16)
        u_scr[j, p_len:] = jnp.dot(a_ref[p_len:], u, preferred_element_type=F32).astype(BF16)

    for i in range(nb):
        for r in range(0, p_len, fc):
            acc_r = acc_i = None
            for j in range(nb):
                d = i - j + nb - 1
                ur, ui = u_scr[j, r:r + fc, :], u_scr[j, p_len + r:p_len + r + fc, :]
                kr, ki = k_ref[0, d, r:r + fc, :], k_ref[0, d, p_len + r:p_len + r + fc, :]
                tr_, ti_ = ur * kr - ui * ki, ur * ki + ui * kr
                acc_r, acc_i = (tr_, ti_) if acc_r is None else (acc_r + tr_, acc_i + ti_)
            y_scr[i % 2, r:r + fc, :] = acc_r
            y_scr[i % 2, p_len + r:p_len + r + fc, :] = acc_i
        conv = jnp.dot(at_ref[...], y_scr[i % 2], preferred_element_type=F32)
        r0, n = block_rows(i)
        v = v_ref[0, r0:r0 + n, :].astype(F32)
        o_ref[0, r0:r0 + n, :] = (g_ref[0, r0:r0 + n, :].astype(F32) * (conv[:n] + v * bias_ref[0])).astype(o_ref.dtype)


def _hy_long_conv(a, at, spec, order, v, v_col, gate, gate_col, bias, p_len):
    b, t, _ = v.shape
    nb = -(-t // p_len)
    full_bytes = 2 * BRANCH_W * ((2 * nb - 1) * 2 * p_len + nb * 2 * p_len + 6 * t)
    cw = BRANCH_W if full_bytes <= HYENA_VMEM_BUDGET else BRANCH_W // 2
    nh = BRANCH_W // cw
    return pl.pallas_call(
        functools.partial(_hy_conv_kernel, t_len=t, p_len=p_len, nb=nb, fc=HYENA_FREQ_CHUNK),
        grid=(nh, b),
        in_specs=[pl.BlockSpec((2 * p_len, p_len), lambda c, i: (0, 0), pipeline_mode=pl.Buffered(1)),
                  pl.BlockSpec((p_len, 2 * p_len), lambda c, i: (0, 0), pipeline_mode=pl.Buffered(1)),
                  pl.BlockSpec((1, 2 * nb - 1, 2 * p_len, cw), lambda c, i: (order, 0, 0, c),
                               pipeline_mode=pl.Buffered(1)),
                  pl.BlockSpec((1, t, cw), lambda c, i: (i, 0, v_col * nh + c)),
                  pl.BlockSpec((1, t, cw), lambda c, i: (i, 0, gate_col * nh + c)),
                  pl.BlockSpec((1, 1, cw), lambda c, i: (order, 0, c))],
        out_specs=pl.BlockSpec((1, t, cw), lambda c, i: (i, 0, c)),
        out_shape=jax.ShapeDtypeStruct((b, t, BRANCH_W), BF16),
        scratch_shapes=[pltpu.VMEM((nb, 2 * p_len, cw), BF16), pltpu.VMEM((2, 2 * p_len, cw), BF16)],
        compiler_params=_cp("arbitrary", "arbitrary"),
        name="hyena_conv",
    )(a, at, spec, v, gate, bias.reshape(HYENA_ORDER, 1, BRANCH_W))


def _softplus(x):
    return jnp.maximum(x, 0.0) + jnp.log(1.0 + jnp.exp(-jnp.abs(x)))


def _ssd_kernel(xs_ref, b_ref, c_ref, z_ref, dt_ref, dtb_ref, an_ref, dsk_ref, nw_ref, tri_ref, exp_ref, o_ref,
                yf_scr, yb_scr, *, t_len, nc, unroll):
    q = SSM_CHUNK
    gw = SSM_HPG * SSM_HEAD_DIM
    front = q - N_META
    ri = lax.broadcasted_iota(jnp.int32, (q, q), 0)
    ci = lax.broadcasted_iota(jnp.int32, (q, q), 1)
    tri = (ri >= ci, ri <= ci)
    head_of_lane = lax.broadcasted_iota(jnp.int32, (1, gw), 1) // SSM_HEAD_DIM
    lane = lax.broadcasted_iota(jnp.int32, (1, q), 1)
    dt_lane_mask = (lane < 2 * SSM_HPG).astype(F32)
    low_half = lane < SSM_HEAD_DIM
    dtb = dtb_ref[...]
    an = an_ref[...]
    masked_out = -1e30

    def widen(cols):
        return jnp.concatenate([jnp.where(low_half, cols[0], cols[1]), jnp.where(low_half, cols[2], cols[3])], axis=1)

    def cumsum(d, x):
        hi = x.astype(BF16)
        r1 = x - hi.astype(F32)
        mid = r1.astype(BF16)
        lo = (r1 - mid.astype(F32)).astype(BF16)
        s = jnp.dot(tri_ref[d], jnp.concatenate([hi, mid, lo], axis=1), preferred_element_type=F32)
        return s[:, :q] + s[:, q:2 * q] + s[:, 2 * q:]

    def load(ref, r0, nrows):
        return ref[0, pl.ds(r0, nrows), :]

    def dt_of(raw):
        return _softplus(raw.astype(F32) + dtb) * dt_lane_mask

    def chunk_data(c):
        r0 = c * q - front if isinstance(c, int) else pl.multiple_of(c * q - front, HALO)
        return load(xs_ref, r0, q), load(b_ref, r0, q), load(c_ref, r0, q), dt_of(load(dt_ref, r0, q))

    def chunk0_data():
        def pad(x):
            return jnp.concatenate([jnp.zeros((front, x.shape[1]), x.dtype), x], axis=0)
        return (pad(load(xs_ref, 0, N_META)), pad(load(b_ref, 0, N_META)), pad(load(c_ref, 0, N_META)),
                pad(dt_of(load(dt_ref, 0, N_META))))

    def stage_local(job, cs_n):
        (xs, bm, cm, dt), d = job
        lanes = [d * SSM_HPG + r for r in range(SSM_HPG)]
        cs_cols = [jnp.broadcast_to(cs_n[:, ln:ln + 1], (q, q)) for ln in lanes]
        cs_w = widen(cs_cols)
        dt_w = jnp.dot(dt.astype(BF16), exp_ref[d], preferred_element_type=F32).astype(BF16)
        tot = cs_w[q - 1:q, :] if d == 0 else cs_w[0:1, :]
        xt = xs * dt_w
        xd = xt * jnp.exp2(tot - cs_w).astype(BF16)
        s_new = lax.dot_general(bm, xd, (((0,), (0,)), ((), ())), preferred_element_type=F32)
        cb = lax.dot_general(cm, bm, (((1,), (1,)), ((), ())), preferred_element_type=F32)
        return cs_cols, xt, cb, cm, s_new, jnp.exp2(cs_w), jnp.exp2(tot), cs_n.T

    def stage_carry(loc, h):
        _, _, _, cm, s_new, ecs, etot, _ = loc
        y_off = jnp.dot(cm, h.astype(BF16), preferred_element_type=F32) * ecs
        return y_off, h * etot + s_new

    def stage_diag(job, cs_n, loc, y_off):
        d = job[1]
        cs_cols, xt, cb = loc[:3]
        cs_t = loc[7]
        ms, xm = [], []
        for r in range(SSM_HPG):
            ln = d * SSM_HPG + r
            dec = jnp.exp2(jnp.where(tri[d], cs_cols[r] - cs_t[ln:ln + 1, :], masked_out))
            ms.append((cb * dec).astype(BF16))
            xm.append(jnp.where(head_of_lane == r, xt, jnp.zeros_like(xt)))
        return y_off + jnp.dot(jnp.concatenate(ms, axis=1), jnp.concatenate(xm, axis=0), preferred_element_type=F32)

    def process_all(fwd_data, bwd_data, hf, hb):
        jobs = [(x, 0) for x in fwd_data] + [(x, 1) for x in bwd_data]
        cs = [cumsum(d, data[3] * an) for data, d in jobs]
        ys, pending = [], None
        for k, (job, c) in enumerate(zip(jobs, cs)):
            loc = stage_local(job, c)
            if pending is not None:
                ys.append(stage_diag(*pending))
            y_off, h = stage_carry(loc, hf if job[1] == 0 else hb)
            hf, hb = (h, hb) if job[1] == 0 else (hf, h)
            pending = (job, c, loc, y_off)
        ys.append(stage_diag(*pending))
        return ys[:len(fwd_data)], ys[len(fwd_data):], hf, hb

    def finish(y, r_out, nrows):
        y = y + load(xs_ref, r_out, nrows).astype(F32) * dsk_ref[...]
        y = y * _silu(load(z_ref, r_out, nrows).astype(F32))
        y = y * lax.rsqrt(jnp.mean(y * y, axis=-1, keepdims=True) + NORM_EPS)
        o_ref[0, pl.ds(r_out, nrows), :] = (y * nw_ref[...]).astype(o_ref.dtype)

    def rows(c):
        return pl.ds(c * q if isinstance(c, int) else pl.multiple_of(c * q, q), q)

    def scan_step(cf, carry):
        hf, hb = carry
        cb_ = [nc - 1 - c for c in cf]
        data = lambda c: chunk0_data() if isinstance(c, int) and c == 0 else chunk_data(c)
        yf, yb, hf, hb = process_all([data(c) for c in cf], [data(c) for c in cb_], hf, hb)
        for c, y in zip(cf, yf):
            yf_scr[rows(c), :] = y
        for c, y in zip(cb_, yb):
            yb_scr[rows(c), :] = y
        return hf, hb

    n_mid = (nc - unroll - 1) // unroll
    h0 = jnp.zeros((SSM_STATE, gw), F32)
    carry = scan_step(list(range(unroll)), (h0, h0))
    carry = lax.fori_loop(1, 1 + n_mid, lambda k, c: scan_step([k * unroll + j for j in range(unroll)], c), carry)
    scan_step(list(range((1 + n_mid) * unroll, nc)), carry)

    finish(yf_scr[front:q, :] + yb_scr[front:q, :], 0, N_META)

    def finish_body(c, carry):
        finish(yf_scr[rows(c), :] + yb_scr[rows(c), :], pl.multiple_of(c * q - front, HALO), q)
        return carry

    lax.fori_loop(1, nc, finish_body, 0, unroll=unroll)


def _ssd_branch(xbc, proj, dt_bias, a_log, d_skip, norm_w):
    b, t, _ = xbc.shape
    q = SSM_CHUNK
    gw = SSM_HPG * SSM_HEAD_DIM
    nc = (q - N_META + t) // q
    assert nc * q == q - N_META + t, "sequence length minus meta tokens must be a multiple of the SSD chunk"
    a = -jnp.exp(a_log.astype(F32)).reshape(2, SSM_GROUPS, SSM_HPG)
    dtb = dt_bias.astype(F32).reshape(2, SSM_GROUPS, SSM_HPG)
    pad = lambda v: jnp.pad(jnp.transpose(v, (1, 0, 2)).reshape(SSM_GROUPS, 1, 2 * SSM_HPG),
                            ((0, 0), (0, 0), (0, q - 2 * SSM_HPG)))
    a_n = pad(a * math.log2(math.e))
    dtb_n = pad(dtb)
    ri = jnp.arange(q, dtype=jnp.int32)[:, None]
    ci = jnp.arange(q, dtype=jnp.int32)[None, :]
    tri = jnp.stack([ri >= ci, ri <= ci]).astype(BF16)
    cj = jnp.arange(gw, dtype=jnp.int32)[None, :] // SSM_HEAD_DIM
    expand = jnp.stack([ri == cj, ri == cj + SSM_HPG]).astype(BF16)
    dsk = jnp.repeat(d_skip.astype(F32).reshape(SSM_GROUPS, 1, SSM_HPG), SSM_HEAD_DIM, axis=-1)
    nw = norm_w.astype(F32).reshape(SSM_GROUPS, 1, gw)
    par = lambda shape: pl.BlockSpec((None,) + shape, lambda i, g: (g, 0, 0))
    return pl.pallas_call(
        functools.partial(_ssd_kernel, t_len=t, nc=nc, unroll=min(SSD_UNROLL, nc - 1)),
        grid=(b, SSM_GROUPS),
        in_specs=[pl.BlockSpec((1, t, gw), lambda i, g: (i, 0, g)),
                  pl.BlockSpec((1, t, SSM_STATE), lambda i, g: (i, 0, 4 + g)),
                  pl.BlockSpec((1, t, SSM_STATE), lambda i, g: (i, 0, 6 + g)),
                  pl.BlockSpec((1, t, gw), lambda i, g: (i, 0, COL_Z // gw + g)),
                  pl.BlockSpec((1, t, q), lambda i, g: (i, 0, COL_DT // q + g)),
                  par((1, q)), par((1, q)), par((1, gw)), par((1, gw)),
                  pl.BlockSpec((2, q, q), lambda i, g: (0, 0, 0)),
                  pl.BlockSpec((2, q, gw), lambda i, g: (0, 0, 0))],
        out_specs=pl.BlockSpec((1, t, gw), lambda i, g: (i, 0, g)),
        out_shape=jax.ShapeDtypeStruct((b, t, BRANCH_W), BF16),
        scratch_shapes=[pltpu.VMEM((nc * q, gw), F32), pltpu.VMEM((nc * q, gw), F32)],
        compiler_params=_cp("arbitrary", "arbitrary"),
        name="ssd",
    )(xbc, xbc, xbc, proj, proj, dtb_n, a_n, dsk, nw, tri, expand)


def _merge_kernel(h_ref, y0_ref, y1_ref, y2_ref, g0_ref, g1_ref, g2_ref, g3_ref, bg_ref, cg_ref, xi_ref, cgp_ref,
                  xip_ref, cgn_ref, xin_ref, scw_ref, wb_ref, wo_ref, o_ref, m_scr, *, tr, t_len):
    u = cg_ref[0].astype(F32) * xi_ref[0].astype(F32)
    prev = _last_row(cgp_ref) * _last_row(xip_ref)
    nxt = _first_row(cgn_ref) * _first_row(xin_ref)
    y_sc = (bg_ref[0].astype(F32) * _conv3(u, prev, nxt, scw_ref[...], pl.program_id(1) * tr, t_len)).astype(BF16)
    ys = (y0_ref[0], y1_ref[0], y2_ref[0], y_sc)
    gs = (g0_ref, g1_ref, g2_ref, g3_ref)
    for s0, w in _col_chunks(h_ref.shape[2]):
        merged = None
        for k in range(len(ys)):
            term = ((1.0 + jnp.tanh(gs[k][0, :, s0:s0 + w].astype(F32)))
                    * jnp.dot(ys[k], wb_ref[k, :, s0:s0 + w], preferred_element_type=F32))
            merged = term if merged is None else merged + term
        m_scr[:, s0:s0 + w] = merged.astype(BF16)
    o_ref[0] = h_ref[0] + jnp.dot(m_scr[...], wo_ref[...], preferred_element_type=F32)


def _merge(h, ys, proj, sc_w, wb, wo, tr):
    b, t, d = h.shape
    nb = wb.shape[0]
    row = lambda w, c: pl.BlockSpec((1, tr, w), lambda b_, i: (b_, i, c))
    c0 = COL_SC // BRANCH_W
    bg, _, _ = _halo_specs(tr, BRANCH_W, t, c0)
    cg, cgp, cgn = _halo_specs(tr, BRANCH_W, t, c0 + 1)
    xi, xip, xin = _halo_specs(tr, BRANCH_W, t, c0 + 2)
    return pl.pallas_call(
        functools.partial(_merge_kernel, tr=tr, t_len=t),
        grid=(b, pl.cdiv(t, tr)),
        in_specs=[row(d, 0)] + [row(BRANCH_W, 0)] * len(ys) + [row(d, COL_GATE // d + k) for k in range(nb)]
                 + [bg, cg, xi, cgp, xip, cgn, xin, _resident((3, BRANCH_W)), _resident((nb, BRANCH_W, d)),
                    _resident((d, d))],
        out_specs=row(d, 0),
        out_shape=jax.ShapeDtypeStruct((b, t, d), F32),
        scratch_shapes=[pltpu.VMEM((tr, d), BF16)],
        compiler_params=_cp("arbitrary", "arbitrary"),
        name="merge",
    )(h, *ys, *([proj] * (nb + 7)), sc_w, wb, wo)


def _rms_rows(x, g):
    return x * lax.rsqrt(jnp.mean(x * x, axis=-1, keepdims=True) + NORM_EPS) * g


def _stage_normed_rows(xs_ref, x_ref, xp_ref, xn_ref, g, row0, tr, t_len):
    li = lax.broadcasted_iota(jnp.int32, (tr, 1), 0)
    xs_ref[HALO:HALO + tr, :] = jnp.where(row0 + li < t_len, _rms_rows(x_ref[0], g), 0.0).astype(BF16)
    xs_ref[0:HALO, :] = jnp.where(row0 > 0, _rms_rows(xp_ref[0], g), 0.0).astype(BF16)
    xs_ref[HALO + tr:, :] = jnp.where(row0 + tr < t_len, _rms_rows(xn_ref[0], g), 0.0).astype(BF16)


def _conv3_rows(u, w):
    n = u.shape[0]
    return pltpu.roll(u, 1, 0) * w[0:1] + u * w[1:2] + pltpu.roll(u, n - 1, 0) * w[2:3]


def _row_halo_specs(tr, d, t_len):
    nh = tr // HALO
    last = t_len // HALO - 1
    return (pl.BlockSpec((1, tr, d), lambda b, i: (b, i, 0)),
            pl.BlockSpec((1, HALO, d), lambda b, i: (b, jnp.maximum(i * nh - 1, 0), 0)),
            pl.BlockSpec((1, HALO, d), lambda b, i: (b, jnp.minimum((i + 1) * nh, last), 0)))


def _resident(shape):
    return pl.BlockSpec(shape, lambda b, i: (0,) * len(shape), pipeline_mode=pl.Buffered(1))


def _col_chunks(n):
    return tuple((s0, min(PROJ_CHUNK, n - s0)) for s0 in range(0, n, PROJ_CHUNK))


def _conv_proj_kernel(x_ref, xp_ref, xn_ref, g_ref, w_ref, c_ref, b_ref, o_ref, xs_ref, *, tr, t_len, n_act):
    _stage_normed_rows(xs_ref, x_ref, xp_ref, xn_ref, g_ref[...], pl.program_id(1) * tr, tr, t_len)
    xs = xs_ref[...]
    for s0, w in _col_chunks(w_ref.shape[1]):
        u = jnp.dot(xs, w_ref[:, s0:s0 + w], preferred_element_type=F32)
        y = _conv3_rows(u, c_ref[:, s0:s0 + w])[HALO:HALO + tr] + b_ref[:, s0:s0 + w]
        o_ref[0, :, s0:s0 + w] = (_silu(y) if s0 < n_act else y).astype(o_ref.dtype)


def _conv_proj(h, g, w, taps, bias, n_act, tr):
    b, t, d = h.shape
    m = w.shape[1]
    main, prev, nxt = _row_halo_specs(tr, d, t)
    return pl.pallas_call(
        functools.partial(_conv_proj_kernel, tr=tr, t_len=t, n_act=n_act),
        grid=(b, pl.cdiv(t, tr)),
        in_specs=[main, prev, nxt, _resident((1, d)), _resident((d, m)), _resident((3, m)), _resident((1, m))],
        out_specs=pl.BlockSpec((1, tr, m), lambda b_, i: (b_, i, 0)),
        out_shape=jax.ShapeDtypeStruct((b, t, m), BF16),
        scratch_shapes=[pltpu.VMEM((tr + 2 * HALO, d), BF16)],
        compiler_params=_cp("arbitrary", "arbitrary"),
        name="conv_proj",
    )(h, h, h, g.reshape(1, d), w, taps, bias)


def _ffn_kernel(x_ref, xp_ref, xn_ref, g_ref, wu_ref, cw_ref, wd_ref, o_ref, xs_ref, gate_ref, *, tr, t_len):
    _stage_normed_rows(xs_ref, x_ref, xp_ref, xn_ref, g_ref[...], pl.program_id(1) * tr, tr, t_len)
    xs = xs_ref[...]
    dff = wd_ref.shape[0]
    for s0, w in _col_chunks(dff):
        a = _conv3_rows(jnp.dot(xs, wu_ref[:, s0:s0 + w], preferred_element_type=F32), cw_ref[:, s0:s0 + w])
        v = _conv3_rows(jnp.dot(xs, wu_ref[:, dff + s0:dff + s0 + w], preferred_element_type=F32),
                        cw_ref[:, dff + s0:dff + s0 + w])
        gate_ref[:, s0:s0 + w] = (_silu(a) * v)[HALO:HALO + tr].astype(BF16)
    o_ref[0] = x_ref[0] + jnp.dot(gate_ref[...], wd_ref[...], preferred_element_type=F32)


def _ffn(h, g, w_up, conv_w, w_down, tr):
    b, t, d = h.shape
    dff = w_down.shape[0]
    main, prev, nxt = _row_halo_specs(tr, d, t)
    return pl.pallas_call(
        functools.partial(_ffn_kernel, tr=tr, t_len=t),
        grid=(b, pl.cdiv(t, tr)),
        in_specs=[main, prev, nxt, _resident((1, d)), _resident((d, 2 * dff)), _resident((3, 2 * dff)),
                  _resident((dff, d))],
        out_specs=pl.BlockSpec((1, tr, d), lambda b_, i: (b_, i, 0)),
        out_shape=jax.ShapeDtypeStruct((b, t, d), F32),
        scratch_shapes=[pltpu.VMEM((tr + 2 * HALO, d), BF16), pltpu.VMEM((tr, dff), BF16)],
        compiler_params=_cp("arbitrary", "arbitrary"),
        name="ffn",
    )(h, h, h, g.reshape(1, d), w_up, conv_w, w_down)


def _final_norm_kernel(x_ref, xn_ref, g_ref, o_ref, *, tr):
    g = g_ref[...]
    o_ref[0, 0:tr - N_META, :] = _rms_rows(x_ref[0], g)[N_META:]
    o_ref[0, tr - N_META:, :] = _rms_rows(xn_ref[0], g)


def _final_norm(x, g, tr):
    b, t, d = x.shape
    assert N_META == HALO
    main, _, nxt = _row_halo_specs(tr, d, t)
    return pl.pallas_call(
        functools.partial(_final_norm_kernel, tr=tr),
        grid=(b, pl.cdiv(t - N_META, tr)),
        in_specs=[main, nxt, pl.BlockSpec((1, d), lambda b_, i: (0, 0))],
        out_specs=pl.BlockSpec((1, tr, d), lambda b_, i: (b_, i, 0)),
        out_shape=jax.ShapeDtypeStruct((b, t - N_META, d), F32),
        compiler_params=_cp("arbitrary", "arbitrary"),
        name="final_norm",
    )(x, x, g.reshape(1, d))


def _prep_w_in(w_in):
    d = w_in.shape[0]
    o_z, o_xbc, o_dt = BRANCH_W, 2 * BRANCH_W, 2 * BRANCH_W + 1024
    ndt = 2 * SSM_GROUPS * SSM_HPG
    o_hy = o_dt + ndt
    o_sc = o_hy + 3 * BRANCH_W
    o_gate = o_sc + 3 * BRANCH_W
    w_dt = w_in[:, o_dt:o_hy].reshape(d, 2, SSM_GROUPS, SSM_HPG)
    w_dt = jnp.transpose(w_dt, (0, 2, 1, 3)).reshape(d, SSM_GROUPS, 2 * SSM_HPG)
    w_dt = jnp.pad(w_dt, ((0, 0), (0, 0), (0, SSM_CHUNK - 2 * SSM_HPG))).reshape(d, SSM_GROUPS * SSM_CHUNK)
    w_conv = jnp.concatenate([w_in[:, o_xbc:o_dt], w_in[:, o_hy:o_sc]], axis=1).astype(BF16)
    w_plain = jnp.concatenate([w_in[:, :o_xbc], 0.5 * w_in[:, o_gate:], w_in[:, o_sc:o_gate], w_dt],
                              axis=1).astype(BF16)
    return w_conv, w_plain


def _run_trunk(x, meta_tokens, norm_final, layers):
    b, seq, d = x.shape
    t = seq + N_META
    meta = jnp.broadcast_to(meta_tokens[None].astype(x.dtype), (b, N_META, d))
    h = jnp.concatenate([meta, x], axis=1)
    tr = _pick_tile(t, ROW_TILE)
    hy_a, hy_at = _hyena_tables(HYENA_BLOCK)
    hy_nb = -(-t // HYENA_BLOCK)
    tmf = tr
    tok = min(TOKEN_TILE, b * t)
    hv, hx1, hx2 = (CV_HY // BRANCH_W + k for k in range(3))
    for p in layers:
        w_conv, w_plain = p["w_in"]
        taps = jnp.concatenate([p["ssm_conv_w"], p["hyena_conv_w"]], axis=1)
        bias = jnp.concatenate([p["ssm_conv_b"], jnp.zeros((3 * BRANCH_W,), F32)]).reshape(1, N_CONV)
        projc = _conv_proj(h, p["norm_mix"], w_conv, taps, bias, CV_HY, tr)
        proj = _norm_matmul(h.reshape(b * t, d), p["norm_mix"], w_plain, tok, 768).reshape(b, t, N_PLAIN)
        y_fn = _fnet_branch(proj, tmf)
        y_ssm = _ssd_branch(projc, proj, p["ssm_dt_bias"], p["ssm_a_log"], p["ssm_d"], p["ssm_norm"])
        filt = _hyena_filters(t, hy_nb * HYENA_BLOCK, p["hyena_w1"], p["hyena_b1"], p["hyena_w2"], p["hyena_b2"],
                              p["hyena_w3"], p["hyena_freq"])
        spec = _hy_filter_spectrum(hy_a, filt, HYENA_BLOCK, hy_nb)
        z = _hy_long_conv(hy_a, hy_at, spec, 0, projc, hv, projc, hx1, p["hyena_bias"], HYENA_BLOCK)
        y_hy = _hy_long_conv(hy_a, hy_at, spec, 1, z, 0, projc, hx2, p["hyena_bias"], HYENA_BLOCK)
        h = _merge(h, [y_fn, y_ssm, y_hy], proj, p["sc_conv_w"], p["w_branch"], p["w_out"], tr)
        h = _ffn(h, p["norm_ffn"], p["w_up"], p["ffn_conv_w"], p["w_down"], tr)
    return _final_norm(h, norm_final, tr)


def kernel(x_prompt, x_sample, meta_tokens, norm_mix, w_in, ssm_conv_w, ssm_conv_b, ssm_dt_bias, ssm_a_log, ssm_d,
           ssm_norm, hyena_conv_w, hyena_w1, hyena_b1, hyena_w2, hyena_b2, hyena_w3, hyena_freq, hyena_bias,
           sc_conv_w, w_branch, w_out, norm_ffn, ffn_conv_w, w_up, w_down, norm_final):
    depth = w_in.shape[0]
    layers = []
    for l in range(depth):
        layers.append(dict(
            norm_mix=norm_mix[l], w_in=_prep_w_in(w_in[l]), ssm_conv_w=ssm_conv_w[l], ssm_conv_b=ssm_conv_b[l],
            ssm_dt_bias=ssm_dt_bias[l], ssm_a_log=ssm_a_log[l], ssm_d=ssm_d[l], ssm_norm=ssm_norm[l],
            hyena_conv_w=hyena_conv_w[l], hyena_w1=hyena_w1[l], hyena_b1=hyena_b1[l], hyena_w2=hyena_w2[l],
            hyena_b2=hyena_b2[l], hyena_w3=hyena_w3[l], hyena_freq=hyena_freq[l], hyena_bias=hyena_bias[l],
            sc_conv_w=sc_conv_w[l], w_branch=w_branch[l].astype(BF16), w_out=(0.5 * w_out[l]).astype(BF16),
            norm_ffn=norm_ffn[l], ffn_conv_w=ffn_conv_w[l], w_up=w_up[l].astype(BF16), w_down=w_down[l].astype(BF16)))
    y_prompt = _run_trunk(x_prompt, meta_tokens, norm_final, layers)
    y_sample = _run_trunk(x_sample, meta_tokens, norm_final, layers)
    return (y_prompt, y_sample)
```
